```python
import math
import jax, jax.numpy as jnp
from jax import lax
import numpy as np

D_MODEL = 1024
BATCH = 32
SEQ = 256
DEPTH = 4
DEC_BATCH = 4
DEC_SEQ = 2048
PAST_LEN = 256

GRID_W = 64
HEAD_DIM = 64
N_HEADS = D_MODEL // HEAD_DIM
N_KV_HEADS = N_HEADS // 4
Q_BLOCK = 128
ROPE_THETA = 10000.0
CONV_W = D_MODEL // 2
CONV_K = 3
SSM_W = D_MODEL // 2
SSM_GROUP_CH = 16
SSM_GROUPS = SSM_W // SSM_GROUP_CH
SSM_STATE = 64
N_DIR = 2
D_FF = -(-8 * D_MODEL // (3 * 256)) * 256
RMS_EPS = 1e-6
IN_SIZES = (CONV_W, CONV_W, CONV_W, N_HEADS * HEAD_DIM, N_KV_HEADS * HEAD_DIM,
            N_KV_HEADS * HEAD_DIM, SSM_W, D_MODEL, D_MODEL, D_MODEL)
IN_COLS = sum(IN_SIZES)
IN_SPLITS = tuple(int(s) for s in np.cumsum(IN_SIZES)[:-1])

kernel_name = 'hybrid_diffusion_parallel_trunk_step'


def rms_norm(x, g):
    xf = x.astype(jnp.float32)
    y = xf * lax.rsqrt(jnp.mean(xf * xf, axis=-1, keepdims=True) + RMS_EPS)
    return (y * g.astype(jnp.float32)).astype(x.dtype)


def rope_axis(x, pos):
    f = x.shape[-1] // 2
    inv = ROPE_THETA ** (-jnp.arange(f, dtype=jnp.float32) / f)
    ang = pos.astype(jnp.float32)[:, None] * inv[None, :]
    cos = jnp.cos(ang)[None, :, None, :]
    sin = jnp.sin(ang)[None, :, None, :]
    xf = x.astype(jnp.float32)
    x1, x2 = xf[..., :f], xf[..., f:]
    return jnp.concatenate([x1 * cos - x2 * sin, x1 * sin + x2 * cos], axis=-1).astype(x.dtype)


def rope_2d(x):
    length = x.shape[1]
    rows = length // GRID_W
    row = jnp.repeat(jnp.arange(rows), GRID_W)
    col = jnp.tile(jnp.arange(GRID_W), rows)
    half = HEAD_DIM // 2
    return jnp.concatenate([rope_axis(x[..., :half], row), rope_axis(x[..., half:], col)], axis=-1)


def depthwise_conv3(x, w, b):
    y = lax.conv_general_dilated(x, w[:, None, :].astype(x.dtype), window_strides=(1,),
                                 padding=((1, 1),), dimension_numbers=('NWC', 'WIO', 'NWC'),
                                 feature_group_count=x.shape[-1])
    return y + b.astype(x.dtype)


def block_attention(q, k, v):
    bsz, lq, _, _ = q.shape
    rep = N_HEADS // N_KV_HEADS
    nb = lq // Q_BLOCK
    qb = (q * (HEAD_DIM ** -0.5)).reshape(bsz, nb, Q_BLOCK, N_KV_HEADS, rep, HEAD_DIM)
    qb = qb.transpose(1, 0, 2, 3, 4, 5)

    def one_block(qi):
        s = jnp.einsum('bqkrd,bskd->bkrqs', qi, k).astype(jnp.float32)
        p = jax.nn.softmax(s, axis=-1).astype(v.dtype)
        return jnp.einsum('bkrqs,bskd->bqkrd', p, v)

    o = lax.map(one_block, qb)
    return o.transpose(1, 0, 2, 3, 4, 5).reshape(bsz, lq, N_HEADS * HEAD_DIM)


def _complex_affine_combine(e1, e2):
    a1r, a1i, b1r, b1i = e1
    a2r, a2i, b2r, b2i = e2
    return (a2r * a1r - a2i * a1i,
            a2r * a1i + a2i * a1r,
            a2r * b1r - a2i * b1i + b2r,
            a2r * b1i + a2i * b1r + b2i)


def ssm_direction(u, h0_re, h0_im, lam_re, lam_im, b_re, b_im, c_re, c_im, log_dt):
    f32 = jnp.float32
    lam_re = lam_re.astype(f32)
    lam_im = lam_im.astype(f32)
    dt = jnp.exp(log_dt.astype(f32))[:, None]
    mag = jnp.exp(lam_re * dt)
    a_re = mag * jnp.cos(lam_im * dt)
    a_im = mag * jnp.sin(lam_im * dt)
    den = lam_re * lam_re + lam_im * lam_im
    n_re = a_re - 1.0
    coef_re = (n_re * lam_re + a_im * lam_im) / den
    coef_im = (a_im * lam_re - n_re * lam_im) / den
    b_re = b_re.astype(f32)
    b_im = b_im.astype(f32)
    bb_re = coef_re[..., None] * b_re - coef_im[..., None] * b_im
    bb_im = coef_re[..., None] * b_im + coef_im[..., None] * b_re
    x_re = jnp.einsum('blgc,gpc->blgp', u, bb_re)
    x_im = jnp.einsum('blgc,gpc->blgp', u, bb_im)
    h0_re = h0_re.astype(f32)
    h0_im = h0_im.astype(f32)
    x_re = x_re.at[:, 0].add(a_re * h0_re - a_im * h0_im)
    x_im = x_im.at[:, 0].add(a_re * h0_im + a_im * h0_re)
    ar = jnp.broadcast_to(a_re, x_re.shape)
    ai = jnp.broadcast_to(a_im, x_im.shape)
    _, _, h_re, h_im = lax.associative_scan(_complex_affine_combine, (ar, ai, x_re, x_im), axis=1)
    y = (jnp.einsum('gcp,blgp->blgc', c_re.astype(f32), h_re)
         - jnp.einsum('gcp,blgp->blgc', c_im.astype(f32), h_im))
    return y, h_re[:, -1], h_im[:, -1]


def _dir_params(p, d):
    return (p['ssm_lambda_re'][d], p['ssm_lambda_im'][d], p['ssm_b_re'][d], p['ssm_b_im'][d],
            p['ssm_c_re'][d], p['ssm_c_im'][d], p['ssm_log_dt'][d])


def ssm_branch(u, h0_re, h0_im, p):
    bsz, length, _ = u.shape
    uf = u.astype(jnp.float32)
    ug = uf.reshape(bsz, length, SSM_GROUPS, SSM_GROUP_CH)
    y_f, hf_re, hf_im = ssm_direction(ug, h0_re[:, 0], h0_im[:, 0], *_dir_params(p, 0))
    y_b, hb_re, hb_im = ssm_direction(ug[:, ::-1], h0_re[:, 1], h0_im[:, 1], *_dir_params(p, 1))
    y = (y_f + y_b[:, ::-1]).reshape(bsz, length, SSM_W) + p['ssm_d'].astype(jnp.float32) * uf
    y = jax.nn.gelu(y)
    y = y * jax.nn.sigmoid(y @ p['w_glu'].astype(jnp.float32) + p['b_glu'].astype(jnp.float32))
    return y.astype(u.dtype), jnp.stack([hf_re, hb_re], axis=1), jnp.stack([hf_im, hb_im], axis=1)


def trunk_layer(x, mod, p, ctx=None):
    bsz, length, _ = x.shape
    shift1, scale1, gate1, shift2, scale2, gate2 = jnp.split(mod, 6, axis=-1)
    h = rms_norm(x, p['norm1_g']) * (1 + scale1) + shift1
    z = h @ p['w_in']
    cb, cc, cx, q, k, v, u, g_conv, g_attn, g_ssm = jnp.split(z, IN_SPLITS, axis=-1)
    y_conv = cb * depthwise_conv3(cc * cx, p['conv_w'], p['conv_b'])
    q = rms_norm(q.reshape(bsz, length, N_HEADS, HEAD_DIM), p['q_norm_g'])
    k = rms_norm(k.reshape(bsz, length, N_KV_HEADS, HEAD_DIM), p['k_norm_g'])
    v = v.reshape(bsz, length, N_KV_HEADS, HEAD_DIM)
    if ctx is None:
        k_att, v_att = k, v
        h0_re = jnp.zeros((bsz, N_DIR, SSM_GROUPS, SSM_STATE), jnp.float32)
        h0_im = jnp.zeros((bsz, N_DIR, SSM_GROUPS, SSM_STATE), jnp.float32)
    else:
        k_ctx, v_ctx, h0_re, h0_im = ctx
        q = rope_2d(q)
        k_att = jnp.concatenate([rope_2d(k), k_ctx.astype(k.dtype)], axis=1)
        v_att = jnp.concatenate([v, v_ctx.astype(v.dtype)], axis=1)
    y_attn = block_attention(q, k_att, v_att)
    y_ssm, hT_re, hT_im = ssm_branch(u, h0_re, h0_im, p)
    merged = (jax.nn.sigmoid(g_conv) * (y_conv @ p['w_proj_conv'])
              + jax.nn.sigmoid(g_attn) * (y_attn @ p['w_proj_attn'])
              + jax.nn.sigmoid(g_ssm) * (y_ssm @ p['w_proj_ssm']))
    x = x + gate1 * (merged @ p['w_out'])
    h = rms_norm(x, p['norm2_g']) * (1 + scale2) + shift2
    x = x + gate2 * ((jax.nn.silu(h @ p['w_ffn_gate']) * (h @ p['w_ffn_up'])) @ p['w_ffn_down'])
    if ctx is None:
        return x, (k, v, hT_re, hT_im)
    return x, None


def setup_inputs(seed: int = 0) -> dict:
    key = jax.random.key(seed)
    keys = iter(jax.random.split(key, 48))
    f32 = jnp.float32

    def nrm(shape, scale):
        return jax.random.normal(next(keys), shape, f32) * scale

    ssm_shape = (DEPTH, N_DIR, SSM_GROUPS, SSM_STATE)
    n_idx = jnp.arange(SSM_STATE, dtype=f32)
    return {
        'x_prompt': nrm((BATCH, SEQ, D_MODEL), 1.0),
        'x_sample': nrm((DEC_BATCH, DEC_SEQ, D_MODEL), 1.0),
        'c': nrm((DEC_BATCH, D_MODEL), 1.0),
        'cache_k': nrm((DEC_BATCH, DEPTH, PAST_LEN, N_KV_HEADS, HEAD_DIM), 1.0),
        'cache_v': nrm((DEC_BATCH, DEPTH, PAST_LEN, N_KV_HEADS, HEAD_DIM), 1.0),
        'state_ssm_re': nrm((DEC_BATCH, DEPTH, N_DIR, SSM_GROUPS, SSM_STATE), 0.5),
        'state_ssm_im': nrm((DEC_BATCH, DEPTH, N_DIR, SSM_GROUPS, SSM_STATE), 0.5),
        'c_ctx': nrm((D_MODEL,), 1.0),
        'w_ada': nrm((DEPTH, D_MODEL, 6 * D_MODEL), D_MODEL ** -0.5),
        'b_ada': nrm((DEPTH, 6 * D_MODEL), 0.02),
        'norm1_g': 1.0 + nrm((DEPTH, D_MODEL), 0.02),
        'w_in': nrm((DEPTH, D_MODEL, IN_COLS), D_MODEL ** -0.5),
        'conv_w': nrm((DEPTH, CONV_K, CONV_W), CONV_K ** -0.5),
        'conv_b': nrm((DEPTH, CONV_W), 0.02),
        'q_norm_g': 1.0 + nrm((DEPTH, HEAD_DIM), 0.02),
        'k_norm_g': 1.0 + nrm((DEPTH, HEAD_DIM), 0.02),
        'ssm_lambda_re': -0.5 + nrm(ssm_shape, 0.01),
        'ssm_lambda_im': math.pi * n_idx + nrm(ssm_shape, 0.01),
        'ssm_b_re': nrm(ssm_shape + (SSM_GROUP_CH,), (2 * SSM_GROUP_CH) ** -0.5),
        'ssm_b_im': nrm(ssm_shape + (SSM_GROUP_CH,), (2 * SSM_GROUP_CH) ** -0.5),
        'ssm_c_re': nrm((DEPTH, N_DIR, SSM_GROUPS, SSM_GROUP_CH, SSM_STATE), (2 * SSM_STATE) ** -0.5),
        'ssm_c_im': nrm((DEPTH, N_DIR, SSM_GROUPS, SSM_GROUP_CH, SSM_STATE), (2 * SSM_STATE) ** -0.5),
        'ssm_log_dt': jax.random.uniform(next(keys), (DEPTH, N_DIR, SSM_GROUPS), f32,
                                         math.log(1e-3), math.log(1e-1)),
        'ssm_d': nrm((DEPTH, SSM_W), 1.0),
        'w_glu': nrm((DEPTH, SSM_W, SSM_W), SSM_W ** -0.5),
        'b_glu': nrm((DEPTH, SSM_W), 0.02),
        'w_proj_conv': nrm((DEPTH, CONV_W, D_MODEL), CONV_W ** -0.5),
        'w_proj_attn': nrm((DEPTH, N_HEADS * HEAD_DIM, D_MODEL), (N_HEADS * HEAD_DIM) ** -0.5),
        'w_proj_ssm': nrm((DEPTH, SSM_W, D_MODEL), SSM_W ** -0.5),
        'w_out': nrm((DEPTH, D_MODEL, D_MODEL), D_MODEL ** -0.5),
        'norm2_g': 1.0 + nrm((DEPTH, D_MODEL), 0.02),
        'w_ffn_gate': nrm((DEPTH, D_MODEL, D_FF), D_MODEL ** -0.5),
        'w_ffn_up': nrm((DEPTH, D_MODEL, D_FF), D_MODEL ** -0.5),
        'w_ffn_down': nrm((DEPTH, D_FF, D_MODEL), D_FF ** -0.5),
        'final_norm_g': 1.0 + nrm((D_MODEL,), 0.02),
    }


def reference(x_prompt, x_sample, c, cache_k, cache_v, state_ssm_re, state_ssm_im, c_ctx,
              w_ada, b_ada, norm1_g, w_in, conv_w, conv_b, q_norm_g, k_norm_g,
              ssm_lambda_re, ssm_lambda_im, ssm_b_re, ssm_b_im, ssm_c_re, ssm_c_im,
              ssm_log_dt, ssm_d, w_glu, b_glu, w_proj_conv, w_proj_attn, w_proj_ssm,
              w_out, norm2_g, w_ffn_gate, w_ffn_up, w_ffn_down, final_norm_g):
    def layer_params(l):
        return {
            'norm1_g': norm1_g[l], 'w_in': w_in[l], 'conv_w': conv_w[l], 'conv_b': conv_b[l],
            'q_norm_g': q_norm_g[l], 'k_norm_g': k_norm_g[l],
            'ssm_lambda_re': ssm_lambda_re[l], 'ssm_lambda_im': ssm_lambda_im[l],
            'ssm_b_re': ssm_b_re[l], 'ssm_b_im': ssm_b_im[l],
            'ssm_c_re': ssm_c_re[l], 'ssm_c_im': ssm_c_im[l],
            'ssm_log_dt': ssm_log_dt[l], 'ssm_d': ssm_d[l], 'w_glu': w_glu[l], 'b_glu': b_glu[l],
            'w_proj_conv': w_proj_conv[l], 'w_proj_attn': w_proj_attn[l],
            'w_proj_ssm': w_proj_ssm[l], 'w_out': w_out[l], 'norm2_g': norm2_g[l],
            'w_ffn_gate': w_ffn_gate[l], 'w_ffn_up': w_ffn_up[l], 'w_ffn_down': w_ffn_down[l],
        }

    xp = x_prompt
    ks, vs, srs, sis = [], [], [], []
    for l in range(DEPTH):
        mod_ctx = (jax.nn.silu(c_ctx) @ w_ada[l] + b_ada[l])[None, None, :]
        xp, (k_l, v_l, sr_l, si_l) = trunk_layer(xp, mod_ctx, layer_params(l), None)
        ks.append(k_l)
        vs.append(v_l)
        srs.append(sr_l)
        sis.append(si_l)
    y_prompt = rms_norm(xp, final_norm_g)
    new_cache_k = jnp.stack(ks, axis=1)
    new_cache_v = jnp.stack(vs, axis=1)
    new_state_ssm_re = jnp.stack(srs, axis=1)
    new_state_ssm_im = jnp.stack(sis, axis=1)

    xs = x_sample
    for l in range(DEPTH):
        mod_lat = (jax.nn.silu(c) @ w_ada[l] + b_ada[l])[:, None, :]
        ctx = (cache_k[:, l], cache_v[:, l], state_ssm_re[:, l], state_ssm_im[:, l])
        xs, _ = trunk_layer(xs, mod_lat, layer_params(l), ctx)
    y_sample = rms_norm(xs, final_norm_g)

    return (y_prompt, y_sample, new_cache_k, new_cache_v, new_state_ssm_re, new_state_ssm_im)
```

```python
import functools
import math

import jax
import jax.numpy as jnp
import numpy as np
from jax import lax
from jax.experimental import pallas as pl
from jax.experimental.pallas import tpu as pltpu

F32 = jnp.float32
BF16 = jnp.bfloat16

D_MODEL = 1024
DEPTH = 4
GRID_W = 64
HEAD_DIM = 64
N_HEADS = 16
N_KV_HEADS = 4
ROPE_THETA = 10000.0
CONV_W = 512
SSM_W = 512
SSM_GROUP_CH = 16
SSM_GROUPS = 32
SSM_STATE = 64
D_FF = 2816
RMS_EPS = 1e-6
IN_COLS = 6656

LANES = 128
SUBLANES = 8
VMEM_LIMIT = 56 * 1024 * 1024

Z_TILE = 512
Z_U_TILE = 6144 // Z_TILE
HALF_STATE = SSM_GROUPS // 2 * SSM_STATE
SSM_ROWS = 512


def _cparams(sem):
    return pltpu.CompilerParams(dimension_semantics=sem, vmem_limit_bytes=VMEM_LIMIT)


def _mod_norm(x, g, shift, scale):
    ms = jnp.mean(x * x, axis=-1, keepdims=True)
    y = x * lax.rsqrt(ms + RMS_EPS) * g
    return y * (1.0 + scale) + shift


def _ada_kernel(c_ref, w_ref, b_ref, o_ref):
    s = jax.nn.silu(c_ref[...])
    o_ref[...] = jnp.dot(s.astype(BF16), w_ref[...].astype(BF16),
                         preferred_element_type=F32) + b_ref[...]


def _ada_call(cvec, w_ada, b_ada):
    tn = 1536
    return pl.pallas_call(
        _ada_kernel,
        grid=(DEPTH, 6 * D_MODEL // tn),
        in_specs=[pl.BlockSpec((8, D_MODEL), lambda l, j: (0, 0)),
                  pl.BlockSpec((None, D_MODEL, tn), lambda l, j: (l, 0, j)),
                  pl.BlockSpec((None, 1, tn), lambda l, j: (l, 0, j))],
        out_specs=pl.BlockSpec((None, 8, tn), lambda l, j: (l, 0, j)),
        out_shape=jax.ShapeDtypeStruct((DEPTH, 8, 6 * D_MODEL), F32),
        compiler_params=_cparams(("arbitrary", "arbitrary")),
        name="ada_mod",
    )(cvec, w_ada, b_ada.reshape(DEPTH, 1, 6 * D_MODEL))


def _disc_kernel(lre_ref, lim_ref, ldt_ref, bre_ref, bim_ref,
                 are_ref, aim_ref, bbre_ref, bbim_ref):
    lre = lre_ref[...]
    lim = lim_ref[...]
    dt = jnp.exp(ldt_ref[...])
    mag = jnp.exp(lre * dt)
    a_re = mag * jnp.cos(lim * dt)
    a_im = mag * jnp.sin(lim * dt)
    den = lre * lre + lim * lim
    n_re = a_re - 1.0
    coef_re = (n_re * lre + a_im * lim) / den
    coef_im = (a_im * lre - n_re * lim) / den
    are_ref[...] = a_re
    aim_ref[...] = a_im
    bre = bre_ref[...]
    bim = bim_ref[...]
    bbre_ref[...] = coef_re * bre - coef_im * bim
    bbim_ref[...] = coef_re * bim + coef_im * bre


def _disc_call(lam_re, lam_im, log_dt, b_re, b_im):
    n = DEPTH * 2 * SSM_GROUPS
    lre = lam_re.reshape(n, 1, SSM_STATE)
    lim = lam_im.reshape(n, 1, SSM_STATE)
    ldt = log_dt.reshape(n, 1, 1)
    bre = jnp.swapaxes(b_re.reshape(n, SSM_STATE, SSM_GROUP_CH), 1, 2)
    bim = jnp.swapaxes(b_im.reshape(n, SSM_STATE, SSM_GROUP_CH), 1, 2)
    small = jax.ShapeDtypeStruct((n, 1, SSM_STATE), F32)
    big = jax.ShapeDtypeStruct((n, SSM_GROUP_CH, SSM_STATE), F32)
    return pl.pallas_call(
        _disc_kernel, out_shape=(small, small, big, big), name="ssm_disc",
    )(lre, lim, ldt, bre, bim)


def _ssm_weights(a_re, a_im, bb_re, bb_im, c_re, c_im):
    def bb_tiles(bb):
        return bb.reshape(DEPTH, 2, 2, 4, 4, SSM_GROUP_CH, SSM_STATE)
    bb = jnp.stack([bb_tiles(bb_re), bb_tiles(bb_im)], axis=2)
    sel = np.zeros((4, 8, 4), np.float32)
    for m in range(4):
        for j in range(4):
            sel[m, 4 * (m % 2) + j, j] = 1.0
    bbt = jnp.einsum('ldrhmjcp,mkj->ldrmhkcjp', bb, jnp.asarray(sel))
    bbt = bbt.reshape(DEPTH, 2, 8, 2 * LANES, 2 * LANES).astype(BF16)
    def c_tiles(c):
        return c.reshape(DEPTH, 2, 2, 2, 8, SSM_GROUP_CH, SSM_STATE)
    cc = jnp.stack([c_tiles(c_re), -c_tiles(c_im)], axis=2)
    eye = jnp.eye(8, dtype=F32)
    cct = jnp.einsum('ldrhmkop,kj->ldmrkphjo', cc, eye)
    cct = cct.reshape(DEPTH, 2, 2, 2 * 8 * SSM_STATE, 2 * LANES).astype(BF16)
    def a_rows(a):
        a = a.reshape(DEPTH, 2, 2, HALF_STATE)
        return jnp.tile(a, (1, 1, 4, 1))
    a8 = jnp.stack([a_rows(a_re), a_rows(a_im)], axis=2)
    return bbt, cct, a8


def _inproj_kernel(x_ref, mod_ref, g_ref, w_ref, z_ref, utb_ref, h_scr, *, nsub):
    j = pl.program_id(1)

    @pl.when(j == 0)
    def _():
        h = _mod_norm(x_ref[...], g_ref[...], mod_ref[:, 0:D_MODEL], mod_ref[:, D_MODEL:2 * D_MODEL])
        h_scr[...] = h.astype(BF16)

    zt = jnp.dot(h_scr[...], w_ref[...], preferred_element_type=F32).astype(BF16)
    z_ref[...] = zt

    @pl.when(j == Z_U_TILE)
    def _():
        sub = zt.shape[0] // nsub
        for s in range(nsub):
            utb_ref[:, s * SSM_W:(s + 1) * SSM_W] = zt[s * sub:(s + 1) * sub]


def _inproj_call(x, mod, g1, w_in, *, nb, seq, tm):
    t = x.shape[0]
    n_m = t // tm
    if tm >= seq:
        nsub = tm // seq
        utb_spec = pl.BlockSpec((seq, nsub * SSM_W), lambda i, j: (0, i))
        mod_map = lambda i, j: (0, 0, 0)
    else:
        nsub = 1
        per = seq // tm
        utb_spec = pl.BlockSpec((tm, SSM_W), lambda i, j: (i % per, i // per))
        mod_map = lambda i, j: (i // per, 0, 0)
    return pl.pallas_call(
        functools.partial(_inproj_kernel, nsub=nsub),
        grid=(n_m, IN_COLS // Z_TILE),
        in_specs=[pl.BlockSpec((tm, D_MODEL), lambda i, j: (i, 0)),
                  pl.BlockSpec((None, 1, 6 * D_MODEL), mod_map),
                  pl.BlockSpec((1, D_MODEL), lambda i, j: (0, 0)),
                  pl.BlockSpec((D_MODEL, Z_TILE), lambda i, j: (0, j))],
        out_specs=[pl.BlockSpec((tm, Z_TILE), lambda i, j: (i, j)), utb_spec],
        out_shape=[jax.ShapeDtypeStruct((t, IN_COLS), BF16),
                   jax.ShapeDtypeStruct((seq, nb * SSM_W), BF16)],
        scratch_shapes=[pltpu.VMEM((tm, D_MODEL), BF16)],
        compiler_params=_cparams(("arbitrary", "arbitrary")),
        name="in_proj",
    )(x, mod, g1, w_in)


def _head_ssq(x, bd):
    parts = []
    for t in range(x.shape[1] // (2 * LANES)):
        xs = x[:, 2 * LANES * t:2 * LANES * (t + 1)]
        parts.append(jnp.dot((xs * xs).astype(BF16), bd, preferred_element_type=F32))
    return parts[0] if len(parts) == 1 else jnp.concatenate(parts, axis=1)


def _rope(x, cos, sin_up, sin_dn):
    w = x.shape[1]
    rep = w // LANES
    cos = jnp.tile(cos, (1, rep))
    sin_up = jnp.tile(sin_up, (1, rep))
    sin_dn = jnp.tile(sin_dn, (1, rep))
    return x * cos + pltpu.roll(x, w - 16, 1) * sin_up + pltpu.roll(x, 16, 1) * sin_dn


def _dup_heads(x, lane_lo):
    xr = pltpu.roll(x, HEAD_DIM, 1)
    return jnp.where(lane_lo, x, xr), jnp.where(lane_lo, xr, x)


def _attn_kernel(*refs, seq, past, rope, emit_kv):
    it = iter(refs)
    q_ref, kv_ref = next(it), next(it)
    kc_ref = vc_ref = cos_ref = sup_ref = sdn_ref = cosk_ref = supk_ref = sdnk_ref = None
    if past:
        kc_ref, vc_ref = next(it), next(it)
    qg_ref, kg_ref, bd_ref = next(it), next(it), next(it)
    if rope:
        cos_ref, sup_ref, sdn_ref = next(it), next(it), next(it)
        cosk_ref, supk_ref, sdnk_ref = next(it), next(it), next(it)
    o_ref = next(it)
    kout_ref = vout_ref = None
    if emit_kv:
        kout_ref, vout_ref = next(it), next(it)
    k2_scr, v2_scr = next(it), next(it)

    qi = pl.program_id(1)
    bd = bd_ref[...]

    @pl.when(qi == 0)
    def _prep():
        kv = kv_ref[...].astype(F32)
        k = kv[:, :2 * LANES]
        v = kv[:, 2 * LANES:]
        kn = k * lax.rsqrt(_head_ssq(k, bd) * (1.0 / HEAD_DIM) + RMS_EPS) * kg_ref[...]
        if emit_kv:
            kout_ref[...] = kn
            vout_ref[...] = v
        if rope:
            kn = _rope(kn, cosk_ref[...], supk_ref[...], sdnk_ref[...])
        lane_lo = lax.broadcasted_iota(jnp.int32, (seq, LANES), 1) < HEAD_DIM
        for t in range(2):
            ka, kb = _dup_heads(kn[:, LANES * t:LANES * (t + 1)], lane_lo)
            va, vb = _dup_heads(v[:, LANES * t:LANES * (t + 1)], lane_lo)
            k2_scr[2 * t, 0:seq, :] = ka.astype(BF16)
            k2_scr[2 * t + 1, 0:seq, :] = kb.astype(BF16)
            v2_scr[2 * t, 0:seq, :] = va.astype(BF16)
            v2_scr[2 * t + 1, 0:seq, :] = vb.astype(BF16)
        if past:
            kc = kc_ref[...]
            vc = vc_ref[...]
            lane_lo_p = lax.broadcasted_iota(jnp.int32, (past, LANES), 1) < HEAD_DIM
            for t in range(2):
                ka, kb = _dup_heads(kc[:, LANES * t:LANES * (t + 1)], lane_lo_p)
                va, vb = _dup_heads(vc[:, LANES * t:LANES * (t + 1)], lane_lo_p)
                k2_scr[2 * t, seq:seq + past, :] = ka.astype(BF16)
                k2_scr[2 * t + 1, seq:seq + past, :] = kb.astype(BF16)
                v2_scr[2 * t, seq:seq + past, :] = va.astype(BF16)
                v2_scr[2 * t + 1, seq:seq + past, :] = vb.astype(BF16)

    q = q_ref[...].astype(F32)
    tq = q.shape[0]
    qn = q * lax.rsqrt(_head_ssq(q, bd) * (1.0 / HEAD_DIM) + RMS_EPS) * qg_ref[...]
    if rope:
        qn = _rope(qn, cos_ref[...], sup_ref[...], sdn_ref[...])
    qn = qn * (HEAD_DIM ** -0.5)
    lane_lo = lax.broadcasted_iota(jnp.int32, (tq, LANES), 1) < HEAD_DIM
    for t in range(N_HEADS // 2):
        qt = qn[:, LANES * t:LANES * (t + 1)]
        kvh = t // 2
        outs = []
        for half in range(2):
            qm = jnp.where(lane_lo if half == 0 else jnp.logical_not(lane_lo), qt, 0.0).astype(BF16)
            s = lax.dot_general(qm, k2_scr[kvh], (((1,), (1,)), ((), ())),
                                preferred_element_type=F32)
            m = jnp.max(s, axis=-1, keepdims=True)
            p = jnp.exp(s - m)
            l = jnp.sum(p, axis=-1, keepdims=True)
            o2 = jnp.dot(p.astype(BF16), v2_scr[kvh], preferred_element_type=F32)
            outs.append(o2 * (1.0 / l))
        o_ref[:, LANES * t:LANES * (t + 1)] = jnp.where(lane_lo, outs[0], outs[1]).astype(BF16)


def _attn_call(z, kc, vc, qg, kg, bd, tabs, *, nb, seq, tq, past, emit_kv):
    rope = tabs is not None
    nq = seq // tq
    keys = seq + past
    q_col = 3072 // D_MODEL
    kv_col = 4096 // 512
    in_specs = [pl.BlockSpec((tq, D_MODEL), lambda b, i: (b * nq + i, q_col)),
                pl.BlockSpec((seq, 512), lambda b, i: (b, kv_col))]
    args = [z, z]
    if past:
        in_specs += [pl.BlockSpec((None, past, 2 * LANES), lambda b, i: (b, 0, 0))] * 2
        args += [kc, vc]
    in_specs += [pl.BlockSpec((1, D_MODEL), lambda b, i: (0, 0)),
                 pl.BlockSpec((1, 2 * LANES), lambda b, i: (0, 0)),
                 pl.BlockSpec((2 * LANES, 2 * LANES), lambda b, i: (0, 0))]
    args += [qg, kg, bd]
    if rope:
        in_specs += [pl.BlockSpec((tq, LANES), lambda b, i: (i, 0))] * 3
        in_specs += [pl.BlockSpec((seq, LANES), lambda b, i: (0, 0))] * 3
        args += list(tabs) + list(tabs)
    out_specs = [pl.BlockSpec((tq, D_MODEL), lambda b, i: (b * nq + i, 0))]
    out_shape = [jax.ShapeDtypeStruct((nb * seq, D_MODEL), BF16)]
    if emit_kv:
        out_specs += [pl.BlockSpec((None, seq, 2 * LANES), lambda b, i: (b, 0, 0))] * 2
        out_shape += [jax.ShapeDtypeStruct((nb, seq, 2 * LANES), F32)] * 2
    return pl.pallas_call(
        functools.partial(_attn_kernel, seq=seq, past=past, rope=rope, emit_kv=emit_kv),
        grid=(nb, nq),
        in_specs=in_specs, out_specs=out_specs, out_shape=out_shape,
        scratch_shapes=[pltpu.VMEM((N_KV_HEADS, keys, LANES), BF16),
                        pltpu.VMEM((N_KV_HEADS, keys, LANES), BF16)],
        compiler_params=_cparams(("arbitrary", "arbitrary")),
        name="attention",
    )(*args)


def _rope_tables(length):
    f = HEAD_DIM // 4
    inv = ROPE_THETA ** (-np.arange(f, dtype=np.float32) / f)
    pos = np.arange(length)
    row = (pos // GRID_W).astype(np.float32)
    col = (pos % GRID_W).astype(np.float32)
    ang_r = row[:, None] * inv[None, :]
    ang_c = col[:, None] * inv[None, :]
    z = np.zeros_like(ang_r)
    cos64 = np.concatenate([np.cos(ang_r), np.cos(ang_r), np.cos(ang_c), np.cos(ang_c)], axis=1)
    up64 = np.concatenate([-np.sin(ang_r), z, -np.sin(ang_c), z], axis=1)
    dn64 = np.concatenate([z, np.sin(ang_r), z, np.sin(ang_c)], axis=1)
    tile2 = lambda a: jnp.asarray(np.tile(a, (1, 2)), dtype=F32)
    return tile2(cos64), tile2(up64), tile2(dn64)


def _ssm_kernel(uf_ref, ub_ref, bbt_ref, cct_ref, a_ref, h0_ref,
                yf_ref, yb_ref, ht_ref, x_scr, hc_scr, *, rows_per_step, steps):
    j = pl.program_id(0)
    r = rows_per_step
    rows = r * steps

    @pl.when(j == 0)
    def _():
        hc_scr[...] = h0_ref[...]

    first_half = (lax.broadcasted_iota(jnp.int32, (rows, LANES), 0) & 1) == 0
    width = SUBLANES * HALF_STATE // r
    for d, (u_ref, y_ref) in enumerate(((uf_ref, yf_ref), (ub_ref, yb_ref))):
        u = u_ref[...].astype(F32)
        kcat = []
        for q in range(2):
            uq = u[:, LANES * q:LANES * (q + 1)]
            kcat.append(jnp.concatenate([jnp.where(first_half, uq, 0.0),
                                         jnp.where(first_half, 0.0, uq)], axis=1).astype(BF16))
        for n in range(8):
            x_scr[:, 2 * LANES * n:2 * LANES * (n + 1)] = jnp.dot(
                kcat[(n % 4) // 2], bbt_ref[d, n], preferred_element_type=F32)

        for p in range(HALF_STATE // width):
            l0 = p * width
            reps = r // SUBLANES
            a_re = jnp.tile(a_ref[d, 0, :, l0:l0 + width], (reps, 1))
            a_im = jnp.tile(a_ref[d, 1, :, l0:l0 + width], (reps, 1))

            def step(kk, carry, d=d, l0=l0, a_re=a_re, a_im=a_im):
                h_re, h_im = carry
                k = kk if d == 0 else steps - 1 - kk
                r0 = pl.multiple_of(k * r, SUBLANES)
                x_re = x_scr[pl.ds(r0, r), l0:l0 + width]
                x_im = x_scr[pl.ds(r0, r), HALF_STATE + l0:HALF_STATE + l0 + width]
                n_re = a_re * h_re - a_im * h_im + x_re
                n_im = a_re * h_im + a_im * h_re + x_im
                x_scr[pl.ds(r0, r), l0:l0 + width] = n_re
                x_scr[pl.ds(r0, r), HALF_STATE + l0:HALF_STATE + l0 + width] = n_im
                return n_re, n_im

            h_re, h_im = lax.fori_loop(
                0, steps, step,
                (hc_scr[d, :, l0:l0 + width], hc_scr[d, :, HALF_STATE + l0:HALF_STATE + l0 + width]))
            hc_scr[d, :, l0:l0 + width] = h_re
            hc_scr[d, :, HALF_STATE + l0:HALF_STATE + l0 + width] = h_im

        for m in range(2):
            hk = jnp.concatenate(
                [x_scr[:, 512 * m:512 * (m + 1)],
                 x_scr[:, HALF_STATE + 512 * m:HALF_STATE + 512 * (m + 1)]], axis=1).astype(BF16)
            yy = jnp.dot(hk, cct_ref[d, m], preferred_element_type=F32)
            y_ref[:, LANES * m:LANES * (m + 1)] = jnp.where(first_half, yy[:, :LANES], yy[:, LANES:])

    @pl.when(j == pl.num_programs(0) - 1)
    def _():
        ht_ref[...] = hc_scr[...]


def _ssm_call(utb, bbt, cct, a8, h0, *, nb, seq):
    r = 2 * nb
    steps = SSM_ROWS // r
    n = seq // steps
    u2 = utb.reshape(seq * r, 2 * LANES)
    ys = jax.ShapeDtypeStruct((seq * r, 2 * LANES), F32)
    blk = (SSM_ROWS, 2 * LANES)
    return pl.pallas_call(
        functools.partial(_ssm_kernel, rows_per_step=r, steps=steps),
        grid=(n,),
        in_specs=[pl.BlockSpec(blk, lambda j: (j, 0)),
                  pl.BlockSpec(blk, lambda j: (n - 1 - j, 0)),
                  pl.BlockSpec(bbt.shape, lambda j: (0, 0, 0, 0)),
                  pl.BlockSpec(cct.shape, lambda j: (0, 0, 0, 0)),
                  pl.BlockSpec(a8.shape, lambda j: (0, 0, 0, 0)),
                  pl.BlockSpec(h0.shape, lambda j: (0, 0, 0))],
        out_specs=[pl.BlockSpec(blk, lambda j: (j, 0)),
                   pl.BlockSpec(blk, lambda j: (n - 1 - j, 0)),
                   pl.BlockSpec(h0.shape, lambda j: (0, 0, 0))],
        out_shape=[ys, ys, jax.ShapeDtypeStruct(h0.shape, F32)],
        scratch_shapes=[pltpu.VMEM((SSM_ROWS, 2 * HALF_STATE), F32),
                        pltpu.VMEM(h0.shape, F32)],
        compiler_params=_cparams(("arbitrary",)),
        name="ssm_scan",
    )(u2, u2, bbt, cct, a8, h0)


def _merge_kernel(x_ref, mod_ref, zg_ref, zc_ref, zu_ref, cprev_ref, cnext_ref,
                  ya_ref, yf_ref, yb_ref, cw_ref, cb_ref, sd_ref, wglu_ref, bglu_ref,
                  wpc_ref, wpa_ref, wps_ref, wo_ref, o_ref, *, seq):
    i = pl.program_id(0)
    tm = x_ref.shape[0]
    zc = zc_ref[...].astype(F32)
    cb = zc[:, 0:CONV_W]
    prod = zc[:, CONV_W:2 * CONV_W] * zc[:, 2 * CONV_W:3 * CONV_W]
    cp = cprev_ref[...].astype(F32)
    cn = cnext_ref[...].astype(F32)
    halo_prev = cp[7:8, CONV_W:2 * CONV_W] * cp[7:8, 2 * CONV_W:3 * CONV_W]
    halo_next = cn[0:1, CONV_W:2 * CONV_W] * cn[0:1, 2 * CONV_W:3 * CONV_W]
    row = lax.broadcasted_iota(jnp.int32, (tm, CONV_W), 0)
    tpos = (i * tm + row) % seq
    prev = jnp.where(row == 0, halo_prev, pltpu.roll(prod, 1, 0))
    prev = jnp.where(tpos == 0, 0.0, prev)
    nxt = jnp.where(row == tm - 1, halo_next, pltpu.roll(prod, tm - 1, 0))
    nxt = jnp.where(tpos == seq - 1, 0.0, nxt)
    cw = cw_ref[...]
    y_conv = cb * (cw[0:1] * prev + cw[1:2] * prod + cw[2:3] * nxt + cb_ref[...])

    u = zu_ref[...].astype(F32)
    y = jax.nn.gelu(yf_ref[...] + yb_ref[...] + sd_ref[...] * u)
    y_ssm = y * jax.nn.sigmoid(
        jnp.dot(y.astype(BF16), wglu_ref[...], preferred_element_type=F32) + bglu_ref[...])

    zg = zg_ref[...].astype(F32)
    merged = (jax.nn.sigmoid(zg[:, 0:D_MODEL])
              * jnp.dot(y_conv.astype(BF16), wpc_ref[...], preferred_element_type=F32)
              + jax.nn.sigmoid(zg[:, D_MODEL:2 * D_MODEL])
              * jnp.dot(ya_ref[...], wpa_ref[...], preferred_element_type=F32)
              + jax.nn.sigmoid(zg[:, 2 * D_MODEL:3 * D_MODEL])
              * jnp.dot(y_ssm.astype(BF16), wps_ref[...], preferred_element_type=F32))
    gate1 = mod_ref[:, 2 * D_MODEL:3 * D_MODEL]
    o_ref[...] = x_ref[...] + gate1 * jnp.dot(merged.astype(BF16), wo_ref[...],
                                              preferred_element_type=F32)


def _merge_call(x, mod, z, y_attn, yf, yb, lp, *, nb, seq, tm):
    t = x.shape[0]
    n_m = t // tm
    per = max(seq // tm, 1)
    hb = tm // SUBLANES
    n_hb = t // SUBLANES
    if tm >= seq:
        nsub = tm // seq
        y_spec = pl.BlockSpec((seq, nsub * SSM_W), lambda i: (0, i))
        mod_map = lambda i: (0, 0, 0)
        assert nsub == 1
    else:
        y_spec = pl.BlockSpec((tm, SSM_W), lambda i: (i % per, i // per))
        mod_map = lambda i: (i // per, 0, 0)
    yf2 = yf.reshape(seq, nb * SSM_W)
    yb2 = yb.reshape(seq, nb * SSM_W)
    const2 = lambda i: (0, 0)
    return pl.pallas_call(
        functools.partial(_merge_kernel, seq=seq),
        grid=(n_m,),
        in_specs=[pl.BlockSpec((tm, D_MODEL), lambda i: (i, 0)),
                  pl.BlockSpec((None, 1, 6 * D_MODEL), mod_map),
                  pl.BlockSpec((tm, 3 * D_MODEL), lambda i: (i, 0)),
                  pl.BlockSpec((tm, 3 * CONV_W), lambda i: (i, 3)),
                  pl.BlockSpec((tm, SSM_W), lambda i: (i, 6144 // SSM_W)),
                  pl.BlockSpec((SUBLANES, 3 * CONV_W), lambda i: (jnp.maximum(i * hb - 1, 0), 3)),
                  pl.BlockSpec((SUBLANES, 3 * CONV_W),
                               lambda i: (jnp.minimum((i + 1) * hb, n_hb - 1), 3)),
                  pl.BlockSpec((tm, D_MODEL), lambda i: (i, 0)),
                  y_spec, y_spec,
                  pl.BlockSpec((3, CONV_W), const2),
                  pl.BlockSpec((1, CONV_W), const2),
                  pl.BlockSpec((1, SSM_W), const2),
                  pl.BlockSpec((SSM_W, SSM_W), const2),
                  pl.BlockSpec((1, SSM_W), const2),
                  pl.BlockSpec((CONV_W, D_MODEL), const2),
                  pl.BlockSpec((D_MODEL, D_MODEL), const2),
                  pl.BlockSpec((SSM_W, D_MODEL), const2),
                  pl.BlockSpec((D_MODEL, D_MODEL), const2)],
        out_specs=pl.BlockSpec((tm, D_MODEL), lambda i: (i, 0)),
        out_shape=jax.ShapeDtypeStruct((t, D_MODEL), F32),
        compiler_params=_cparams(("arbitrary",)),
        name="merge",
    )(x, mod, z, z, z, z, z, y_attn, yf2, yb2,
      lp['conv_w'], lp['conv_b'], lp['ssm_d'], lp['w_glu'], lp['b_glu'],
      lp['w_proj_conv'], lp['w_proj_attn'], lp['w_proj_ssm'], lp['w_out'])


def _ffn_kernel(x_ref, mod_ref, g_ref, wg_ref, wu_ref, wd_ref, fg_ref, o_ref, h_scr, acc_scr,
                *, final):
    k = pl.program_id(1)

    @pl.when(k == 0)
    def _():
        h = _mod_norm(x_ref[...], g_ref[...], mod_ref[:, 3 * D_MODEL:4 * D_MODEL],
                      mod_ref[:, 4 * D_MODEL:5 * D_MODEL])
        h_scr[...] = h.astype(BF16)
        acc_scr[...] = jnp.zeros_like(acc_scr)

    h = h_scr[...]
    g = jnp.dot(h, wg_ref[...], preferred_element_type=F32)
    u = jnp.dot(h, wu_ref[...], preferred_element_type=F32)
    acc_scr[...] += jnp.dot((jax.nn.silu(g) * u).astype(BF16), wd_ref[...],
                            preferred_element_type=F32)

    @pl.when(k == pl.num_programs(1) - 1)
    def _():
        xn = x_ref[...] + mod_ref[:, 5 * D_MODEL:6 * D_MODEL] * acc_scr[...]
        if final:
            ms = jnp.mean(xn * xn, axis=-1, keepdims=True)
            xn = xn * lax.rsqrt(ms + RMS_EPS) * fg_ref[...]
        o_ref[...] = xn


def _ffn_call(x, mod, g2, wg, wu, wd, fg, *, seq, tm, tf, final):
    t = x.shape[0]
    per = max(seq // tm, 1)
    mod_map = (lambda i, k: (0, 0, 0)) if tm >= seq else (lambda i, k: (i // per, 0, 0))
    return pl.pallas_call(
        functools.partial(_ffn_kernel, final=final),
        grid=(t // tm, D_FF // tf),
        in_specs=[pl.BlockSpec((tm, D_MODEL), lambda i, k: (i, 0)),
                  pl.BlockSpec((None, 1, 6 * D_MODEL), mod_map),
                  pl.BlockSpec((1, D_MODEL), lambda i, k: (0, 0)),
                  pl.BlockSpec((D_MODEL, tf), lambda i, k: (0, k)),
                  pl.BlockSpec((D_MODEL, tf), lambda i, k: (0, k)),
                  pl.BlockSpec((tf, D_MODEL), lambda i, k: (k, 0)),
                  pl.BlockSpec((1, D_MODEL), lambda i, k: (0, 0))],
        out_specs=pl.BlockSpec((tm, D_MODEL), lambda i, k: (i, 0)),
        out_shape=jax.ShapeDtypeStruct((t, D_MODEL), F32),
        scratch_shapes=[pltpu.VMEM((tm, D_MODEL), BF16), pltpu.VMEM((tm, D_MODEL), F32)],
        compiler_params=_cparams(("arbitrary", "arbitrary")),
        name="ffn",
    )(x, mod, g2, wg, wu, wd, fg)


def _state_rows(s):
    b = s.shape[0]
    return jnp.transpose(s, (1, 0, 2, 3)).reshape(2, 2 * b, HALF_STATE)


def _state_unrows(s, b):
    return jnp.transpose(s.reshape(2, b, SSM_GROUPS, SSM_STATE), (1, 0, 2, 3))


def kernel(x_prompt, x_sample, c, cache_k, cache_v, state_ssm_re, state_ssm_im, c_ctx, w_ada, b_ada, norm1_g, w_in, conv_w, conv_b, q_norm_g, k_norm_g, ssm_lambda_re, ssm_lambda_im, ssm_b_re, ssm_b_im, ssm_c_re, ssm_c_im, ssm_log_dt, ssm_d, w_glu, b_glu, w_proj_conv, w_proj_attn, w_proj_ssm, w_out, norm2_g, w_ffn_gate, w_ffn_up, w_ffn_down, final_norm_g):
    nb_c, seq_c, _ = x_prompt.shape
    nb_l, seq_l, _ = x_sample.shape
    past = cache_k.shape[2]

    cvec = jnp.zeros((8, D_MODEL), F32).at[0].set(c_ctx).at[1:1 + nb_l].set(c)
    mod_all = _ada_call(cvec, w_ada, b_ada)

    a_re, a_im, bb_re, bb_im = _disc_call(ssm_lambda_re, ssm_lambda_im, ssm_log_dt, ssm_b_re, ssm_b_im)
    bbt, cct, a8 = _ssm_weights(a_re, a_im, bb_re, bb_im, ssm_c_re, ssm_c_im)

    w_in_p = jnp.concatenate([w_in[:, :, 3584:6656], w_in[:, :, 1536:2560], w_in[:, :, 2560:3072],
                              w_in[:, :, 0:1536], w_in[:, :, 3072:3584]], axis=2).astype(BF16)
    wb = {k: v.astype(BF16) for k, v in dict(
        w_glu=w_glu, w_proj_conv=w_proj_conv, w_proj_attn=w_proj_attn, w_proj_ssm=w_proj_ssm,
        w_out=w_out, w_ffn_gate=w_ffn_gate, w_ffn_up=w_ffn_up, w_ffn_down=w_ffn_down).items()}

    bd = jnp.asarray(np.kron(np.eye(2 * LANES // HEAD_DIM, dtype=np.float32),
                             np.ones((HEAD_DIM, HEAD_DIM), np.float32)), dtype=BF16)
    tabs = _rope_tables(seq_l)
    fg = final_norm_g.reshape(1, D_MODEL)

    xc = x_prompt.reshape(nb_c * seq_c, D_MODEL)
    xl = x_sample.reshape(nb_l * seq_l, D_MODEL)
    h0_zero = jnp.zeros((2, 2 * nb_c, 2 * HALF_STATE), F32)
    ks, vs, srs, sis = [], [], [], []

    for l in range(DEPTH):
        lp = dict(conv_w=conv_w[l], conv_b=conv_b[l].reshape(1, CONV_W),
                  ssm_d=ssm_d[l].reshape(1, SSM_W), b_glu=b_glu[l].reshape(1, SSM_W),
                  w_glu=wb['w_glu'][l], w_proj_conv=wb['w_proj_conv'][l],
                  w_proj_attn=wb['w_proj_attn'][l], w_proj_ssm=wb['w_proj_ssm'][l],
                  w_out=wb['w_out'][l])
        g1 = norm1_g[l].reshape(1, D_MODEL)
        g2 = norm2_g[l].reshape(1, D_MODEL)
        qg = jnp.tile(q_norm_g[l], N_HEADS).reshape(1, D_MODEL)
        kg = jnp.tile(k_norm_g[l], N_KV_HEADS).reshape(1, 2 * LANES)
        mod_c = mod_all[l, 0:1].reshape(1, 1, 6 * D_MODEL)
        mod_l = mod_all[l, 1:1 + nb_l].reshape(nb_l, 1, 6 * D_MODEL)
        final = l == DEPTH - 1

        z, utb = _inproj_call(xc, mod_c, g1, w_in_p[l], nb=nb_c, seq=seq_c, tm=1024)
        y_attn, k_l, v_l = _attn_call(z, None, None, qg, kg, bd, None, nb=nb_c, seq=seq_c,
                                      tq=seq_c, past=0, emit_kv=True)
        yf, yb, ht = _ssm_call(utb, bbt[l], cct[l], a8[l], h0_zero, nb=nb_c, seq=seq_c)
        xc = _merge_call(xc, mod_c, z, y_attn, yf, yb, lp, nb=nb_c, seq=seq_c, tm=seq_c)
        xc = _ffn_call(xc, mod_c, g2, wb['w_ffn_gate'][l], wb['w_ffn_up'][l], wb['w_ffn_down'][l],
                       fg, seq=seq_c, tm=512, tf=1408, final=final)
        ks.append(k_l.reshape(nb_c, seq_c, N_KV_HEADS, HEAD_DIM))
        vs.append(v_l.reshape(nb_c, seq_c, N_KV_HEADS, HEAD_DIM))
        srs.append(_state_unrows(ht[:, :, :HALF_STATE], nb_c))
        sis.append(_state_unrows(ht[:, :, HALF_STATE:], nb_c))

        h0 = jnp.concatenate([_state_rows(state_ssm_re[:, l]), _state_rows(state_ssm_im[:, l])], axis=2)
        z, utb = _inproj_call(xl, mod_l, g1, w_in_p[l], nb=nb_l, seq=seq_l, tm=1024)
        (y_attn,) = _attn_call(z, cache_k[:, l].reshape(nb_l, past, 2 * LANES),
                               cache_v[:, l].reshape(nb_l, past, 2 * LANES), qg, kg, bd, tabs,
                               nb=nb_l, seq=seq_l, tq=256, past=past, emit_kv=False)
        yf, yb, _ = _ssm_call(utb, bbt[l], cct[l], a8[l], h0, nb=nb_l, seq=seq_l)
        xl = _merge_call(xl, mod_l, z, y_attn, yf, yb, lp, nb=nb_l, seq=seq_l, tm=512)
        xl = _ffn_call(xl, mod_l, g2, wb['w_ffn_gate'][l], wb['w_ffn_up'][l], wb['w_ffn_down'][l],
                       fg, seq=seq_l, tm=512, tf=1408, final=final)

    y_prompt = xc.reshape(nb_c, seq_c, D_MODEL)
    y_sample = xl.reshape(nb_l, seq_l, D_MODEL)
    return (y_prompt, y_sample, jnp.stack(ks, axis=1), jnp.stack(vs, axis=1),
            jnp.stack(srs, axis=1), jnp.stack(sis, axis=1))
```

```python
import functools
import math

import jax
import jax.numpy as jnp
import numpy as np
from jax import lax
from jax.experimental import pallas as pl
from jax.experimental.pallas import tpu as pltpu

F32 = jnp.float32
BF16 = jnp.bfloat16

D_MODEL = 1024
DEPTH = 4
GRID_W = 64
HEAD_DIM = 64
N_HEADS = 16
N_KV_HEADS = 4
ROPE_THETA = 10000.0
CONV_W = 512
SSM_W = 512
SSM_GROUP_CH = 16
SSM_GROUPS = 32
SSM_STATE = 64
D_FF = 2816
RMS_EPS = 1e-6
IN_COLS = 6656

LANES = 128
SUBLANES = 8
VMEM_LIMIT = 56 * 1024 * 1024

Z_TILE = 512
Z_U_TILE = 6144 // Z_TILE
HALF_STATE = SSM_GROUPS // 2 * SSM_STATE
SSM_ROWS = 512
ATT_ROWS = 128


def _cparams(sem, flags=None):
    return pltpu.CompilerParams(dimension_semantics=sem, vmem_limit_bytes=VMEM_LIMIT, flags=flags)


def _mod_norm(x, g, shift, scale):
    ms = jnp.mean(x * x, axis=-1, keepdims=True)
    y = x * lax.rsqrt(ms + RMS_EPS) * g
    return y * (1.0 + scale) + shift


def _ada_kernel(c_ref, w_ref, b_ref, o_ref):
    s = jax.nn.silu(c_ref[...])
    o_ref[...] = jnp.dot(s.astype(BF16), w_ref[...].astype(BF16),
                         preferred_element_type=F32) + b_ref[...]


def _ada_call(cvec, w_ada, b_ada):
    tn = 1536
    return pl.pallas_call(
        _ada_kernel,
        grid=(DEPTH, 6 * D_MODEL // tn),
        in_specs=[pl.BlockSpec((8, D_MODEL), lambda l, j: (0, 0)),
                  pl.BlockSpec((None, D_MODEL, tn), lambda l, j: (l, 0, j)),
                  pl.BlockSpec((None, 1, tn), lambda l, j: (l, 0, j))],
        out_specs=pl.BlockSpec((None, 8, tn), lambda l, j: (l, 0, j)),
        out_shape=jax.ShapeDtypeStruct((DEPTH, 8, 6 * D_MODEL), F32),
        compiler_params=_cparams(("arbitrary", "arbitrary")),
        name="ada_mod",
    )(cvec, w_ada, b_ada.reshape(DEPTH, 1, 6 * D_MODEL))


def _disc_kernel(lre_ref, lim_ref, ldt_ref, bre_ref, bim_ref,
                 are_ref, aim_ref, bbre_ref, bbim_ref):
    lre = lre_ref[...]
    lim = lim_ref[...]
    dt = jnp.exp(ldt_ref[...])
    mag = jnp.exp(lre * dt)
    a_re = mag * jnp.cos(lim * dt)
    a_im = mag * jnp.sin(lim * dt)
    den = lre * lre + lim * lim
    n_re = a_re - 1.0
    coef_re = (n_re * lre + a_im * lim) / den
    coef_im = (a_im * lre - n_re * lim) / den
    are_ref[...] = a_re
    aim_ref[...] = a_im
    bre = bre_ref[...]
    bim = bim_ref[...]
    bbre_ref[...] = coef_re * bre - coef_im * bim
    bbim_ref[...] = coef_re * bim + coef_im * bre


def _disc_call(lam_re, lam_im, log_dt, b_re, b_im):
    n = DEPTH * 2 * SSM_GROUPS
    lre = lam_re.reshape(n, 1, SSM_STATE)
    lim = lam_im.reshape(n, 1, SSM_STATE)
    ldt = log_dt.reshape(n, 1, 1)
    bre = jnp.swapaxes(b_re.reshape(n, SSM_STATE, SSM_GROUP_CH), 1, 2)
    bim = jnp.swapaxes(b_im.reshape(n, SSM_STATE, SSM_GROUP_CH), 1, 2)
    small = jax.ShapeDtypeStruct((n, 1, SSM_STATE), F32)
    big = jax.ShapeDtypeStruct((n, SSM_GROUP_CH, SSM_STATE), F32)
    return pl.pallas_call(
        _disc_kernel, out_shape=(small, small, big, big), name="ssm_disc",
    )(lre, lim, ldt, bre, bim)


def _ssm_weights(a_re, a_im, bb_re, bb_im, c_re, c_im):
    def bb_tiles(bb):
        return bb.reshape(DEPTH, 2, 2, 4, 4, SSM_GROUP_CH, SSM_STATE)
    bb = jnp.stack([bb_tiles(bb_re), bb_tiles(bb_im)], axis=2)
    sel = np.zeros((4, 8, 4), np.float32)
    for m in range(4):
        for j in range(4):
            sel[m, 4 * (m % 2) + j, j] = 1.0
    bbt = jnp.einsum('ldrhmjcp,mkj->ldrmhkcjp', bb, jnp.asarray(sel))
    bbt = bbt.reshape(DEPTH, 2, 8, 2 * LANES, 2 * LANES).astype(BF16)
    def c_tiles(c):
        return c.reshape(DEPTH, 2, 2, 2, 8, SSM_GROUP_CH, SSM_STATE)
    cc = jnp.stack([c_tiles(c_re), -c_tiles(c_im)], axis=2)
    eye = jnp.eye(8, dtype=F32)
    cct = jnp.einsum('ldrhmkop,kj->ldmrkphjo', cc, eye)
    cct = cct.reshape(DEPTH, 2, 2, 2 * 8 * SSM_STATE, 2 * LANES).astype(BF16)
    def a_rows(a):
        a = a.reshape(DEPTH, 2, 2, HALF_STATE)
        return jnp.tile(a, (1, 1, 4, 1))
    a8 = jnp.stack([a_rows(a_re), a_rows(a_im)], axis=2)
    return bbt, cct, a8


def _inproj_kernel(x_ref, mod_ref, g_ref, w_ref, z_ref):
    h = _mod_norm(x_ref[...], g_ref[...], mod_ref[:, 0:D_MODEL], mod_ref[:, D_MODEL:2 * D_MODEL])
    h = h.astype(BF16)
    for c in range(IN_COLS // Z_TILE):
        cols = slice(c * Z_TILE, (c + 1) * Z_TILE)
        z_ref[:, cols] = jnp.dot(h, w_ref[:, cols], preferred_element_type=F32).astype(BF16)


def _inproj_call(x, mod, g1, w_in, *, seq, tm):
    t = x.shape[0]
    per = max(seq // tm, 1)
    mod_map = (lambda i: (0, 0, 0)) if tm >= seq else (lambda i: (i // per, 0, 0))
    return pl.pallas_call(
        _inproj_kernel,
        grid=(t // tm,),
        in_specs=[pl.BlockSpec((tm, D_MODEL), lambda i: (i, 0)),
                  pl.BlockSpec((None, 1, 6 * D_MODEL), mod_map),
                  pl.BlockSpec((1, D_MODEL), lambda i: (0, 0)),
                  pl.BlockSpec((D_MODEL, IN_COLS), lambda i: (0, 0), pipeline_mode=pl.Buffered(1))],
        out_specs=pl.BlockSpec((tm, IN_COLS), lambda i: (i, 0)),
        out_shape=jax.ShapeDtypeStruct((t, IN_COLS), BF16),
        compiler_params=_cparams(("arbitrary",)),
        name="in_proj",
    )(x, mod, g1, w_in)


def _head_ssq(x, bd):
    parts = []
    for t in range(x.shape[1] // (2 * LANES)):
        xs = x[:, 2 * LANES * t:2 * LANES * (t + 1)]
        parts.append(jnp.dot((xs * xs).astype(BF16), bd, preferred_element_type=F32))
    return parts[0] if len(parts) == 1 else jnp.concatenate(parts, axis=1)


def _rope(x, cos, sin_up, sin_dn):
    w = x.shape[1]
    rep = w // LANES
    cos = jnp.tile(cos, (1, rep))
    sin_up = jnp.tile(sin_up, (1, rep))
    sin_dn = jnp.tile(sin_dn, (1, rep))
    return x * cos + pltpu.roll(x, w - 16, 1) * sin_up + pltpu.roll(x, 16, 1) * sin_dn


def _dup_heads(x, lane_lo):
    xr = pltpu.roll(x, HEAD_DIM, 1)
    return jnp.where(lane_lo, x, xr), jnp.where(lane_lo, xr, x)


def _attn_kernel(*refs, seq, past, rope, emit_kv):
    it = iter(refs)
    q_ref, kv_ref = next(it), next(it)
    kc_ref = vc_ref = cos_ref = sup_ref = sdn_ref = cosk_ref = supk_ref = sdnk_ref = None
    if past:
        kc_ref, vc_ref = next(it), next(it)
    qg_ref, kg_ref, bd_ref = next(it), next(it), next(it)
    if rope:
        cos_ref, sup_ref, sdn_ref = next(it), next(it), next(it)
        cosk_ref, supk_ref, sdnk_ref = next(it), next(it), next(it)
    o_ref = next(it)
    kout_ref = vout_ref = None
    if emit_kv:
        kout_ref, vout_ref = next(it), next(it)
    k2_scr, v2_scr = next(it), next(it)

    qi = pl.program_id(1)
    bd = bd_ref[...]

    @pl.when(qi == 0)
    def _prep():
        kv = kv_ref[...].astype(F32)
        k = kv[:, :2 * LANES]
        v = kv[:, 2 * LANES:]
        kn = k * lax.rsqrt(_head_ssq(k, bd) * (1.0 / HEAD_DIM) + RMS_EPS) * kg_ref[...]
        if emit_kv:
            kout_ref[...] = kn
            vout_ref[...] = v
        if rope:
            kn = _rope(kn, cosk_ref[...], supk_ref[...], sdnk_ref[...])
        lane_lo = lax.broadcasted_iota(jnp.int32, (seq, LANES), 1) < HEAD_DIM
        for t in range(2):
            ka, kb = _dup_heads(kn[:, LANES * t:LANES * (t + 1)], lane_lo)
            va, vb = _dup_heads(v[:, LANES * t:LANES * (t + 1)], lane_lo)
            k2_scr[2 * t, 0:seq, :] = ka.astype(BF16)
            k2_scr[2 * t + 1, 0:seq, :] = kb.astype(BF16)
            v2_scr[2 * t, 0:seq, :] = va.astype(BF16)
            v2_scr[2 * t + 1, 0:seq, :] = vb.astype(BF16)
        if past:
            kc = kc_ref[...]
            vc = vc_ref[...]
            lane_lo_p = lax.broadcasted_iota(jnp.int32, (past, LANES), 1) < HEAD_DIM
            for t in range(2):
                ka, kb = _dup_heads(kc[:, LANES * t:LANES * (t + 1)], lane_lo_p)
                va, vb = _dup_heads(vc[:, LANES * t:LANES * (t + 1)], lane_lo_p)
                k2_scr[2 * t, seq:seq + past, :] = ka.astype(BF16)
                k2_scr[2 * t + 1, seq:seq + past, :] = kb.astype(BF16)
                v2_scr[2 * t, seq:seq + past, :] = va.astype(BF16)
                v2_scr[2 * t + 1, seq:seq + past, :] = vb.astype(BF16)

    q = q_ref[...].astype(F32)
    tq = q.shape[0]
    qn = q * lax.rsqrt(_head_ssq(q, bd) * (1.0 / HEAD_DIM) + RMS_EPS) * qg_ref[...]
    if rope:
        qn = _rope(qn, cos_ref[...], sup_ref[...], sdn_ref[...])
    qn = qn * (HEAD_DIM ** -0.5 * math.log2(math.e))
    lane_lo = lax.broadcasted_iota(jnp.int32, (tq, LANES), 1) < HEAD_DIM
    for t in range(N_HEADS // 2):
        qt = qn[:, LANES * t:LANES * (t + 1)]
        kvh = t // 2
        outs = []
        for half in range(2):
            qm = jnp.where(lane_lo if half == 0 else jnp.logical_not(lane_lo), qt, 0.0).astype(BF16)
            s = lax.dot_general(qm, k2_scr[kvh], (((1,), (1,)), ((), ())),
                                preferred_element_type=F32)
            m = jnp.max(s, axis=-1, keepdims=True)
            p = jnp.exp2(s - m)
            l = jnp.sum(p, axis=-1, keepdims=True)
            o2 = jnp.dot(p.astype(BF16), v2_scr[kvh], preferred_element_type=F32)
            outs.append(o2 * (1.0 / l))
        o_ref[:, LANES * t:LANES * (t + 1)] = jnp.where(lane_lo, outs[0], outs[1]).astype(BF16)


def _attn_call(z, kc, vc, qg, kg, bd, tabs, *, nb, seq, tq, past, emit_kv):
    rope = tabs is not None
    nq = seq // tq
    keys = seq + past
    q_col = 3072 // D_MODEL
    kv_col = 4096 // 512
    in_specs = [pl.BlockSpec((tq, D_MODEL), lambda b, i: (b * nq + i, q_col)),
                pl.BlockSpec((seq, 512), lambda b, i: (b, kv_col))]
    args = [z, z]
    if past:
        in_specs += [pl.BlockSpec((None, past, 2 * LANES), lambda b, i: (b, 0, 0))] * 2
        args += [kc, vc]
    in_specs += [pl.BlockSpec((1, D_MODEL), lambda b, i: (0, 0)),
                 pl.BlockSpec((1, 2 * LANES), lambda b, i: (0, 0)),
                 pl.BlockSpec((2 * LANES, 2 * LANES), lambda b, i: (0, 0))]
    args += [qg, kg, bd]
    if rope:
        in_specs += [pl.BlockSpec((tq, LANES), lambda b, i: (i, 0))] * 3
        in_specs += [pl.BlockSpec((seq, LANES), lambda b, i: (0, 0))] * 3
        args += list(tabs) + list(tabs)
    out_specs = [pl.BlockSpec((tq, D_MODEL), lambda b, i: (b * nq + i, 0))]
    out_shape = [jax.ShapeDtypeStruct((nb * seq, D_MODEL), BF16)]
    if emit_kv:
        out_specs += [pl.BlockSpec((None, seq, 2 * LANES), lambda b, i: (b, 0, 0))] * 2
        out_shape += [jax.ShapeDtypeStruct((nb, seq, 2 * LANES), F32)] * 2
    return pl.pallas_call(
        functools.partial(_attn_kernel, seq=seq, past=past, rope=rope, emit_kv=emit_kv),
        grid=(nb, nq),
        in_specs=in_specs, out_specs=out_specs, out_shape=out_shape,
        scratch_shapes=[pltpu.VMEM((N_KV_HEADS, keys, LANES), BF16),
                        pltpu.VMEM((N_KV_HEADS, keys, LANES), BF16)],
        compiler_params=_cparams(("arbitrary", "arbitrary")),
        name="attention",
    )(*args)


def _rope_tables(length):
    f = HEAD_DIM // 4
    inv = ROPE_THETA ** (-np.arange(f, dtype=np.float32) / f)
    pos = np.arange(length)
    row = (pos // GRID_W).astype(np.float32)
    col = (pos % GRID_W).astype(np.float32)
    ang_r = row[:, None] * inv[None, :]
    ang_c = col[:, None] * inv[None, :]
    z = np.zeros_like(ang_r)
    cos64 = np.concatenate([np.cos(ang_r), np.cos(ang_r), np.cos(ang_c), np.cos(ang_c)], axis=1)
    up64 = np.concatenate([-np.sin(ang_r), z, -np.sin(ang_c), z], axis=1)
    dn64 = np.concatenate([z, np.sin(ang_r), z, np.sin(ang_c)], axis=1)
    tile2 = lambda a: jnp.asarray(np.tile(a, (1, 2)), dtype=F32)
    return tile2(cos64), tile2(up64), tile2(dn64)


def _ssm_kernel(uf_ref, ub_ref, pin_ref, pout_ref, bbt_ref, cct_ref, a_ref, h0_ref,
                yf_ref, yb_ref, ht_ref, x_scr, hc_scr, *, rows_per_step, steps):
    j = pl.program_id(1)
    r = rows_per_step
    rows = r * steps
    nbg = r // 2
    tok = nbg * steps

    @pl.when(j == 0)
    def _():
        hc_scr[...] = h0_ref[...]

    first_half = (lax.broadcasted_iota(jnp.int32, (rows, LANES), 0) & 1) == 0
    width = SUBLANES * HALF_STATE // r
    for d, (u_ref, y_ref) in enumerate(((uf_ref, yf_ref), (ub_ref, yb_ref))):
        u_bt = u_ref[...].reshape(tok, SSM_W)
        u = jnp.dot(pin_ref[...], u_bt, preferred_element_type=F32)
        kcat = []
        for q in range(2):
            u0 = u[:, LANES * q:LANES * (q + 1)]
            u1 = u[:, 2 * LANES + LANES * q:2 * LANES + LANES * (q + 1)]
            kcat.append(jnp.concatenate([jnp.where(first_half, u0, 0.0),
                                         jnp.where(first_half, 0.0, u1)], axis=1).astype(BF16))
        for n in range(8):
            x_scr[:, 2 * LANES * n:2 * LANES * (n + 1)] = jnp.dot(
                kcat[(n % 4) // 2], bbt_ref[d, n], preferred_element_type=F32)

        for p in range(HALF_STATE // width):
            l0 = p * width
            reps = r // SUBLANES
            a_re = jnp.tile(a_ref[d, 0, :, l0:l0 + width], (reps, 1))
            a_im = jnp.tile(a_ref[d, 1, :, l0:l0 + width], (reps, 1))

            def step(kk, carry, d=d, l0=l0, a_re=a_re, a_im=a_im):
                h_re, h_im = carry
                k = kk if d == 0 else steps - 1 - kk
                r0 = pl.multiple_of(k * r, SUBLANES)
                x_re = x_scr[pl.ds(r0, r), l0:l0 + width]
                x_im = x_scr[pl.ds(r0, r), HALF_STATE + l0:HALF_STATE + l0 + width]
                n_re = a_re * h_re - a_im * h_im + x_re
                n_im = a_re * h_im + a_im * h_re + x_im
                x_scr[pl.ds(r0, r), l0:l0 + width] = n_re
                x_scr[pl.ds(r0, r), HALF_STATE + l0:HALF_STATE + l0 + width] = n_im
                return n_re, n_im

            h_re, h_im = lax.fori_loop(
                0, steps, step,
                (hc_scr[d, :, l0:l0 + width], hc_scr[d, :, HALF_STATE + l0:HALF_STATE + l0 + width]))
            hc_scr[d, :, l0:l0 + width] = h_re
            hc_scr[d, :, HALF_STATE + l0:HALF_STATE + l0 + width] = h_im

        yy = []
        for m in range(2):
            hk = jnp.concatenate(
                [x_scr[:, 512 * m:512 * (m + 1)],
                 x_scr[:, HALF_STATE + 512 * m:HALF_STATE + 512 * (m + 1)]], axis=1).astype(BF16)
            yy.append(jnp.dot(hk, cct_ref[d, m], preferred_element_type=F32))
        for hh in range(2):
            ysel = jnp.concatenate([yy[0][:, LANES * hh:LANES * (hh + 1)],
                                    yy[1][:, LANES * hh:LANES * (hh + 1)]], axis=1).astype(BF16)
            y_bt = jnp.dot(pout_ref[hh], ysel, preferred_element_type=F32)
            y_ref[:, :, 2 * LANES * hh:2 * LANES * (hh + 1)] = (
                y_bt.reshape(nbg, steps, 2 * LANES).astype(BF16))

    @pl.when(j == pl.num_programs(1) - 1)
    def _():
        ht_ref[...] = hc_scr[...]


def _ssm_perms(nbg, steps):
    tok = nbg * steps
    pin = np.zeros((2 * tok, tok), np.float32)
    for t in range(steps):
        for b in range(nbg):
            for h in range(2):
                pin[(t * nbg + b) * 2 + h, b * steps + t] = 1.0
    pout = np.zeros((2, tok, 2 * tok), np.float32)
    for h in range(2):
        pout[h] = (pin * (np.arange(2 * tok)[:, None] % 2 == h)).T
    return jnp.asarray(pin, dtype=BF16), jnp.asarray(pout, dtype=BF16)


def _ssm_call(z, bbt, cct, a8, h0, *, nb, seq, nbg):
    r = 2 * nbg
    steps = SSM_ROWS // r
    n = seq // steps
    ngrp = nb // nbg
    z3 = z.reshape(nb, seq, IN_COLS)
    pin, pout = _ssm_perms(nbg, steps)
    ys = jax.ShapeDtypeStruct((nb, seq, SSM_W), BF16)
    ublk = (nbg, steps, SSM_W)
    ucol = 6144 // SSM_W
    hblk = (2, r, 2 * HALF_STATE)
    c4 = lambda g, j: (0, 0, 0, 0)
    return pl.pallas_call(
        functools.partial(_ssm_kernel, rows_per_step=r, steps=steps),
        grid=(ngrp, n),
        in_specs=[pl.BlockSpec(ublk, lambda g, j: (g, j, ucol)),
                  pl.BlockSpec(ublk, lambda g, j: (g, n - 1 - j, ucol)),
                  pl.BlockSpec(pin.shape, lambda g, j: (0, 0)),
                  pl.BlockSpec(pout.shape, lambda g, j: (0, 0, 0)),
                  pl.BlockSpec(bbt.shape, c4),
                  pl.BlockSpec(cct.shape, c4),
                  pl.BlockSpec(a8.shape, c4),
                  pl.BlockSpec(hblk, lambda g, j: (0, g, 0))],
        out_specs=[pl.BlockSpec(ublk, lambda g, j: (g, j, 0)),
                   pl.BlockSpec(ublk, lambda g, j: (g, n - 1 - j, 0)),
                   pl.BlockSpec(hblk, lambda g, j: (0, g, 0))],
        out_shape=[ys, ys, jax.ShapeDtypeStruct(h0.shape, F32)],
        scratch_shapes=[pltpu.VMEM((SSM_ROWS, 2 * HALF_STATE), F32),
                        pltpu.VMEM(hblk, F32)],
        compiler_params=_cparams(("arbitrary", "arbitrary")),
        name="ssm_scan",
    )(z3, z3, pin, pout, bbt, cct, a8, h0)


def _merge_kernel(x_ref, mod_ref, zg_ref, zc_ref, zu_ref, cprev_ref, cnext_ref,
                  ya_ref, yf_ref, yb_ref, cw_ref, cb_ref, sd_ref, wglu_ref, bglu_ref,
                  wpc_ref, wpa_ref, wps_ref, wo_ref, o_ref, *, seq):
    i = pl.program_id(0)
    tm = x_ref.shape[0]
    zc = zc_ref[...].astype(F32)
    cb = zc[:, 0:CONV_W]
    prod = zc[:, CONV_W:2 * CONV_W] * zc[:, 2 * CONV_W:3 * CONV_W]
    cp = cprev_ref[...].astype(F32)
    cn = cnext_ref[...].astype(F32)
    halo_prev = cp[7:8, CONV_W:2 * CONV_W] * cp[7:8, 2 * CONV_W:3 * CONV_W]
    halo_next = cn[0:1, CONV_W:2 * CONV_W] * cn[0:1, 2 * CONV_W:3 * CONV_W]
    row = lax.broadcasted_iota(jnp.int32, (tm, CONV_W), 0)
    tpos = (i * tm + row) % seq
    prev = jnp.where(row == 0, halo_prev, pltpu.roll(prod, 1, 0))
    prev = jnp.where(tpos == 0, 0.0, prev)
    nxt = jnp.where(row == tm - 1, halo_next, pltpu.roll(prod, tm - 1, 0))
    nxt = jnp.where(tpos == seq - 1, 0.0, nxt)
    cw = cw_ref[...]
    y_conv = cb * (cw[0:1] * prev + cw[1:2] * prod + cw[2:3] * nxt + cb_ref[...])

    u = zu_ref[...].astype(F32)
    y = jax.nn.gelu(yf_ref[...].astype(F32) + yb_ref[...].astype(F32) + sd_ref[...] * u)
    y_ssm = y * jax.nn.sigmoid(
        jnp.dot(y.astype(BF16), wglu_ref[...], preferred_element_type=F32) + bglu_ref[...])

    zg = zg_ref[...].astype(F32)
    merged = (jax.nn.sigmoid(zg[:, 0:D_MODEL])
              * jnp.dot(y_conv.astype(BF16), wpc_ref[...], preferred_element_type=F32)
              + jax.nn.sigmoid(zg[:, D_MODEL:2 * D_MODEL])
              * jnp.dot(ya_ref[...], wpa_ref[...], preferred_element_type=F32)
              + jax.nn.sigmoid(zg[:, 2 * D_MODEL:3 * D_MODEL])
              * jnp.dot(y_ssm.astype(BF16), wps_ref[...], preferred_element_type=F32))
    gate1 = mod_ref[:, 2 * D_MODEL:3 * D_MODEL]
    o_ref[...] = x_ref[...] + gate1 * jnp.dot(merged.astype(BF16), wo_ref[...],
                                              preferred_element_type=F32)


def _merge_call(x, mod, z, y_attn, yf, yb, lp, *, seq, tm):
    t = x.shape[0]
    n_m = t // tm
    per = max(seq // tm, 1)
    hb = tm // SUBLANES
    n_hb = t // SUBLANES
    mod_map = (lambda i: (0, 0, 0)) if tm >= seq else (lambda i: (i // per, 0, 0))
    y_spec = pl.BlockSpec((tm, SSM_W), lambda i: (i, 0))
    yf2 = yf.reshape(t, SSM_W)
    yb2 = yb.reshape(t, SSM_W)
    const2 = lambda i: (0, 0)
    resident = dict(pipeline_mode=pl.Buffered(1))
    return pl.pallas_call(
        functools.partial(_merge_kernel, seq=seq),
        grid=(n_m,),
        in_specs=[pl.BlockSpec((tm, D_MODEL), lambda i: (i, 0)),
                  pl.BlockSpec((None, 1, 6 * D_MODEL), mod_map),
                  pl.BlockSpec((tm, 3 * D_MODEL), lambda i: (i, 0)),
                  pl.BlockSpec((tm, 3 * CONV_W), lambda i: (i, 3)),
                  pl.BlockSpec((tm, SSM_W), lambda i: (i, 6144 // SSM_W)),
                  pl.BlockSpec((SUBLANES, 3 * CONV_W), lambda i: (jnp.maximum(i * hb - 1, 0), 3)),
                  pl.BlockSpec((SUBLANES, 3 * CONV_W),
                               lambda i: (jnp.minimum((i + 1) * hb, n_hb - 1), 3)),
                  pl.BlockSpec((tm, D_MODEL), lambda i: (i, 0)),
                  y_spec, y_spec,
                  pl.BlockSpec((3, CONV_W), const2),
                  pl.BlockSpec((1, CONV_W), const2),
                  pl.BlockSpec((1, SSM_W), const2),
                  pl.BlockSpec((SSM_W, SSM_W), const2, **resident),
                  pl.BlockSpec((1, SSM_W), const2),
                  pl.BlockSpec((CONV_W, D_MODEL), const2, **resident),
                  pl.BlockSpec((D_MODEL, D_MODEL), const2, **resident),
                  pl.BlockSpec((SSM_W, D_MODEL), const2, **resident),
                  pl.BlockSpec((D_MODEL, D_MODEL), const2, **resident)],
        out_specs=pl.BlockSpec((tm, D_MODEL), lambda i: (i, 0)),
        out_shape=jax.ShapeDtypeStruct((t, D_MODEL), F32),
        compiler_params=_cparams(("arbitrary",)),
        name="merge",
    )(x, mod, z, z, z, z, z, y_attn, yf2, yb2,
      lp['conv_w'], lp['conv_b'], lp['ssm_d'], lp['w_glu'], lp['b_glu'],
      lp['w_proj_conv'], lp['w_proj_attn'], lp['w_proj_ssm'], lp['w_out'])


def _ffn_kernel(x_ref, mod_ref, g_ref, wg_ref, wu_ref, wd_ref, fg_ref, o_ref, *, final, tf):
    x = x_ref[...]
    h = _mod_norm(x, g_ref[...], mod_ref[:, 3 * D_MODEL:4 * D_MODEL],
                  mod_ref[:, 4 * D_MODEL:5 * D_MODEL]).astype(BF16)
    acc = None
    for c in range(D_FF // tf):
        cols = slice(c * tf, (c + 1) * tf)
        g = jnp.dot(h, wg_ref[:, cols], preferred_element_type=F32)
        u = jnp.dot(h, wu_ref[:, cols], preferred_element_type=F32)
        part = jnp.dot((jax.nn.silu(g) * u).astype(BF16), wd_ref[cols, :], preferred_element_type=F32)
        acc = part if acc is None else acc + part
    xn = x + mod_ref[:, 5 * D_MODEL:6 * D_MODEL] * acc
    if final:
        ms = jnp.mean(xn * xn, axis=-1, keepdims=True)
        xn = xn * lax.rsqrt(ms + RMS_EPS) * fg_ref[...]
    o_ref[...] = xn


def _ffn_call(x, mod, g2, wg, wu, wd, fg, *, seq, tm, tf, final):
    t = x.shape[0]
    per = max(seq // tm, 1)
    mod_map = (lambda i: (0, 0, 0)) if tm >= seq else (lambda i: (i // per, 0, 0))
    resident = dict(pipeline_mode=pl.Buffered(1))
    return pl.pallas_call(
        functools.partial(_ffn_kernel, final=final, tf=tf),
        grid=(t // tm,),
        in_specs=[pl.BlockSpec((tm, D_MODEL), lambda i: (i, 0)),
                  pl.BlockSpec((None, 1, 6 * D_MODEL), mod_map),
                  pl.BlockSpec((1, D_MODEL), lambda i: (0, 0)),
                  pl.BlockSpec((D_MODEL, D_FF), lambda i: (0, 0), **resident),
                  pl.BlockSpec((D_MODEL, D_FF), lambda i: (0, 0), **resident),
                  pl.BlockSpec((D_FF, D_MODEL), lambda i: (0, 0), **resident),
                  pl.BlockSpec((1, D_MODEL), lambda i: (0, 0))],
        out_specs=pl.BlockSpec((tm, D_MODEL), lambda i: (i, 0)),
        out_shape=jax.ShapeDtypeStruct((t, D_MODEL), F32),
        compiler_params=_cparams(("arbitrary",)),
        name="ffn",
    )(x, mod, g2, wg, wu, wd, fg)


def _state_rows(s):
    b = s.shape[0]
    return jnp.transpose(s, (1, 0, 2, 3)).reshape(2, 2 * b, HALF_STATE)


def _state_unrows(s, b):
    return jnp.transpose(s.reshape(2, b, SSM_GROUPS, SSM_STATE), (1, 0, 2, 3))


def kernel(x_prompt, x_sample, c, cache_k, cache_v, state_ssm_re, state_ssm_im, c_ctx, w_ada, b_ada, norm1_g, w_in, conv_w, conv_b, q_norm_g, k_norm_g, ssm_lambda_re, ssm_lambda_im, ssm_b_re, ssm_b_im, ssm_c_re, ssm_c_im, ssm_log_dt, ssm_d, w_glu, b_glu, w_proj_conv, w_proj_attn, w_proj_ssm, w_out, norm2_g, w_ffn_gate, w_ffn_up, w_ffn_down, final_norm_g):
    nb_c, seq_c, _ = x_prompt.shape
    nb_l, seq_l, _ = x_sample.shape
    past = cache_k.shape[2]

    cvec = jnp.zeros((8, D_MODEL), F32).at[0].set(c_ctx).at[1:1 + nb_l].set(c)
    mod_all = _ada_call(cvec, w_ada, b_ada)

    a_re, a_im, bb_re, bb_im = _disc_call(ssm_lambda_re, ssm_lambda_im, ssm_log_dt, ssm_b_re, ssm_b_im)
    bbt, cct, a8 = _ssm_weights(a_re, a_im, bb_re, bb_im, ssm_c_re, ssm_c_im)

    w_in_p = jnp.concatenate([w_in[:, :, 3584:6656], w_in[:, :, 1536:2560], w_in[:, :, 2560:3072],
                              w_in[:, :, 0:1536], w_in[:, :, 3072:3584]], axis=2).astype(BF16)
    wb = {k: v.astype(BF16) for k, v in dict(
        w_glu=w_glu, w_proj_conv=w_proj_conv, w_proj_attn=w_proj_attn, w_proj_ssm=w_proj_ssm,
        w_out=w_out, w_ffn_gate=w_ffn_gate, w_ffn_up=w_ffn_up, w_ffn_down=w_ffn_down).items()}

    bd = jnp.asarray(np.kron(np.eye(2 * LANES // HEAD_DIM, dtype=np.float32),
                             np.ones((HEAD_DIM, HEAD_DIM), np.float32)), dtype=BF16)
    tabs = _rope_tables(seq_l)
    fg = final_norm_g.reshape(1, D_MODEL)

    xc = x_prompt.reshape(nb_c * seq_c, D_MODEL)
    xl = x_sample.reshape(nb_l * seq_l, D_MODEL)
    h0_zero = jnp.zeros((2, 2 * nb_c, 2 * HALF_STATE), F32)
    ks, vs, srs, sis = [], [], [], []

    for l in range(DEPTH):
        lp = dict(conv_w=conv_w[l], conv_b=conv_b[l].reshape(1, CONV_W),
                  ssm_d=ssm_d[l].reshape(1, SSM_W), b_glu=b_glu[l].reshape(1, SSM_W),
                  w_glu=wb['w_glu'][l], w_proj_conv=wb['w_proj_conv'][l],
                  w_proj_attn=wb['w_proj_attn'][l], w_proj_ssm=wb['w_proj_ssm'][l],
                  w_out=wb['w_out'][l])
        g1 = norm1_g[l].reshape(1, D_MODEL)
        g2 = norm2_g[l].reshape(1, D_MODEL)
        qg = jnp.tile(q_norm_g[l], N_HEADS).reshape(1, D_MODEL)
        kg = jnp.tile(k_norm_g[l], N_KV_HEADS).reshape(1, 2 * LANES)
        mod_c = mod_all[l, 0:1].reshape(1, 1, 6 * D_MODEL)
        mod_l = mod_all[l, 1:1 + nb_l].reshape(nb_l, 1, 6 * D_MODEL)
        final = l == DEPTH - 1

        z = _inproj_call(xc, mod_c, g1, w_in_p[l], seq=seq_c, tm=512)
        y_attn, k_l, v_l = _attn_call(z, None, None, qg, kg, bd, None, nb=nb_c, seq=seq_c,
                                      tq=seq_c, past=0, emit_kv=True)
        yf, yb, ht = _ssm_call(z, bbt[l], cct[l], a8[l], h0_zero, nb=nb_c, seq=seq_c, nbg=16)
        xc = _merge_call(xc, mod_c, z, y_attn, yf, yb, lp, seq=seq_c, tm=512)
        xc = _ffn_call(xc, mod_c, g2, wb['w_ffn_gate'][l], wb['w_ffn_up'][l], wb['w_ffn_down'][l],
                       fg, seq=seq_c, tm=512, tf=1408, final=final)
        ks.append(k_l.reshape(nb_c, seq_c, N_KV_HEADS, HEAD_DIM))
        vs.append(v_l.reshape(nb_c, seq_c, N_KV_HEADS, HEAD_DIM))
        srs.append(_state_unrows(ht[:, :, :HALF_STATE], nb_c))
        sis.append(_state_unrows(ht[:, :, HALF_STATE:], nb_c))

        h0 = jnp.concatenate([_state_rows(state_ssm_re[:, l]), _state_rows(state_ssm_im[:, l])], axis=2)
        z = _inproj_call(xl, mod_l, g1, w_in_p[l], seq=seq_l, tm=512)
        (y_attn,) = _attn_call(z, cache_k[:, l].reshape(nb_l, past, 2 * LANES),
                               cache_v[:, l].reshape(nb_l, past, 2 * LANES), qg, kg, bd, tabs,
                               nb=nb_l, seq=seq_l, tq=256, past=past, emit_kv=False)
        yf, yb, _ = _ssm_call(z, bbt[l], cct[l], a8[l], h0, nb=nb_l, seq=seq_l, nbg=nb_l)
        xl = _merge_call(xl, mod_l, z, y_attn, yf, yb, lp, seq=seq_l, tm=512)
        xl = _ffn_call(xl, mod_l, g2, wb['w_ffn_gate'][l], wb['w_ffn_up'][l], wb['w_ffn_down'][l],
                       fg, seq=seq_l, tm=512, tf=1408, final=final)

    y_prompt = xc.reshape(nb_c, seq_c, D_MODEL)
    y_sample = xl.reshape(nb_l, seq_l, D_MODEL)
    return (y_prompt, y_sample, jnp.stack(ks, axis=1), jnp.stack(vs, axis=1),
            jnp.stack(srs, axis=1), jnp.stack(sis, axis=1))
```

```python
import functools
import math

import jax
import jax.numpy as jnp
import numpy as np
from jax import lax
from jax.experimental import pallas as pl
from jax.experimental.pallas import tpu as pltpu

F32 = jnp.float32
BF16 = jnp.bfloat16

D_MODEL = 1024
DEPTH = 4
GRID_W = 64
HEAD_DIM = 64
N_HEADS = 16
N_KV_HEADS = 4
ROPE_THETA = 10000.0
CONV_W = 512
SSM_W = 512
SSM_GROUP_CH = 16
SSM_GROUPS = 32
SSM_STATE = 64
D_FF = 2816
RMS_EPS = 1e-6
IN_COLS = 6656

LANES = 128
SUBLANES = 8
VMEM_LIMIT = 56 * 1024 * 1024

Z_TILE = 512
Z_U_TILE = 6144 // Z_TILE
HALF_STATE = SSM_GROUPS // 2 * SSM_STATE
SSM_ROWS = 512
ATT_TOK = 64
ATT_KEYS = 32
ATT_WAVE = 2


def _cparams(sem, flags=None):
    return pltpu.CompilerParams(dimension_semantics=sem, vmem_limit_bytes=VMEM_LIMIT, flags=flags)


def _mod_norm(x, g, shift, scale):
    ms = jnp.mean(x * x, axis=-1, keepdims=True)
    y = x * lax.rsqrt(ms + RMS_EPS) * g
    return y * (1.0 + scale) + shift


def _ada_kernel(c_ref, w_ref, b_ref, o_ref):
    s = jax.nn.silu(c_ref[...])
    o_ref[...] = jnp.dot(s.astype(BF16), w_ref[...].astype(BF16),
                         preferred_element_type=F32) + b_ref[...]


def _ada_call(cvec, w_ada, b_ada):
    tn = 1536
    return pl.pallas_call(
        _ada_kernel,
        grid=(DEPTH, 6 * D_MODEL // tn),
        in_specs=[pl.BlockSpec((8, D_MODEL), lambda l, j: (0, 0)),
                  pl.BlockSpec((None, D_MODEL, tn), lambda l, j: (l, 0, j)),
                  pl.BlockSpec((None, 1, tn), lambda l, j: (l, 0, j))],
        out_specs=pl.BlockSpec((None, 8, tn), lambda l, j: (l, 0, j)),
        out_shape=jax.ShapeDtypeStruct((DEPTH, 8, 6 * D_MODEL), F32),
        compiler_params=_cparams(("arbitrary", "arbitrary")),
        name="ada_mod",
    )(cvec, w_ada, b_ada.reshape(DEPTH, 1, 6 * D_MODEL))


def _disc_kernel(lre_ref, lim_ref, ldt_ref, bre_ref, bim_ref,
                 are_ref, aim_ref, bbre_ref, bbim_ref):
    lre = lre_ref[...]
    lim = lim_ref[...]
    dt = jnp.exp(ldt_ref[...])
    mag = jnp.exp(lre * dt)
    a_re = mag * jnp.cos(lim * dt)
    a_im = mag * jnp.sin(lim * dt)
    den = lre * lre + lim * lim
    n_re = a_re - 1.0
    coef_re = (n_re * lre + a_im * lim) / den
    coef_im = (a_im * lre - n_re * lim) / den
    are_ref[...] = a_re
    aim_ref[...] = a_im
    bre = bre_ref[...]
    bim = bim_ref[...]
    bbre_ref[...] = coef_re * bre - coef_im * bim
    bbim_ref[...] = coef_re * bim + coef_im * bre


def _disc_call(lam_re, lam_im, log_dt, b_re, b_im):
    n = DEPTH * 2 * SSM_GROUPS
    lre = lam_re.reshape(n, 1, SSM_STATE)
    lim = lam_im.reshape(n, 1, SSM_STATE)
    ldt = log_dt.reshape(n, 1, 1)
    bre = jnp.swapaxes(b_re.reshape(n, SSM_STATE, SSM_GROUP_CH), 1, 2)
    bim = jnp.swapaxes(b_im.reshape(n, SSM_STATE, SSM_GROUP_CH), 1, 2)
    small = jax.ShapeDtypeStruct((n, 1, SSM_STATE), F32)
    big = jax.ShapeDtypeStruct((n, SSM_GROUP_CH, SSM_STATE), F32)
    return pl.pallas_call(
        _disc_kernel, out_shape=(small, small, big, big), name="ssm_disc",
    )(lre, lim, ldt, bre, bim)


def _ssm_weights(a_re, a_im, bb_re, bb_im, c_re, c_im):
    def bb_tiles(bb):
        return bb.reshape(DEPTH, 2, 2, 4, 4, SSM_GROUP_CH, SSM_STATE)
    bb = jnp.stack([bb_tiles(bb_re), bb_tiles(bb_im)], axis=2)
    sel = np.zeros((4, 8, 4), np.float32)
    for m in range(4):
        for j in range(4):
            sel[m, 4 * (m % 2) + j, j] = 1.0
    bbt = jnp.einsum('ldrhmjcp,mkj->ldrmhkcjp', bb, jnp.asarray(sel))
    bbt = bbt.reshape(DEPTH, 2, 8, 2 * LANES, 2 * LANES).astype(BF16)
    def c_tiles(c):
        return c.reshape(DEPTH, 2, 2, 2, 8, SSM_GROUP_CH, SSM_STATE)
    cc = jnp.stack([c_tiles(c_re), -c_tiles(c_im)], axis=2)
    eye = jnp.eye(8, dtype=F32)
    cct = jnp.einsum('ldrhmkop,kj->ldmrkphjo', cc, eye)
    cct = cct.reshape(DEPTH, 2, 2, 2 * 8 * SSM_STATE, 2 * LANES).astype(BF16)
    def a_rows(a):
        a = a.reshape(DEPTH, 2, 2, HALF_STATE)
        return jnp.tile(a, (1, 1, 4, 1))
    a8 = jnp.stack([a_rows(a_re), a_rows(a_im)], axis=2)
    return bbt, cct, a8


def _inproj_kernel(x_ref, mod_ref, g_ref, w_ref, z_ref):
    h = _mod_norm(x_ref[...], g_ref[...], mod_ref[:, 0:D_MODEL], mod_ref[:, D_MODEL:2 * D_MODEL])
    h = h.astype(BF16)
    for c in range(IN_COLS // Z_TILE):
        cols = slice(c * Z_TILE, (c + 1) * Z_TILE)
        z_ref[:, cols] = jnp.dot(h, w_ref[:, cols], preferred_element_type=F32).astype(BF16)


def _inproj_call(x, mod, g1, w_in, *, layer, seq, tm):
    t = x.shape[0]
    per = max(seq // tm, 1)
    mod_map = (lambda i: (0, 0, 0)) if tm >= seq else (lambda i: (i // per, 0, 0))
    return pl.pallas_call(
        _inproj_kernel,
        grid=(t // tm,),
        in_specs=[pl.BlockSpec((tm, D_MODEL), lambda i: (i, 0)),
                  pl.BlockSpec((None, 1, 6 * D_MODEL), mod_map),
                  pl.BlockSpec((1, D_MODEL), lambda i: (0, 0)),
                  pl.BlockSpec((None, D_MODEL, IN_COLS), lambda i: (layer, 0, 0),
                               pipeline_mode=pl.Buffered(1))],
        out_specs=pl.BlockSpec((tm, IN_COLS), lambda i: (i, 0)),
        out_shape=jax.ShapeDtypeStruct((t, IN_COLS), BF16),
        compiler_params=_cparams(("arbitrary",)),
        name="in_proj",
    )(x, mod, g1, w_in)


def _head_ssq(x, bd):
    parts = []
    for t in range(x.shape[1] // (2 * LANES)):
        xs = x[:, 2 * LANES * t:2 * LANES * (t + 1)]
        parts.append(jnp.dot((xs * xs).astype(BF16), bd, preferred_element_type=F32))
    return parts[0] if len(parts) == 1 else jnp.concatenate(parts, axis=1)


def _rope(x, cos, sin_up, sin_dn):
    w = x.shape[1]
    rep = w // LANES
    cos = jnp.tile(cos, (1, rep))
    sin_up = jnp.tile(sin_up, (1, rep))
    sin_dn = jnp.tile(sin_dn, (1, rep))
    return x * cos + pltpu.roll(x, w - 16, 1) * sin_up + pltpu.roll(x, 16, 1) * sin_dn


def _dup_heads(x, lane_lo):
    xr = pltpu.roll(x, HEAD_DIM, 1)
    return jnp.where(lane_lo, x, xr), jnp.where(lane_lo, xr, x)


def _attn_kernel(*refs, seq, past, rope, emit_kv):
    it = iter(refs)
    q_ref, kv_ref = next(it), next(it)
    kc_ref = vc_ref = cos_ref = sup_ref = sdn_ref = cosk_ref = supk_ref = sdnk_ref = None
    if past:
        kc_ref, vc_ref = next(it), next(it)
    qg_ref, kg_ref, bd_ref = next(it), next(it), next(it)
    if rope:
        cos_ref, sup_ref, sdn_ref = next(it), next(it), next(it)
        cosk_ref, supk_ref, sdnk_ref = next(it), next(it), next(it)
    o_ref = next(it)
    kout_ref = vout_ref = None
    if emit_kv:
        kout_ref, vout_ref = next(it), next(it)
    k2_scr, v2t_scr, qst_scr, s_scr, p_scr, linv_scr, o_scr = (next(it) for _ in range(7))

    qi = pl.program_id(1)
    bd = bd_ref[...]

    @pl.when(qi == 0)
    def _prep():
        kv = kv_ref[...].astype(F32)
        k = kv[:, :2 * LANES]
        v = kv[:, 2 * LANES:]
        kn = k * lax.rsqrt(_head_ssq(k, bd) * (1.0 / HEAD_DIM) + RMS_EPS) * kg_ref[...]
        if emit_kv:
            kout_ref[...] = kn
            vout_ref[...] = v
        if rope:
            kn = _rope(kn, cosk_ref[...], supk_ref[...], sdnk_ref[...])
        lane_lo = lax.broadcasted_iota(jnp.int32, (seq, LANES), 1) < HEAD_DIM
        for t in range(2):
            ka, kb = _dup_heads(kn[:, LANES * t:LANES * (t + 1)], lane_lo)
            va, vb = _dup_heads(v[:, LANES * t:LANES * (t + 1)], lane_lo)
            k2_scr[2 * t, 0:seq, :] = ka.astype(BF16)
            k2_scr[2 * t + 1, 0:seq, :] = kb.astype(BF16)
            v2t_scr[2 * t, :, 0:seq] = va.T.astype(BF16)
            v2t_scr[2 * t + 1, :, 0:seq] = vb.T.astype(BF16)
        if past:
            kc = kc_ref[...]
            vc = vc_ref[...]
            lane_lo_p = lax.broadcasted_iota(jnp.int32, (past, LANES), 1) < HEAD_DIM
            for t in range(2):
                ka, kb = _dup_heads(kc[:, LANES * t:LANES * (t + 1)], lane_lo_p)
                va, vb = _dup_heads(vc[:, LANES * t:LANES * (t + 1)], lane_lo_p)
                k2_scr[2 * t, seq:seq + past, :] = ka.astype(BF16)
                k2_scr[2 * t + 1, seq:seq + past, :] = kb.astype(BF16)
                v2t_scr[2 * t, :, seq:seq + past] = va.T.astype(BF16)
                v2t_scr[2 * t + 1, :, seq:seq + past] = vb.T.astype(BF16)

    q = q_ref[...].astype(F32)
    tq = q.shape[0]
    qn = q * lax.rsqrt(_head_ssq(q, bd) * (1.0 / HEAD_DIM) + RMS_EPS) * qg_ref[...]
    if rope:
        qn = _rope(qn, cos_ref[...], sup_ref[...], sdn_ref[...])
    qn = qn * (HEAD_DIM ** -0.5 * math.log2(math.e))

    n_tb = tq // ATT_TOK
    n_units = N_KV_HEADS * n_tb
    keys = k2_scr.shape[1]
    n_chunks = keys // ATT_KEYS
    lane_lo = lax.broadcasted_iota(jnp.int32, (ATT_TOK, LANES), 1) < HEAD_DIM
    for kvh in range(N_KV_HEADS):
        for tb in range(n_tb):
            stack = []
            for t in (2 * kvh, 2 * kvh + 1):
                qt = qn[tb * ATT_TOK:(tb + 1) * ATT_TOK, LANES * t:LANES * (t + 1)]
                stack += [jnp.where(lane_lo, qt, 0.0), jnp.where(lane_lo, 0.0, qt)]
            qst_scr[kvh * n_tb + tb] = jnp.concatenate(stack, axis=0).astype(BF16)

    def kv_of(u):
        return u // n_tb if isinstance(u, int) else lax.shift_right_logical(u, n_tb.bit_length() - 1)

    def scores(u, slot):
        s_scr[slot] = lax.dot_general(k2_scr[kv_of(u)], qst_scr[u], (((1,), (1,)), ((), ())),
                                      preferred_element_type=F32)

    def softmax(slot):
        m_run = s_scr[slot, 0:ATT_KEYS, :]
        for c in range(1, n_chunks):
            m_run = jnp.maximum(m_run, s_scr[slot, c * ATT_KEYS:(c + 1) * ATT_KEYS, :])
        m = jnp.broadcast_to(jnp.max(m_run, axis=0, keepdims=True), m_run.shape)
        l_run = jnp.zeros_like(m_run)
        for c in range(n_chunks):
            p = jnp.exp2(s_scr[slot, c * ATT_KEYS:(c + 1) * ATT_KEYS, :] - m)
            l_run = l_run + p
            p_scr[slot, c * ATT_KEYS:(c + 1) * ATT_KEYS, :] = p.astype(BF16)
        l = jnp.sum(l_run, axis=0, keepdims=True)
        linv_scr[slot] = jnp.broadcast_to(1.0 / l, linv_scr.shape[1:])

    def values(u, slot):
        ot = jnp.dot(v2t_scr[kv_of(u)], p_scr[slot], preferred_element_type=F32)
        o_scr[u] = (ot * linv_scr[slot, 0:1, :]).T

    n_waves = n_units // ATT_WAVE
    assert n_tb & (n_tb - 1) == 0 and n_units % ATT_WAVE == 0 and n_waves % 2 == 0 and n_waves >= 4

    def wave(g, par, do_scores=False, do_softmax=False, do_values=False):
        for w in range(ATT_WAVE):
            u, slot = g * ATT_WAVE + w, par * ATT_WAVE + w
            if do_scores:
                scores(u, slot)
            if do_softmax:
                softmax(slot)
            if do_values:
                values(u, slot)

    wave(0, 0, do_scores=True)
    wave(0, 0, do_softmax=True)
    wave(1, 1, do_scores=True)

    def pipelined(i, carry):
        g = 2 * i + 1
        wave(g + 1, 0, do_scores=True)
        wave(g, 1, do_softmax=True)
        wave(g - 1, 0, do_values=True)
        wave(g + 2, 1, do_scores=True)
        wave(g + 1, 0, do_softmax=True)
        wave(g, 1, do_values=True)
        return carry

    lax.fori_loop(0, (n_waves - 2) // 2, pipelined, 0)
    wave(n_waves - 1, 1, do_softmax=True)
    wave(n_waves - 2, 0, do_values=True)
    wave(n_waves - 1, 1, do_values=True)

    for kvh in range(N_KV_HEADS):
        for tb in range(n_tb):
            u = kvh * n_tb + tb
            for pair, t in enumerate((2 * kvh, 2 * kvh + 1)):
                r0 = 2 * pair * ATT_TOK
                o_ref[tb * ATT_TOK:(tb + 1) * ATT_TOK, LANES * t:LANES * (t + 1)] = jnp.where(
                    lane_lo, o_scr[u, r0:r0 + ATT_TOK], o_scr[u, r0 + ATT_TOK:r0 + 2 * ATT_TOK]).astype(BF16)


def _attn_call(z, kc, vc, qg, kg, bd, tabs, *, nb, seq, tq, past, emit_kv):
    rope = tabs is not None
    nq = seq // tq
    keys = seq + past
    n_units = N_KV_HEADS * (tq // ATT_TOK)
    q_col = 3072 // D_MODEL
    kv_col = 4096 // 512
    in_specs = [pl.BlockSpec((tq, D_MODEL), lambda b, i: (b * nq + i, q_col)),
                pl.BlockSpec((seq, 512), lambda b, i: (b, kv_col))]
    args = [z, z]
    if past:
        in_specs += [pl.BlockSpec((None, past, 2 * LANES), lambda b, i: (b, 0, 0))] * 2
        args += [kc, vc]
    in_specs += [pl.BlockSpec((1, D_MODEL), lambda b, i: (0, 0)),
                 pl.BlockSpec((1, 2 * LANES), lambda b, i: (0, 0)),
                 pl.BlockSpec((2 * LANES, 2 * LANES), lambda b, i: (0, 0))]
    args += [qg, kg, bd]
    if rope:
        in_specs += [pl.BlockSpec((tq, LANES), lambda b, i: (i, 0))] * 3
        in_specs += [pl.BlockSpec((seq, LANES), lambda b, i: (0, 0))] * 3
        args += list(tabs) + list(tabs)
    out_specs = [pl.BlockSpec((tq, D_MODEL), lambda b, i: (b * nq + i, 0))]
    out_shape = [jax.ShapeDtypeStruct((nb * seq, D_MODEL), BF16)]
    if emit_kv:
        out_specs += [pl.BlockSpec((None, seq, 2 * LANES), lambda b, i: (b, 0, 0))] * 2
        out_shape += [jax.ShapeDtypeStruct((nb, seq, 2 * LANES), F32)] * 2
    return pl.pallas_call(
        functools.partial(_attn_kernel, seq=seq, past=past, rope=rope, emit_kv=emit_kv),
        grid=(nb, nq),
        in_specs=in_specs, out_specs=out_specs, out_shape=out_shape,
        scratch_shapes=[pltpu.VMEM((N_KV_HEADS, keys, LANES), BF16),
                        pltpu.VMEM((N_KV_HEADS, LANES, keys), BF16),
                        pltpu.VMEM((n_units, 4 * ATT_TOK, LANES), BF16),
                        pltpu.VMEM((2 * ATT_WAVE, keys, 4 * ATT_TOK), F32),
                        pltpu.VMEM((2 * ATT_WAVE, keys, 4 * ATT_TOK), BF16),
                        pltpu.VMEM((2 * ATT_WAVE, SUBLANES, 4 * ATT_TOK), F32),
                        pltpu.VMEM((n_units, 4 * ATT_TOK, LANES), F32)],
        compiler_params=_cparams(("arbitrary", "arbitrary")),
        name="attention",
    )(*args)


def _rope_tables(length):
    f = HEAD_DIM // 4
    inv = ROPE_THETA ** (-np.arange(f, dtype=np.float32) / f)
    pos = np.arange(length)
    row = (pos // GRID_W).astype(np.float32)
    col = (pos % GRID_W).astype(np.float32)
    ang_r = row[:, None] * inv[None, :]
    ang_c = col[:, None] * inv[None, :]
    z = np.zeros_like(ang_r)
    cos64 = np.concatenate([np.cos(ang_r), np.cos(ang_r), np.cos(ang_c), np.cos(ang_c)], axis=1)
    up64 = np.concatenate([-np.sin(ang_r), z, -np.sin(ang_c), z], axis=1)
    dn64 = np.concatenate([z, np.sin(ang_r), z, np.sin(ang_c)], axis=1)
    tile2 = lambda a: jnp.asarray(np.tile(a, (1, 2)), dtype=F32)
    return tile2(cos64), tile2(up64), tile2(dn64)


def _ssm_kernel(uf_ref, ub_ref, pin_ref, pout_ref, bbt_ref, cct_ref, a_ref, h0_ref,
                yf_ref, yb_ref, ht_ref, x_scr, hc_scr, *, rows_per_step, steps):
    j = pl.program_id(1)
    r = rows_per_step
    rows = r * steps
    nbg = r // 2
    tok = nbg * steps

    @pl.when(j == 0)
    def _():
        hc_scr[...] = h0_ref[...]

    first_half = (lax.broadcasted_iota(jnp.int32, (rows, LANES), 0) & 1) == 0
    width = SUBLANES * HALF_STATE // r
    for d, (u_ref, y_ref) in enumerate(((uf_ref, yf_ref), (ub_ref, yb_ref))):
        u_bt = u_ref[...].reshape(tok, SSM_W)
        u = jnp.dot(pin_ref[...], u_bt, preferred_element_type=F32)
        kcat = []
        for q in range(2):
            u0 = u[:, LANES * q:LANES * (q + 1)]
            u1 = u[:, 2 * LANES + LANES * q:2 * LANES + LANES * (q + 1)]
            kcat.append(jnp.concatenate([jnp.where(first_half, u0, 0.0),
                                         jnp.where(first_half, 0.0, u1)], axis=1).astype(BF16))
        for n in range(8):
            x_scr[d, :, 2 * LANES * n:2 * LANES * (n + 1)] = jnp.dot(
                kcat[(n % 4) // 2], bbt_ref[d, n], preferred_element_type=F32)

        for p in range(HALF_STATE // width):
            re_cols = slice(p * width, (p + 1) * width)
            im_cols = slice(HALF_STATE + p * width, HALF_STATE + (p + 1) * width)
            reps = r // SUBLANES
            a_re = jnp.tile(a_ref[d, 0, :, re_cols], (reps, 1))
            a_im = jnp.tile(a_ref[d, 1, :, re_cols], (reps, 1))
            h_re = hc_scr[d, :, re_cols]
            h_im = hc_scr[d, :, im_cols]
            for kk in range(steps):
                k = kk if d == 0 else steps - 1 - kk
                srows = slice(k * r, (k + 1) * r)
                n_re = a_re * h_re - a_im * h_im + x_scr[d, srows, re_cols]
                n_im = a_re * h_im + a_im * h_re + x_scr[d, srows, im_cols]
                x_scr[d, srows, re_cols] = n_re
                x_scr[d, srows, im_cols] = n_im
                h_re, h_im = n_re, n_im
            hc_scr[d, :, re_cols] = h_re
            hc_scr[d, :, im_cols] = h_im

        yy = []
        for m in range(2):
            hk = jnp.concatenate(
                [x_scr[d, :, 512 * m:512 * (m + 1)],
                 x_scr[d, :, HALF_STATE + 512 * m:HALF_STATE + 512 * (m + 1)]], axis=1).astype(BF16)
            yy.append(jnp.dot(hk, cct_ref[d, m], preferred_element_type=F32))
        for hh in range(2):
            ysel = jnp.concatenate([yy[0][:, LANES * hh:LANES * (hh + 1)],
                                    yy[1][:, LANES * hh:LANES * (hh + 1)]], axis=1).astype(BF16)
            y_bt = jnp.dot(pout_ref[hh], ysel, preferred_element_type=F32)
            y_ref[:, :, 2 * LANES * hh:2 * LANES * (hh + 1)] = (
                y_bt.reshape(nbg, steps, 2 * LANES).astype(BF16))

    @pl.when(j == pl.num_programs(1) - 1)
    def _():
        ht_ref[...] = hc_scr[...]


def _ssm_perms(nbg, steps):
    tok = nbg * steps
    pin = np.zeros((2 * tok, tok), np.float32)
    for t in range(steps):
        for b in range(nbg):
            for h in range(2):
                pin[(t * nbg + b) * 2 + h, b * steps + t] = 1.0
    pout = np.zeros((2, tok, 2 * tok), np.float32)
    for h in range(2):
        pout[h] = (pin * (np.arange(2 * tok)[:, None] % 2 == h)).T
    return jnp.asarray(pin, dtype=BF16), jnp.asarray(pout, dtype=BF16)


def _ssm_call(z, bbt, cct, a8, h0, *, nb, seq, nbg):
    r = 2 * nbg
    steps = SSM_ROWS // r
    n = seq // steps
    ngrp = nb // nbg
    z3 = z.reshape(nb, seq, IN_COLS)
    pin, pout = _ssm_perms(nbg, steps)
    ys = jax.ShapeDtypeStruct((nb, seq, SSM_W), BF16)
    ublk = (nbg, steps, SSM_W)
    ucol = 6144 // SSM_W
    hblk = (2, r, 2 * HALF_STATE)
    c4 = lambda g, j: (0, 0, 0, 0)
    return pl.pallas_call(
        functools.partial(_ssm_kernel, rows_per_step=r, steps=steps),
        grid=(ngrp, n),
        in_specs=[pl.BlockSpec(ublk, lambda g, j: (g, j, ucol)),
                  pl.BlockSpec(ublk, lambda g, j: (g, n - 1 - j, ucol)),
                  pl.BlockSpec(pin.shape, lambda g, j: (0, 0)),
                  pl.BlockSpec(pout.shape, lambda g, j: (0, 0, 0)),
                  pl.BlockSpec(bbt.shape, c4),
                  pl.BlockSpec(cct.shape, c4),
                  pl.BlockSpec(a8.shape, c4),
                  pl.BlockSpec(hblk, lambda g, j: (0, g, 0))],
        out_specs=[pl.BlockSpec(ublk, lambda g, j: (g, j, 0)),
                   pl.BlockSpec(ublk, lambda g, j: (g, n - 1 - j, 0)),
                   pl.BlockSpec(hblk, lambda g, j: (0, g, 0))],
        out_shape=[ys, ys, jax.ShapeDtypeStruct(h0.shape, F32)],
        scratch_shapes=[pltpu.VMEM((2, SSM_ROWS, 2 * HALF_STATE), F32),
                        pltpu.VMEM(hblk, F32)],
        compiler_params=_cparams(("arbitrary", "arbitrary")),
        name="ssm_scan",
    )(z3, z3, pin, pout, bbt, cct, a8, h0)


def _merge_kernel(x_ref, mod_ref, zg_ref, zc_ref, zu_ref, cprev_ref, cnext_ref,
                  ya_ref, yf_ref, yb_ref, cw_ref, cb_ref, sd_ref, wglu_ref, bglu_ref,
                  wpc_ref, wpa_ref, wps_ref, wo_ref, o_ref, *, seq):
    i = pl.program_id(0)
    tm = x_ref.shape[0]
    zc = zc_ref[...].astype(F32)
    cb = zc[:, 0:CONV_W]
    prod = zc[:, CONV_W:2 * CONV_W] * zc[:, 2 * CONV_W:3 * CONV_W]
    cp = cprev_ref[...].astype(F32)
    cn = cnext_ref[...].astype(F32)
    halo_prev = cp[7:8, CONV_W:2 * CONV_W] * cp[7:8, 2 * CONV_W:3 * CONV_W]
    halo_next = cn[0:1, CONV_W:2 * CONV_W] * cn[0:1, 2 * CONV_W:3 * CONV_W]
    row = lax.broadcasted_iota(jnp.int32, (tm, CONV_W), 0)
    tpos = (i * tm + row) % seq
    prev = jnp.where(row == 0, halo_prev, pltpu.roll(prod, 1, 0))
    prev = jnp.where(tpos == 0, 0.0, prev)
    nxt = jnp.where(row == tm - 1, halo_next, pltpu.roll(prod, tm - 1, 0))
    nxt = jnp.where(tpos == seq - 1, 0.0, nxt)
    cw = cw_ref[...]
    y_conv = cb * (cw[0:1] * prev + cw[1:2] * prod + cw[2:3] * nxt + cb_ref[...])

    u = zu_ref[...].astype(F32)
    y = jax.nn.gelu(yf_ref[...].astype(F32) + yb_ref[...].astype(F32) + sd_ref[...] * u)
    y_ssm = y * jax.nn.sigmoid(
        jnp.dot(y.astype(BF16), wglu_ref[...], preferred_element_type=F32) + bglu_ref[...])

    zg = zg_ref[...].astype(F32)
    merged = (jax.nn.sigmoid(zg[:, 0:D_MODEL])
              * jnp.dot(y_conv.astype(BF16), wpc_ref[...], preferred_element_type=F32)
              + jax.nn.sigmoid(zg[:, D_MODEL:2 * D_MODEL])
              * jnp.dot(ya_ref[...], wpa_ref[...], preferred_element_type=F32)
              + jax.nn.sigmoid(zg[:, 2 * D_MODEL:3 * D_MODEL])
              * jnp.dot(y_ssm.astype(BF16), wps_ref[...], preferred_element_type=F32))
    gate1 = mod_ref[:, 2 * D_MODEL:3 * D_MODEL]
    o_ref[...] = x_ref[...] + gate1 * jnp.dot(merged.astype(BF16), wo_ref[...],
                                              preferred_element_type=F32)


def _merge_call(x, mod, z, y_attn, yf, yb, lp, *, layer, seq, tm):
    t = x.shape[0]
    n_m = t // tm
    per = max(seq // tm, 1)
    hb = tm // SUBLANES
    n_hb = t // SUBLANES
    mod_map = (lambda i: (0, 0, 0)) if tm >= seq else (lambda i: (i // per, 0, 0))
    y_spec = pl.BlockSpec((tm, SSM_W), lambda i: (i, 0))
    yf2 = yf.reshape(t, SSM_W)
    yb2 = yb.reshape(t, SSM_W)
    const2 = lambda i: (0, 0)
    lslab = lambda i: (layer, 0, 0)
    resident = dict(pipeline_mode=pl.Buffered(1))
    return pl.pallas_call(
        functools.partial(_merge_kernel, seq=seq),
        grid=(n_m,),
        in_specs=[pl.BlockSpec((tm, D_MODEL), lambda i: (i, 0)),
                  pl.BlockSpec((None, 1, 6 * D_MODEL), mod_map),
                  pl.BlockSpec((tm, 3 * D_MODEL), lambda i: (i, 0)),
                  pl.BlockSpec((tm, 3 * CONV_W), lambda i: (i, 3)),
                  pl.BlockSpec((tm, SSM_W), lambda i: (i, 6144 // SSM_W)),
                  pl.BlockSpec((SUBLANES, 3 * CONV_W), lambda i: (jnp.maximum(i * hb - 1, 0), 3)),
                  pl.BlockSpec((SUBLANES, 3 * CONV_W),
                               lambda i: (jnp.minimum((i + 1) * hb, n_hb - 1), 3)),
                  pl.BlockSpec((tm, D_MODEL), lambda i: (i, 0)),
                  y_spec, y_spec,
                  pl.BlockSpec((3, CONV_W), const2),
                  pl.BlockSpec((1, CONV_W), const2),
                  pl.BlockSpec((1, SSM_W), const2),
                  pl.BlockSpec((None, SSM_W, SSM_W), lslab, **resident),
                  pl.BlockSpec((1, SSM_W), const2),
                  pl.BlockSpec((None, CONV_W, D_MODEL), lslab, **resident),
                  pl.BlockSpec((None, D_MODEL, D_MODEL), lslab, **resident),
                  pl.BlockSpec((None, SSM_W, D_MODEL), lslab, **resident),
                  pl.BlockSpec((None, D_MODEL, D_MODEL), lslab, **resident)],
        out_specs=pl.BlockSpec((tm, D_MODEL), lambda i: (i, 0)),
        out_shape=jax.ShapeDtypeStruct((t, D_MODEL), F32),
        compiler_params=_cparams(("arbitrary",)),
        name="merge",
    )(x, mod, z, z, z, z, z, y_attn, yf2, yb2,
      lp['conv_w'], lp['conv_b'], lp['ssm_d'], lp['w_glu'], lp['b_glu'],
      lp['w_proj_conv'], lp['w_proj_attn'], lp['w_proj_ssm'], lp['w_out'])


def _ffn_kernel(x_ref, mod_ref, g_ref, wg_ref, wu_ref, wd_ref, fg_ref, o_ref, *, final, tf):
    x = x_ref[...]
    h = _mod_norm(x, g_ref[...], mod_ref[:, 3 * D_MODEL:4 * D_MODEL],
                  mod_ref[:, 4 * D_MODEL:5 * D_MODEL]).astype(BF16)
    acc = None
    for c in range(D_FF // tf):
        cols = slice(c * tf, (c + 1) * tf)
        g = jnp.dot(h, wg_ref[:, cols], preferred_element_type=F32)
        u = jnp.dot(h, wu_ref[:, cols], preferred_element_type=F32)
        part = jnp.dot((jax.nn.silu(g) * u).astype(BF16), wd_ref[cols, :], preferred_element_type=F32)
        acc = part if acc is None else acc + part
    xn = x + mod_ref[:, 5 * D_MODEL:6 * D_MODEL] * acc
    if final:
        ms = jnp.mean(xn * xn, axis=-1, keepdims=True)
        xn = xn * lax.rsqrt(ms + RMS_EPS) * fg_ref[...]
    o_ref[...] = xn


def _ffn_call(x, mod, g2, wg, wu, wd, fg, *, layer, seq, tm, tf, final):
    t = x.shape[0]
    per = max(seq // tm, 1)
    mod_map = (lambda i: (0, 0, 0)) if tm >= seq else (lambda i: (i // per, 0, 0))
    lslab = lambda i: (layer, 0, 0)
    resident = dict(pipeline_mode=pl.Buffered(1))
    return pl.pallas_call(
        functools.partial(_ffn_kernel, final=final, tf=tf),
        grid=(t // tm,),
        in_specs=[pl.BlockSpec((tm, D_MODEL), lambda i: (i, 0)),
                  pl.BlockSpec((None, 1, 6 * D_MODEL), mod_map),
                  pl.BlockSpec((1, D_MODEL), lambda i: (0, 0)),
                  pl.BlockSpec((None, D_MODEL, D_FF), lslab, **resident),
                  pl.BlockSpec((None, D_MODEL, D_FF), lslab, **resident),
                  pl.BlockSpec((None, D_FF, D_MODEL), lslab, **resident),
                  pl.BlockSpec((1, D_MODEL), lambda i: (0, 0))],
        out_specs=pl.BlockSpec((tm, D_MODEL), lambda i: (i, 0)),
        out_shape=jax.ShapeDtypeStruct((t, D_MODEL), F32),
        compiler_params=_cparams(("arbitrary",)),
        name="ffn",
    )(x, mod, g2, wg, wu, wd, fg)


def _state_rows(s):
    b = s.shape[0]
    return jnp.transpose(s, (1, 0, 2, 3)).reshape(2, 2 * b, HALF_STATE)


def _state_unrows(s, b):
    return jnp.transpose(s.reshape(2, b, SSM_GROUPS, SSM_STATE), (1, 0, 2, 3))


def kernel(x_prompt, x_sample, c, cache_k, cache_v, state_ssm_re, state_ssm_im, c_ctx, w_ada, b_ada, norm1_g, w_in, conv_w, conv_b, q_norm_g, k_norm_g, ssm_lambda_re, ssm_lambda_im, ssm_b_re, ssm_b_im, ssm_c_re, ssm_c_im, ssm_log_dt, ssm_d, w_glu, b_glu, w_proj_conv, w_proj_attn, w_proj_ssm, w_out, norm2_g, w_ffn_gate, w_ffn_up, w_ffn_down, final_norm_g):
    nb_c, seq_c, _ = x_prompt.shape
    nb_l, seq_l, _ = x_sample.shape
    past = cache_k.shape[2]

    cvec = jnp.zeros((8, D_MODEL), F32).at[0].set(c_ctx).at[1:1 + nb_l].set(c)
    mod_all = _ada_call(cvec, w_ada, b_ada)

    a_re, a_im, bb_re, bb_im = _disc_call(ssm_lambda_re, ssm_lambda_im, ssm_log_dt, ssm_b_re, ssm_b_im)
    bbt, cct, a8 = _ssm_weights(a_re, a_im, bb_re, bb_im, ssm_c_re, ssm_c_im)

    w_in_p = jnp.concatenate([w_in[:, :, 3584:6656], w_in[:, :, 1536:2560], w_in[:, :, 2560:3072],
                              w_in[:, :, 0:1536], w_in[:, :, 3072:3584]], axis=2).astype(BF16)
    wb = {k: v.astype(BF16) for k, v in dict(
        w_glu=w_glu, w_proj_conv=w_proj_conv, w_proj_attn=w_proj_attn, w_proj_ssm=w_proj_ssm,
        w_out=w_out, w_ffn_gate=w_ffn_gate, w_ffn_up=w_ffn_up, w_ffn_down=w_ffn_down).items()}

    bd = jnp.asarray(np.kron(np.eye(2 * LANES // HEAD_DIM, dtype=np.float32),
                             np.ones((HEAD_DIM, HEAD_DIM), np.float32)), dtype=BF16)
    tabs = _rope_tables(seq_l)
    fg = final_norm_g.reshape(1, D_MODEL)

    xc = x_prompt.reshape(nb_c * seq_c, D_MODEL)
    xl = x_sample.reshape(nb_l * seq_l, D_MODEL)
    h0_zero = jnp.zeros((2, 2 * nb_c, 2 * HALF_STATE), F32)
    ks, vs, srs, sis = [], [], [], []

    for l in range(DEPTH):
        lp = dict(conv_w=conv_w[l], conv_b=conv_b[l].reshape(1, CONV_W),
                  ssm_d=ssm_d[l].reshape(1, SSM_W), b_glu=b_glu[l].reshape(1, SSM_W),
                  w_glu=wb['w_glu'], w_proj_conv=wb['w_proj_conv'],
                  w_proj_attn=wb['w_proj_attn'], w_proj_ssm=wb['w_proj_ssm'],
                  w_out=wb['w_out'])
        g1 = norm1_g[l].reshape(1, D_MODEL)
        g2 = norm2_g[l].reshape(1, D_MODEL)
        qg = jnp.tile(q_norm_g[l], N_HEADS).reshape(1, D_MODEL)
        kg = jnp.tile(k_norm_g[l], N_KV_HEADS).reshape(1, 2 * LANES)
        mod_c = mod_all[l, 0:1].reshape(1, 1, 6 * D_MODEL)
        mod_l = mod_all[l, 1:1 + nb_l].reshape(nb_l, 1, 6 * D_MODEL)
        final = l == DEPTH - 1

        z = _inproj_call(xc, mod_c, g1, w_in_p, layer=l, seq=seq_c, tm=512)
        y_attn, k_l, v_l = _attn_call(z, None, None, qg, kg, bd, None, nb=nb_c, seq=seq_c,
                                      tq=seq_c, past=0, emit_kv=True)
        yf, yb, ht = _ssm_call(z, bbt[l], cct[l], a8[l], h0_zero, nb=nb_c, seq=seq_c, nbg=16)
        xc = _merge_call(xc, mod_c, z, y_attn, yf, yb, lp, layer=l, seq=seq_c, tm=512)
        xc = _ffn_call(xc, mod_c, g2, wb['w_ffn_gate'], wb['w_ffn_up'], wb['w_ffn_down'],
                       fg, layer=l, seq=seq_c, tm=512, tf=1408, final=final)
        ks.append(k_l.reshape(nb_c, seq_c, N_KV_HEADS, HEAD_DIM))
        vs.append(v_l.reshape(nb_c, seq_c, N_KV_HEADS, HEAD_DIM))
        srs.append(_state_unrows(ht[:, :, :HALF_STATE], nb_c))
        sis.append(_state_unrows(ht[:, :, HALF_STATE:], nb_c))

        h0 = jnp.concatenate([_state_rows(state_ssm_re[:, l]), _state_rows(state_ssm_im[:, l])], axis=2)
        z = _inproj_call(xl, mod_l, g1, w_in_p, layer=l, seq=seq_l, tm=512)
        (y_attn,) = _attn_call(z, cache_k[:, l].reshape(nb_l, past, 2 * LANES),
                               cache_v[:, l].reshape(nb_l, past, 2 * LANES), qg, kg, bd, tabs,
                               nb=nb_l, seq=seq_l, tq=512, past=past, emit_kv=False)
        yf, yb, _ = _ssm_call(z, bbt[l], cct[l], a8[l], h0, nb=nb_l, seq=seq_l, nbg=nb_l)
        xl = _merge_call(xl, mod_l, z, y_attn, yf, yb, lp, layer=l, seq=seq_l, tm=512)
        xl = _ffn_call(xl, mod_l, g2, wb['w_ffn_gate'], wb['w_ffn_up'], wb['w_ffn_down'],
                       fg, layer=l, seq=seq_l, tm=512, tf=1408, final=final)

    y_prompt = xc.reshape(nb_c, seq_c, D_MODEL)
    y_sample = xl.reshape(nb_l, seq_l, D_MODEL)
    return (y_prompt, y_sample, jnp.stack(ks, axis=1), jnp.stack(vs, axis=1),
            jnp.stack(srs, axis=1), jnp.stack(sis, axis=1))
```

```python
import functools
import math

import jax
import jax.numpy as jnp
import numpy as np
from jax import lax
from jax.experimental import pallas as pl
from jax.experimental.pallas import tpu as pltpu

F32 = jnp.float32
BF16 = jnp.bfloat16

D_MODEL = 1024
DEPTH = 4
GRID_W = 64
HEAD_DIM = 64
N_HEADS = 16
N_KV_HEADS = 4
ROPE_THETA = 10000.0
CONV_W = 512
SSM_W = 512
SSM_GROUP_CH = 16
SSM_GROUPS = 32
SSM_STATE = 64
D_FF = 2816
RMS_EPS = 1e-6
IN_COLS = 6656

LANES = 128
SUBLANES = 8
VMEM_LIMIT = 56 * 1024 * 1024

Z_TILE = 512
Z_SRC = (3584, 4096, 4608, 5120, 5632, 6144, 1536, 2048, 2560, 0, 512, 1024, 3072)
HALF_STATE = SSM_GROUPS // 2 * SSM_STATE
SSM_ROWS = 512
VT_ROWS = HEAD_DIM + 16


def _cparams(sem):
    return pltpu.CompilerParams(dimension_semantics=sem, vmem_limit_bytes=VMEM_LIMIT)


def _mod_norm(x, g, shift, scale):
    ms = jnp.mean(x * x, axis=-1, keepdims=True)
    y = x * lax.rsqrt(ms + RMS_EPS) * g
    return y * (1.0 + scale) + shift


def _ada_kernel(c_ref, w_ref, b_ref, o_ref):
    s = jax.nn.silu(c_ref[...])
    o_ref[...] = jnp.dot(s.astype(BF16), w_ref[...].astype(BF16),
                         preferred_element_type=F32) + b_ref[...]


def _ada_call(cvec, w_ada, b_ada):
    tn = 1536
    return pl.pallas_call(
        _ada_kernel,
        grid=(DEPTH, 6 * D_MODEL // tn),
        in_specs=[pl.BlockSpec((8, D_MODEL), lambda l, j: (0, 0)),
                  pl.BlockSpec((None, D_MODEL, tn), lambda l, j: (l, 0, j)),
                  pl.BlockSpec((None, 1, tn), lambda l, j: (l, 0, j))],
        out_specs=pl.BlockSpec((None, 8, tn), lambda l, j: (l, 0, j)),
        out_shape=jax.ShapeDtypeStruct((DEPTH, 8, 6 * D_MODEL), F32),
        compiler_params=_cparams(("arbitrary", "arbitrary")),
        name="ada_mod",
    )(cvec, w_ada, b_ada.reshape(DEPTH, 1, 6 * D_MODEL))


def _disc_kernel(lre_ref, lim_ref, ldt_ref, bre_ref, bim_ref,
                 are_ref, aim_ref, bbre_ref, bbim_ref):
    lre = lre_ref[...]
    lim = lim_ref[...]
    dt = jnp.exp(ldt_ref[...])
    mag = jnp.exp(lre * dt)
    a_re = mag * jnp.cos(lim * dt)
    a_im = mag * jnp.sin(lim * dt)
    den = lre * lre + lim * lim
    n_re = a_re - 1.0
    coef_re = (n_re * lre + a_im * lim) / den
    coef_im = (a_im * lre - n_re * lim) / den
    are_ref[...] = a_re
    aim_ref[...] = a_im
    bre = bre_ref[...]
    bim = bim_ref[...]
    bbre_ref[...] = coef_re * bre - coef_im * bim
    bbim_ref[...] = coef_re * bim + coef_im * bre


def _disc_call(lam_re, lam_im, log_dt, b_re, b_im):
    n = DEPTH * 2 * SSM_GROUPS
    lre = lam_re.reshape(n, 1, SSM_STATE)
    lim = lam_im.reshape(n, 1, SSM_STATE)
    ldt = log_dt.reshape(n, 1, 1)
    bre = jnp.swapaxes(b_re.reshape(n, SSM_STATE, SSM_GROUP_CH), 1, 2)
    bim = jnp.swapaxes(b_im.reshape(n, SSM_STATE, SSM_GROUP_CH), 1, 2)
    small = jax.ShapeDtypeStruct((n, 1, SSM_STATE), F32)
    big = jax.ShapeDtypeStruct((n, SSM_GROUP_CH, SSM_STATE), F32)
    return pl.pallas_call(
        _disc_kernel, out_shape=(small, small, big, big), name="ssm_disc",
    )(lre, lim, ldt, bre, bim)


def _ssm_weights(a_re, a_im, bb_re, bb_im, c_re, c_im):
    def bb_tiles(bb):
        return bb.reshape(DEPTH, 2, 2, 4, 4, SSM_GROUP_CH, SSM_STATE)
    bb = jnp.stack([bb_tiles(bb_re), bb_tiles(bb_im)], axis=2)
    sel = np.zeros((4, 8, 4), np.float32)
    for m in range(4):
        for j in range(4):
            sel[m, 4 * (m % 2) + j, j] = 1.0
    bbt = jnp.einsum('ldrhmjcp,mkj->ldrmhkcjp', bb, jnp.asarray(sel))
    bbt = bbt.reshape(DEPTH, 2, 8, 2 * LANES, 2 * LANES).astype(BF16)
    def c_tiles(c):
        return c.reshape(DEPTH, 2, 2, 2, 8, SSM_GROUP_CH, SSM_STATE)
    cc = jnp.stack([c_tiles(c_re), -c_tiles(c_im)], axis=2)
    eye = jnp.eye(8, dtype=F32)
    cct = jnp.einsum('ldrhmkop,kj->ldmrkphjo', cc, eye)
    cct = cct.reshape(DEPTH, 2, 2, 2 * 8 * SSM_STATE, 2 * LANES).astype(BF16)
    def a_rows(a):
        a = a.reshape(DEPTH, 2, 2, HALF_STATE)
        return jnp.tile(a, (1, 1, 4, 1))
    a8 = jnp.stack([a_rows(a_re), a_rows(a_im)], axis=2)
    return bbt, cct, a8


def _inproj_kernel(x_ref, mod_ref, g_ref, w_ref, z_ref):
    h = _mod_norm(x_ref[...], g_ref[...], mod_ref[:, 0:D_MODEL], mod_ref[:, D_MODEL:2 * D_MODEL])
    h = h.astype(BF16)
    for c, src in enumerate(Z_SRC):
        z_ref[:, c * Z_TILE:(c + 1) * Z_TILE] = jnp.dot(
            h, w_ref[:, src:src + Z_TILE], preferred_element_type=F32).astype(BF16)


def _inproj_call(x, mod, g1, w_in, *, layer, seq, tm):
    t = x.shape[0]
    per = max(seq // tm, 1)
    mod_map = (lambda i: (0, 0, 0)) if tm >= seq else (lambda i: (i // per, 0, 0))
    return pl.pallas_call(
        _inproj_kernel,
        grid=(t // tm,),
        in_specs=[pl.BlockSpec((tm, D_MODEL), lambda i: (i, 0)),
                  pl.BlockSpec((None, 1, 6 * D_MODEL), mod_map),
                  pl.BlockSpec((1, D_MODEL), lambda i: (0, 0)),
                  pl.BlockSpec((None, D_MODEL, IN_COLS), lambda i: (layer, 0, 0),
                               pipeline_mode=pl.Buffered(1))],
        out_specs=pl.BlockSpec((tm, IN_COLS), lambda i: (i, 0)),
        out_shape=jax.ShapeDtypeStruct((t, IN_COLS), BF16),
        compiler_params=_cparams(("arbitrary",)),
        name="in_proj",
    )(x, mod, g1, w_in)


def _head_ssq(x, bd):
    parts = []
    for t in range(x.shape[1] // (2 * LANES)):
        xs = x[:, 2 * LANES * t:2 * LANES * (t + 1)]
        parts.append(jnp.dot((xs * xs).astype(BF16), bd, preferred_element_type=F32))
    return parts[0] if len(parts) == 1 else jnp.concatenate(parts, axis=1)


def _rope(x, cos, sin_up, sin_dn):
    w = x.shape[1]
    rep = w // LANES
    cos = jnp.tile(cos, (1, rep))
    sin_up = jnp.tile(sin_up, (1, rep))
    sin_dn = jnp.tile(sin_dn, (1, rep))
    return x * cos + pltpu.roll(x, w - 16, 1) * sin_up + pltpu.roll(x, 16, 1) * sin_dn


def _dup_heads(x, lane_lo):
    xr = pltpu.roll(x, HEAD_DIM, 1)
    return jnp.where(lane_lo, x, xr), jnp.where(lane_lo, xr, x)


def _attn_kernel(*refs, seq, past, rope, emit_kv, tok, kchunk, wave_units, barrier):
    it = iter(refs)
    q_ref, kv_ref = next(it), next(it)
    kc_ref = vc_ref = cos_ref = sup_ref = sdn_ref = cosk_ref = supk_ref = sdnk_ref = None
    if past:
        kc_ref, vc_ref = next(it), next(it)
    qg_ref, kg_ref, bd_ref = next(it), next(it), next(it)
    if rope:
        cos_ref, sup_ref, sdn_ref = next(it), next(it), next(it)
        cosk_ref, supk_ref, sdnk_ref = next(it), next(it), next(it)
    o_ref = next(it)
    kout_ref = vout_ref = None
    if emit_kv:
        kout_ref, vout_ref = next(it), next(it)
    k2_scr, vt_scr, qst_scr, s_scr, m_scr, p_scr, ot_scr, o_scr = (next(it) for _ in range(8))

    qi = pl.program_id(1)
    bd = bd_ref[...]

    @pl.when(qi == 0)
    def _prep():
        kv = kv_ref[...].astype(F32)
        k = kv[:, :2 * LANES]
        v = kv[:, 2 * LANES:]
        kn = k * lax.rsqrt(_head_ssq(k, bd) * (1.0 / HEAD_DIM) + RMS_EPS) * kg_ref[...]
        if emit_kv:
            kout_ref[...] = kn
            vout_ref[...] = v
        if rope:
            kn = _rope(kn, cosk_ref[...], supk_ref[...], sdnk_ref[...])
        lane_lo = lax.broadcasted_iota(jnp.int32, (seq, LANES), 1) < HEAD_DIM
        for t in range(2):
            ka, kb = _dup_heads(kn[:, LANES * t:LANES * (t + 1)], lane_lo)
            k2_scr[2 * t, 0:seq, :] = ka.astype(BF16)
            k2_scr[2 * t + 1, 0:seq, :] = kb.astype(BF16)
        vt = v.T.astype(BF16)
        for kvh in range(N_KV_HEADS):
            vt_scr[kvh * VT_ROWS:kvh * VT_ROWS + HEAD_DIM, 0:seq] = vt[kvh * HEAD_DIM:(kvh + 1) * HEAD_DIM]
            vt_scr[kvh * VT_ROWS + HEAD_DIM:(kvh + 1) * VT_ROWS, :] = jnp.ones(
                (VT_ROWS - HEAD_DIM, vt_scr.shape[1]), BF16)
        if past:
            kc = kc_ref[...]
            lane_lo_p = lax.broadcasted_iota(jnp.int32, (past, LANES), 1) < HEAD_DIM
            for t in range(2):
                ka, kb = _dup_heads(kc[:, LANES * t:LANES * (t + 1)], lane_lo_p)
                k2_scr[2 * t, seq:seq + past, :] = ka.astype(BF16)
                k2_scr[2 * t + 1, seq:seq + past, :] = kb.astype(BF16)
            vct = vc_ref[...].T.astype(BF16)
            for kvh in range(N_KV_HEADS):
                vt_scr[kvh * VT_ROWS:kvh * VT_ROWS + HEAD_DIM, seq:seq + past] = (
                    vct[kvh * HEAD_DIM:(kvh + 1) * HEAD_DIM])

    q = q_ref[...].astype(F32)
    tq = q.shape[0]
    qn = q * lax.rsqrt(_head_ssq(q, bd) * (1.0 / HEAD_DIM) + RMS_EPS) * qg_ref[...]
    if rope:
        qn = _rope(qn, cos_ref[...], sup_ref[...], sdn_ref[...])
    qn = qn * (HEAD_DIM ** -0.5 * math.log2(math.e))

    n_tb = tq // tok
    n_units = N_KV_HEADS * n_tb
    keys = k2_scr.shape[1]
    n_chunks = keys // kchunk
    lane_lo = lax.broadcasted_iota(jnp.int32, (tok, LANES), 1) < HEAD_DIM
    for kvh in range(N_KV_HEADS):
        for tb in range(n_tb):
            stack = []
            for t in (2 * kvh, 2 * kvh + 1):
                qt = qn[tb * tok:(tb + 1) * tok, LANES * t:LANES * (t + 1)]
                stack += [jnp.where(lane_lo, qt, 0.0), jnp.where(lane_lo, 0.0, qt)]
            qst_scr[kvh * n_tb + tb] = jnp.concatenate(stack, axis=0).astype(BF16)

    def kv_of(u):
        return u // n_tb if isinstance(u, int) else lax.shift_right_logical(u, n_tb.bit_length() - 1)

    def scores(u, slot):
        s = lax.dot_general(k2_scr[kv_of(u)], qst_scr[u], (((1,), (1,)), ((), ())),
                            preferred_element_type=F32)
        s_scr[slot] = s
        m_run = s[0:kchunk]
        for c in range(1, n_chunks):
            m_run = jnp.maximum(m_run, s[c * kchunk:(c + 1) * kchunk])
        m_scr[slot] = jnp.broadcast_to(jnp.max(m_run, axis=0, keepdims=True), m_scr.shape[1:])

    def softmax(u, slot):
        del u
        m = jnp.broadcast_to(m_scr[slot, 0:1, :], (kchunk, 4 * tok))
        for c in range(n_chunks):
            p = jnp.exp2(s_scr[slot, c * kchunk:(c + 1) * kchunk, :] - m)
            p_scr[slot, c * kchunk:(c + 1) * kchunk, :] = p.astype(BF16)

    def values(u, slot):
        r0 = kv_of(u) * VT_ROWS
        v_rows = (slice(r0, r0 + VT_ROWS) if isinstance(r0, int)
                  else pl.ds(pl.multiple_of(r0, VT_ROWS), VT_ROWS))
        ot_scr[slot] = jnp.dot(vt_scr[v_rows, :], p_scr[slot], preferred_element_type=F32)

    def finish(u, slot):
        ot = ot_scr[slot]
        ot = ot[0:HEAD_DIM] * (1.0 / ot[HEAD_DIM:HEAD_DIM + 1])
        for pair in range(2):
            if tok % LANES == 0:
                w = jnp.concatenate([ot[:, 2 * pair * tok:(2 * pair + 1) * tok],
                                     ot[:, (2 * pair + 1) * tok:(2 * pair + 2) * tok]], axis=0)
            else:
                both = ot[:, LANES * pair:LANES * (pair + 1)]
                w = jnp.concatenate([both, pltpu.roll(both, tok, 1)], axis=0)
            o_scr[u, pair] = w.T[0:tok].astype(BF16)

    n_waves = n_units // wave_units
    assert n_tb & (n_tb - 1) == 0 and n_units % wave_units == 0
    assert n_waves >= 4 and (n_waves - 1) % 3 == 0

    def wave(g, slot_set, stage):
        for w in range(wave_units):
            stage(g * wave_units + w, slot_set * wave_units + w)

    def step_barrier():
        if barrier:
            pl.delay(1)

    wave(0, 0, scores)
    wave(0, 0, softmax)
    wave(1, 1, scores)
    for w in range(wave_units):
        ot_scr[2 * wave_units + w] = jnp.ones(ot_scr.shape[1:], F32)

    def pipelined(i, carry):
        g = 3 * i + 1
        for k in range(3):
            cur = (1 + k) % 3
            step_barrier()
            wave(jnp.minimum(g + k + 1, n_waves - 1), (cur + 1) % 3, scores)
            wave(g + k, cur, softmax)
            wave(g + k - 1, (cur + 2) % 3, values)
            wave(jnp.maximum(g + k - 2, 0), (cur + 1) % 3, finish)
        return carry

    lax.fori_loop(0, (n_waves - 1) // 3, pipelined, 0)
    step_barrier()
    wave(n_waves - 1, (n_waves - 1) % 3, values)
    wave(n_waves - 2, (n_waves - 2) % 3, finish)
    step_barrier()
    wave(n_waves - 1, (n_waves - 1) % 3, finish)

    for kvh in range(N_KV_HEADS):
        for tb in range(n_tb):
            u = kvh * n_tb + tb
            for pair, t in enumerate((2 * kvh, 2 * kvh + 1)):
                o_ref[tb * tok:(tb + 1) * tok, LANES * t:LANES * (t + 1)] = o_scr[u, pair]


def _attn_call(z, kc, vc, qg, kg, bd, tabs, *, layer, nb, seq, tq, past, emit_kv, tok, kchunk,
               wave_units, barrier):
    rope = tabs is not None
    nq = seq // tq
    keys = seq + past
    n_units = N_KV_HEADS * (tq // tok)
    q_col = 3072 // D_MODEL
    kv_col = 4096 // 512
    in_specs = [pl.BlockSpec((tq, D_MODEL), lambda b, i: (b * nq + i, q_col)),
                pl.BlockSpec((seq, 512), lambda b, i: (b, kv_col), pipeline_mode=pl.Buffered(1))]
    args = [z, z]
    if past:
        in_specs += [pl.BlockSpec((None, None, past, 2 * LANES), lambda b, i: (b, layer, 0, 0))] * 2
        args += [kc, vc]
    in_specs += [pl.BlockSpec((1, D_MODEL), lambda b, i: (0, 0)),
                 pl.BlockSpec((1, 2 * LANES), lambda b, i: (0, 0)),
                 pl.BlockSpec((2 * LANES, 2 * LANES), lambda b, i: (0, 0))]
    args += [qg, kg, bd]
    if rope:
        in_specs += [pl.BlockSpec((tq, LANES), lambda b, i: (i, 0))] * 3
        in_specs += [pl.BlockSpec((seq, LANES), lambda b, i: (0, 0))] * 3
        args += list(tabs) + list(tabs)
    out_specs = [pl.BlockSpec((tq, D_MODEL), lambda b, i: (b * nq + i, 0))]
    out_shape = [jax.ShapeDtypeStruct((nb * seq, D_MODEL), BF16)]
    if emit_kv:
        out_specs += [pl.BlockSpec((None, seq, 2 * LANES), lambda b, i: (b, 0, 0))] * 2
        out_shape += [jax.ShapeDtypeStruct((nb, seq, 2 * LANES), F32)] * 2
    return pl.pallas_call(
        functools.partial(_attn_kernel, seq=seq, past=past, rope=rope, emit_kv=emit_kv,
                          tok=tok, kchunk=kchunk, wave_units=wave_units, barrier=barrier),
        grid=(nb, nq),
        in_specs=in_specs, out_specs=out_specs, out_shape=out_shape,
        scratch_shapes=[pltpu.VMEM((N_KV_HEADS, keys, LANES), BF16),
                        pltpu.VMEM((N_KV_HEADS * VT_ROWS, keys), BF16),
                        pltpu.VMEM((n_units, 4 * tok, LANES), BF16),
                        pltpu.VMEM((3 * wave_units, keys, 4 * tok), F32),
                        pltpu.VMEM((3 * wave_units, SUBLANES, 4 * tok), F32),
                        pltpu.VMEM((3 * wave_units, keys, 4 * tok), BF16),
                        pltpu.VMEM((3 * wave_units, VT_ROWS, 4 * tok), F32),
                        pltpu.VMEM((n_units, 2, tok, LANES), BF16)],
        compiler_params=_cparams(("arbitrary", "arbitrary")),
        name="attention",
    )(*args)


def _rope_tables(length):
    f = HEAD_DIM // 4
    inv = ROPE_THETA ** (-np.arange(f, dtype=np.float32) / f)
    pos = np.arange(length)
    row = (pos // GRID_W).astype(np.float32)
    col = (pos % GRID_W).astype(np.float32)
    ang_r = row[:, None] * inv[None, :]
    ang_c = col[:, None] * inv[None, :]
    z = np.zeros_like(ang_r)
    cos64 = np.concatenate([np.cos(ang_r), np.cos(ang_r), np.cos(ang_c), np.cos(ang_c)], axis=1)
    up64 = np.concatenate([-np.sin(ang_r), z, -np.sin(ang_c), z], axis=1)
    dn64 = np.concatenate([z, np.sin(ang_r), z, np.sin(ang_c)], axis=1)
    tile2 = lambda a: jnp.asarray(np.tile(a, (1, 2)), dtype=F32)
    return tile2(cos64), tile2(up64), tile2(dn64)


def _ssm_kernel(uf_ref, ub_ref, pin_ref, pout_ref, bbt_ref, cct_ref, a_ref, h0_ref,
                yf_ref, yb_ref, ht_ref, x_scr, hc_scr, *, rows_per_step, steps):
    j = pl.program_id(1)
    r = rows_per_step
    rows = r * steps
    nbg = r // 2
    tok = nbg * steps

    @pl.when(j == 0)
    def _():
        hc_scr[...] = h0_ref[...]

    first_half = (lax.broadcasted_iota(jnp.int32, (rows, LANES), 0) & 1) == 0
    width = SUBLANES * HALF_STATE // r
    for d, (u_ref, y_ref) in enumerate(((uf_ref, yf_ref), (ub_ref, yb_ref))):
        u_bt = u_ref[...].reshape(tok, SSM_W)
        u = jnp.dot(pin_ref[...], u_bt, preferred_element_type=F32)
        kcat = []
        for q in range(2):
            u0 = u[:, LANES * q:LANES * (q + 1)]
            u1 = u[:, 2 * LANES + LANES * q:2 * LANES + LANES * (q + 1)]
            kcat.append(jnp.concatenate([jnp.where(first_half, u0, 0.0),
                                         jnp.where(first_half, 0.0, u1)], axis=1).astype(BF16))
        for n in range(8):
            x_scr[d, :, 2 * LANES * n:2 * LANES * (n + 1)] = jnp.dot(
                kcat[(n % 4) // 2], bbt_ref[d, n], preferred_element_type=F32)

        for p in range(HALF_STATE // width):
            re_cols = slice(p * width, (p + 1) * width)
            im_cols = slice(HALF_STATE + p * width, HALF_STATE + (p + 1) * width)
            reps = r // SUBLANES
            a_re = jnp.tile(a_ref[d, 0, :, re_cols], (reps, 1))
            a_im = jnp.tile(a_ref[d, 1, :, re_cols], (reps, 1))
            h_re = hc_scr[d, :, re_cols]
            h_im = hc_scr[d, :, im_cols]
            for kk in range(steps):
                k = kk if d == 0 else steps - 1 - kk
                srows = slice(k * r, (k + 1) * r)
                n_re = a_re * h_re - a_im * h_im + x_scr[d, srows, re_cols]
                n_im = a_re * h_im + a_im * h_re + x_scr[d, srows, im_cols]
                x_scr[d, srows, re_cols] = n_re
                x_scr[d, srows, im_cols] = n_im
                h_re, h_im = n_re, n_im
            hc_scr[d, :, re_cols] = h_re
            hc_scr[d, :, im_cols] = h_im

        yy = []
        for m in range(2):
            hk = jnp.concatenate(
                [x_scr[d, :, 512 * m:512 * (m + 1)],
                 x_scr[d, :, HALF_STATE + 512 * m:HALF_STATE + 512 * (m + 1)]], axis=1).astype(BF16)
            yy.append(jnp.dot(hk, cct_ref[d, m], preferred_element_type=F32))
        for hh in range(2):
            ysel = jnp.concatenate([yy[0][:, LANES * hh:LANES * (hh + 1)],
                                    yy[1][:, LANES * hh:LANES * (hh + 1)]], axis=1).astype(BF16)
            y_bt = jnp.dot(pout_ref[hh], ysel, preferred_element_type=F32)
            y_ref[:, :, 2 * LANES * hh:2 * LANES * (hh + 1)] = (
                y_bt.reshape(nbg, steps, 2 * LANES).astype(BF16))

    @pl.when(j == pl.num_programs(1) - 1)
    def _():
        ht_ref[...] = hc_scr[...]


def _ssm_perms(nbg, steps):
    tok = nbg * steps
    pin = np.zeros((2 * tok, tok), np.float32)
    for t in range(steps):
        for b in range(nbg):
            for h in range(2):
                pin[(t * nbg + b) * 2 + h, b * steps + t] = 1.0
    pout = np.zeros((2, tok, 2 * tok), np.float32)
    for h in range(2):
        pout[h] = (pin * (np.arange(2 * tok)[:, None] % 2 == h)).T
    return jnp.asarray(pin, dtype=BF16), jnp.asarray(pout, dtype=BF16)


def _ssm_call(z, bbt, cct, a8, h0, *, nb, seq, nbg):
    r = 2 * nbg
    steps = SSM_ROWS // r
    n = seq // steps
    ngrp = nb // nbg
    z3 = z.reshape(nb, seq, IN_COLS)
    pin, pout = _ssm_perms(nbg, steps)
    ys = jax.ShapeDtypeStruct((nb, seq, SSM_W), BF16)
    ublk = (nbg, steps, SSM_W)
    ucol = 6144 // SSM_W
    hblk = (2, r, 2 * HALF_STATE)
    c4 = lambda g, j: (0, 0, 0, 0)
    return pl.pallas_call(
        functools.partial(_ssm_kernel, rows_per_step=r, steps=steps),
        grid=(ngrp, n),
        in_specs=[pl.BlockSpec(ublk, lambda g, j: (g, j, ucol)),
                  pl.BlockSpec(ublk, lambda g, j: (g, n - 1 - j, ucol)),
                  pl.BlockSpec(pin.shape, lambda g, j: (0, 0)),
                  pl.BlockSpec(pout.shape, lambda g, j: (0, 0, 0)),
                  pl.BlockSpec(bbt.shape, c4),
                  pl.BlockSpec(cct.shape, c4),
                  pl.BlockSpec(a8.shape, c4),
                  pl.BlockSpec(hblk, lambda g, j: (0, g, 0))],
        out_specs=[pl.BlockSpec(ublk, lambda g, j: (g, j, 0)),
                   pl.BlockSpec(ublk, lambda g, j: (g, n - 1 - j, 0)),
                   pl.BlockSpec(hblk, lambda g, j: (0, g, 0))],
        out_shape=[ys, ys, jax.ShapeDtypeStruct(h0.shape, F32)],
        scratch_shapes=[pltpu.VMEM((2, SSM_ROWS, 2 * HALF_STATE), F32),
                        pltpu.VMEM(hblk, F32)],
        compiler_params=_cparams(("arbitrary", "arbitrary")),
        name="ssm_scan",
    )(z3, z3, pin, pout, bbt, cct, a8, h0)


def _merge_kernel(x_ref, mod_ref, zg_ref, zc_ref, zu_ref, cprev_ref, cnext_ref,
                  ya_ref, yf_ref, yb_ref, cw_ref, cb_ref, sd_ref, wglu_ref, bglu_ref,
                  wpc_ref, wpa_ref, wps_ref, wo_ref, o_ref, *, seq):
    i = pl.program_id(0)
    tm = x_ref.shape[0]
    zc = zc_ref[...].astype(F32)
    cb = zc[:, 0:CONV_W]
    prod = zc[:, CONV_W:2 * CONV_W] * zc[:, 2 * CONV_W:3 * CONV_W]
    cp = cprev_ref[...].astype(F32)
    cn = cnext_ref[...].astype(F32)
    halo_prev = cp[7:8, CONV_W:2 * CONV_W] * cp[7:8, 2 * CONV_W:3 * CONV_W]
    halo_next = cn[0:1, CONV_W:2 * CONV_W] * cn[0:1, 2 * CONV_W:3 * CONV_W]
    row = lax.broadcasted_iota(jnp.int32, (tm, CONV_W), 0)
    tpos = (i * tm + row) % seq
    prev = jnp.where(row == 0, halo_prev, pltpu.roll(prod, 1, 0))
    prev = jnp.where(tpos == 0, 0.0, prev)
    nxt = jnp.where(row == tm - 1, halo_next, pltpu.roll(prod, tm - 1, 0))
    nxt = jnp.where(tpos == seq - 1, 0.0, nxt)
    cw = cw_ref[...]
    y_conv = cb * (cw[0:1] * prev + cw[1:2] * prod + cw[2:3] * nxt + cb_ref[...])

    u = zu_ref[...].astype(F32)
    y = jax.nn.gelu(yf_ref[...].astype(F32) + yb_ref[...].astype(F32) + sd_ref[...] * u)
    y_ssm = y * jax.nn.sigmoid(
        jnp.dot(y.astype(BF16), wglu_ref[...], preferred_element_type=F32) + bglu_ref[...])

    zg = zg_ref[...].astype(F32)
    merged = (jax.nn.sigmoid(zg[:, 0:D_MODEL])
              * jnp.dot(y_conv.astype(BF16), wpc_ref[...], preferred_element_type=F32)
              + jax.nn.sigmoid(zg[:, D_MODEL:2 * D_MODEL])
              * jnp.dot(ya_ref[...], wpa_ref[...], preferred_element_type=F32)
              + jax.nn.sigmoid(zg[:, 2 * D_MODEL:3 * D_MODEL])
              * jnp.dot(y_ssm.astype(BF16), wps_ref[...], preferred_element_type=F32))
    gate1 = mod_ref[:, 2 * D_MODEL:3 * D_MODEL]
    o_ref[...] = x_ref[...] + gate1 * jnp.dot(merged.astype(BF16), wo_ref[...],
                                              preferred_element_type=F32)


def _merge_call(x, mod, z, y_attn, yf, yb, lp, *, layer, seq, tm):
    t = x.shape[0]
    n_m = t // tm
    per = max(seq // tm, 1)
    hb = tm // SUBLANES
    n_hb = t // SUBLANES
    mod_map = (lambda i: (0, 0, 0)) if tm >= seq else (lambda i: (i // per, 0, 0))
    y_spec = pl.BlockSpec((tm, SSM_W), lambda i: (i, 0))
    yf2 = yf.reshape(t, SSM_W)
    yb2 = yb.reshape(t, SSM_W)
    const2 = lambda i: (0, 0)
    lslab = lambda i: (layer, 0, 0)
    resident = dict(pipeline_mode=pl.Buffered(1))
    return pl.pallas_call(
        functools.partial(_merge_kernel, seq=seq),
        grid=(n_m,),
        in_specs=[pl.BlockSpec((tm, D_MODEL), lambda i: (i, 0)),
                  pl.BlockSpec((None, 1, 6 * D_MODEL), mod_map),
                  pl.BlockSpec((tm, 3 * D_MODEL), lambda i: (i, 0)),
                  pl.BlockSpec((tm, 3 * CONV_W), lambda i: (i, 3)),
                  pl.BlockSpec((tm, SSM_W), lambda i: (i, 6144 // SSM_W)),
                  pl.BlockSpec((SUBLANES, 3 * CONV_W), lambda i: (jnp.maximum(i * hb - 1, 0), 3)),
                  pl.BlockSpec((SUBLANES, 3 * CONV_W),
                               lambda i: (jnp.minimum((i + 1) * hb, n_hb - 1), 3)),
                  pl.BlockSpec((tm, D_MODEL), lambda i: (i, 0)),
                  y_spec, y_spec,
                  pl.BlockSpec((3, CONV_W), const2),
                  pl.BlockSpec((1, CONV_W), const2),
                  pl.BlockSpec((1, SSM_W), const2),
                  pl.BlockSpec((None, SSM_W, SSM_W), lslab, **resident),
                  pl.BlockSpec((1, SSM_W), const2),
                  pl.BlockSpec((None, CONV_W, D_MODEL), lslab, **resident),
                  pl.BlockSpec((None, D_MODEL, D_MODEL), lslab, **resident),
                  pl.BlockSpec((None, SSM_W, D_MODEL), lslab, **resident),
                  pl.BlockSpec((None, D_MODEL, D_MODEL), lslab, **resident)],
        out_specs=pl.BlockSpec((tm, D_MODEL), lambda i: (i, 0)),
        out_shape=jax.ShapeDtypeStruct((t, D_MODEL), F32),
        compiler_params=_cparams(("arbitrary",)),
        name="merge",
    )(x, mod, z, z, z, z, z, y_attn, yf2, yb2,
      lp['conv_w'], lp['conv_b'], lp['ssm_d'], lp['w_glu'], lp['b_glu'],
      lp['w_proj_conv'], lp['w_proj_attn'], lp['w_proj_ssm'], lp['w_out'])


def _ffn_kernel(x_ref, mod_ref, g_ref, wg_ref, wu_ref, wd_ref, fg_ref, o_ref, *, final, tf):
    x = x_ref[...]
    h = _mod_norm(x, g_ref[...], mod_ref[:, 3 * D_MODEL:4 * D_MODEL],
                  mod_ref[:, 4 * D_MODEL:5 * D_MODEL]).astype(BF16)
    acc = None
    for c in range(D_FF // tf):
        cols = slice(c * tf, (c + 1) * tf)
        g = jnp.dot(h, wg_ref[:, cols], preferred_element_type=F32)
        u = jnp.dot(h, wu_ref[:, cols], preferred_element_type=F32)
        part = jnp.dot((jax.nn.silu(g) * u).astype(BF16), wd_ref[cols, :], preferred_element_type=F32)
        acc = part if acc is None else acc + part
    xn = x + mod_ref[:, 5 * D_MODEL:6 * D_MODEL] * acc
    if final:
        ms = jnp.mean(xn * xn, axis=-1, keepdims=True)
        xn = xn * lax.rsqrt(ms + RMS_EPS) * fg_ref[...]
    o_ref[...] = xn


def _ffn_call(x, mod, g2, wg, wu, wd, fg, *, layer, seq, tm, tf, final):
    t = x.shape[0]
    per = max(seq // tm, 1)
    mod_map = (lambda i: (0, 0, 0)) if tm >= seq else (lambda i: (i // per, 0, 0))
    lslab = lambda i: (layer, 0, 0)
    resident = dict(pipeline_mode=pl.Buffered(1))
    return pl.pallas_call(
        functools.partial(_ffn_kernel, final=final, tf=tf),
        grid=(t // tm,),
        in_specs=[pl.BlockSpec((tm, D_MODEL), lambda i: (i, 0)),
                  pl.BlockSpec((None, 1, 6 * D_MODEL), mod_map),
                  pl.BlockSpec((1, D_MODEL), lambda i: (0, 0)),
                  pl.BlockSpec((None, D_MODEL, D_FF), lslab, **resident),
                  pl.BlockSpec((None, D_MODEL, D_FF), lslab, **resident),
                  pl.BlockSpec((None, D_FF, D_MODEL), lslab, **resident),
                  pl.BlockSpec((1, D_MODEL), lambda i: (0, 0))],
        out_specs=pl.BlockSpec((tm, D_MODEL), lambda i: (i, 0)),
        out_shape=jax.ShapeDtypeStruct((t, D_MODEL), F32),
        compiler_params=_cparams(("arbitrary",)),
        name="ffn",
    )(x, mod, g2, wg, wu, wd, fg)


def _state_rows(s):
    b = s.shape[0]
    return jnp.transpose(s, (1, 0, 2, 3)).reshape(2, 2 * b, HALF_STATE)


def _state_unrows(s, b):
    return jnp.transpose(s.reshape(2, b, SSM_GROUPS, SSM_STATE), (1, 0, 2, 3))


def kernel(x_prompt, x_sample, c, cache_k, cache_v, state_ssm_re, state_ssm_im, c_ctx, w_ada, b_ada, norm1_g, w_in, conv_w, conv_b, q_norm_g, k_norm_g, ssm_lambda_re, ssm_lambda_im, ssm_b_re, ssm_b_im, ssm_c_re, ssm_c_im, ssm_log_dt, ssm_d, w_glu, b_glu, w_proj_conv, w_proj_attn, w_proj_ssm, w_out, norm2_g, w_ffn_gate, w_ffn_up, w_ffn_down, final_norm_g):
    nb_c, seq_c, _ = x_prompt.shape
    nb_l, seq_l, _ = x_sample.shape
    past = cache_k.shape[2]

    cvec = jnp.zeros((8, D_MODEL), F32).at[0].set(c_ctx).at[1:1 + nb_l].set(c)
    mod_all = _ada_call(cvec, w_ada, b_ada)

    a_re, a_im, bb_re, bb_im = _disc_call(ssm_lambda_re, ssm_lambda_im, ssm_log_dt, ssm_b_re, ssm_b_im)
    bbt, cct, a8 = _ssm_weights(a_re, a_im, bb_re, bb_im, ssm_c_re, ssm_c_im)

    w_in_p = w_in.astype(BF16)
    wb = {k: v.astype(BF16) for k, v in dict(
        w_glu=w_glu, w_proj_conv=w_proj_conv, w_proj_attn=w_proj_attn, w_proj_ssm=w_proj_ssm,
        w_out=w_out, w_ffn_gate=w_ffn_gate, w_ffn_up=w_ffn_up, w_ffn_down=w_ffn_down).items()}

    bd = jnp.asarray(np.kron(np.eye(2 * LANES // HEAD_DIM, dtype=np.float32),
                             np.ones((HEAD_DIM, HEAD_DIM), np.float32)), dtype=BF16)
    tabs = _rope_tables(seq_l)
    fg = final_norm_g.reshape(1, D_MODEL)

    xc = x_prompt.reshape(nb_c * seq_c, D_MODEL)
    xl = x_sample.reshape(nb_l * seq_l, D_MODEL)
    h0_zero = jnp.zeros((2, 2 * nb_c, 2 * HALF_STATE), F32)
    ck = cache_k.reshape(nb_l, DEPTH, past, N_KV_HEADS * HEAD_DIM)
    cv = cache_v.reshape(nb_l, DEPTH, past, N_KV_HEADS * HEAD_DIM)
    ks, vs, srs, sis = [], [], [], []

    for l in range(DEPTH):
        lp = dict(conv_w=conv_w[l], conv_b=conv_b[l].reshape(1, CONV_W),
                  ssm_d=ssm_d[l].reshape(1, SSM_W), b_glu=b_glu[l].reshape(1, SSM_W),
                  w_glu=wb['w_glu'], w_proj_conv=wb['w_proj_conv'],
                  w_proj_attn=wb['w_proj_attn'], w_proj_ssm=wb['w_proj_ssm'],
                  w_out=wb['w_out'])
        g1 = norm1_g[l].reshape(1, D_MODEL)
        g2 = norm2_g[l].reshape(1, D_MODEL)
        qg = jnp.tile(q_norm_g[l], N_HEADS).reshape(1, D_MODEL)
        kg = jnp.tile(k_norm_g[l], N_KV_HEADS).reshape(1, 2 * LANES)
        mod_c = mod_all[l, 0:1].reshape(1, 1, 6 * D_MODEL)
        mod_l = mod_all[l, 1:1 + nb_l].reshape(nb_l, 1, 6 * D_MODEL)
        final = l == DEPTH - 1

        z = _inproj_call(xc, mod_c, g1, w_in_p, layer=l, seq=seq_c, tm=512)
        y_attn, k_l, v_l = _attn_call(z, None, None, qg, kg, bd, None, layer=l, nb=nb_c, seq=seq_c,
                                      tq=seq_c, past=0, emit_kv=True,
                                      tok=seq_c, kchunk=SUBLANES, wave_units=1, barrier=False)
        yf, yb, ht = _ssm_call(z, bbt[l], cct[l], a8[l], h0_zero, nb=nb_c, seq=seq_c, nbg=16)
        xc = _merge_call(xc, mod_c, z, y_attn, yf, yb, lp, layer=l, seq=seq_c, tm=512)
        xc = _ffn_call(xc, mod_c, g2, wb['w_ffn_gate'], wb['w_ffn_up'], wb['w_ffn_down'],
                       fg, layer=l, seq=seq_c, tm=512, tf=1408, final=final)
        ks.append(k_l)
        vs.append(v_l)
        srs.append(_state_unrows(ht[:, :, :HALF_STATE], nb_c))
        sis.append(_state_unrows(ht[:, :, HALF_STATE:], nb_c))

        h0 = jnp.concatenate([_state_rows(state_ssm_re[:, l]), _state_rows(state_ssm_im[:, l])], axis=2)
        z = _inproj_call(xl, mod_l, g1, w_in_p, layer=l, seq=seq_l, tm=512)
        (y_attn,) = _attn_call(z, ck, cv, qg, kg, bd, tabs,
                               layer=l, nb=nb_l, seq=seq_l, tq=512, past=past, emit_kv=False,
                               tok=64, kchunk=32, wave_units=2, barrier=True)
        yf, yb, _ = _ssm_call(z, bbt[l], cct[l], a8[l], h0, nb=nb_l, seq=seq_l, nbg=nb_l)
        xl = _merge_call(xl, mod_l, z, y_attn, yf, yb, lp, layer=l, seq=seq_l, tm=512)
        xl = _ffn_call(xl, mod_l, g2, wb['w_ffn_gate'], wb['w_ffn_up'], wb['w_ffn_down'],
                       fg, layer=l, seq=seq_l, tm=512, tf=1408, final=final)

    y_prompt = xc.reshape(nb_c, seq_c, D_MODEL)
    y_sample = xl.reshape(nb_l, seq_l, D_MODEL)
    kv_shape = (nb_c, DEPTH, seq_c, N_KV_HEADS, HEAD_DIM)
    return (y_prompt, y_sample, jnp.stack(ks, axis=1).reshape(kv_shape),
            jnp.stack(vs, axis=1).reshape(kv_shape),
            jnp.stack(srs, axis=1), jnp.stack(sis, axis=1))
```

```python
import functools
import math

import jax
import jax.numpy as jnp
import numpy as np
from jax import lax
from jax.experimental import pallas as pl
from jax.experimental.pallas import tpu as pltpu

F32 = jnp.float32
BF16 = jnp.bfloat16

D_MODEL = 1024
DEPTH = 4
GRID_W = 64
HEAD_DIM = 64
N_HEADS = 16
N_KV_HEADS = 4
ROPE_THETA = 10000.0
CONV_W = 512
SSM_W = 512
SSM_GROUP_CH = 16
SSM_GROUPS = 32
SSM_STATE = 64
D_FF = 2816
RMS_EPS = 1e-6
IN_COLS = 6656

LANES = 128
SUBLANES = 8
VMEM_LIMIT = 56 * 1024 * 1024

Z_TILE = 512
Z_SRC = (3584, 4096, 4608, 5120, 5632, 6144, 1536, 2048, 2560, 0, 512, 1024, 3072)
HALF_STATE = SSM_GROUPS // 2 * SSM_STATE
SSM_ROWS = 512
VT_ROWS = HEAD_DIM + 16


def _cparams(sem):
    return pltpu.CompilerParams(dimension_semantics=sem, vmem_limit_bytes=VMEM_LIMIT)


def _mod_norm(x, g, shift, scale):
    ms = jnp.mean(x * x, axis=-1, keepdims=True)
    y = x * lax.rsqrt(ms + RMS_EPS) * g
    return y * (1.0 + scale) + shift


def _ada_kernel(c_ref, w_ref, b_ref, o_ref):
    s = jax.nn.silu(c_ref[...])
    o_ref[...] = jnp.dot(s.astype(BF16), w_ref[...].astype(BF16),
                         preferred_element_type=F32) + b_ref[...]


def _ada_call(cvec, w_ada, b_ada):
    tn = 1536
    return pl.pallas_call(
        _ada_kernel,
        grid=(DEPTH, 6 * D_MODEL // tn),
        in_specs=[pl.BlockSpec((8, D_MODEL), lambda l, j: (0, 0)),
                  pl.BlockSpec((None, D_MODEL, tn), lambda l, j: (l, 0, j)),
                  pl.BlockSpec((None, 1, tn), lambda l, j: (l, 0, j))],
        out_specs=pl.BlockSpec((None, 8, tn), lambda l, j: (l, 0, j)),
        out_shape=jax.ShapeDtypeStruct((DEPTH, 8, 6 * D_MODEL), F32),
        compiler_params=_cparams(("arbitrary", "arbitrary")),
        name="ada_mod",
    )(cvec, w_ada, b_ada.reshape(DEPTH, 1, 6 * D_MODEL))


def _disc_kernel(lre_ref, lim_ref, ldt_ref, bre_ref, bim_ref,
                 are_ref, aim_ref, bbre_ref, bbim_ref):
    lre = lre_ref[...]
    lim = lim_ref[...]
    dt = jnp.exp(ldt_ref[...])
    mag = jnp.exp(lre * dt)
    a_re = mag * jnp.cos(lim * dt)
    a_im = mag * jnp.sin(lim * dt)
    den = lre * lre + lim * lim
    n_re = a_re - 1.0
    coef_re = (n_re * lre + a_im * lim) / den
    coef_im = (a_im * lre - n_re * lim) / den
    are_ref[...] = a_re
    aim_ref[...] = a_im
    bre = bre_ref[...]
    bim = bim_ref[...]
    bbre_ref[...] = coef_re * bre - coef_im * bim
    bbim_ref[...] = coef_re * bim + coef_im * bre


def _disc_call(lam_re, lam_im, log_dt, b_re, b_im):
    n = DEPTH * 2 * SSM_GROUPS
    lre = lam_re.reshape(n, 1, SSM_STATE)
    lim = lam_im.reshape(n, 1, SSM_STATE)
    ldt = log_dt.reshape(n, 1, 1)
    bre = jnp.swapaxes(b_re.reshape(n, SSM_STATE, SSM_GROUP_CH), 1, 2)
    bim = jnp.swapaxes(b_im.reshape(n, SSM_STATE, SSM_GROUP_CH), 1, 2)
    small = jax.ShapeDtypeStruct((n, 1, SSM_STATE), F32)
    big = jax.ShapeDtypeStruct((n, SSM_GROUP_CH, SSM_STATE), F32)
    return pl.pallas_call(
        _disc_kernel, out_shape=(small, small, big, big), name="ssm_disc",
    )(lre, lim, ldt, bre, bim)


def _ssm_weights(a_re, a_im, bb_re, bb_im, c_re, c_im):
    def bb_tiles(bb):
        return bb.reshape(DEPTH, 2, 2, 4, 4, SSM_GROUP_CH, SSM_STATE)
    bb = jnp.stack([bb_tiles(bb_re), bb_tiles(bb_im)], axis=2)
    sel = np.zeros((4, 8, 4), np.float32)
    for m in range(4):
        for j in range(4):
            sel[m, 4 * (m % 2) + j, j] = 1.0
    bbt = jnp.einsum('ldrhmjcp,mkj->ldrmhkcjp', bb, jnp.asarray(sel))
    bbt = bbt.reshape(DEPTH, 2, 8, 2 * LANES, 2 * LANES).astype(BF16)
    def c_tiles(c):
        return c.reshape(DEPTH, 2, 2, 2, 8, SSM_GROUP_CH, SSM_STATE)
    cc = jnp.stack([c_tiles(c_re), -c_tiles(c_im)], axis=2)
    eye = jnp.eye(8, dtype=F32)
    cct = jnp.einsum('ldrhmkop,kj->ldmrkphjo', cc, eye)
    cct = cct.reshape(DEPTH, 2, 2, 2 * 8 * SSM_STATE, 2 * LANES).astype(BF16)
    def a_rows(a):
        a = a.reshape(DEPTH, 2, 2, HALF_STATE)
        return jnp.tile(a, (1, 1, 4, 1))
    a8 = jnp.stack([a_rows(a_re), a_rows(a_im)], axis=2)
    return bbt, cct, a8


def _inproj_kernel(x_ref, mod_ref, g_ref, w_ref, z_ref):
    h = _mod_norm(x_ref[...], g_ref[...], mod_ref[:, 0:D_MODEL], mod_ref[:, D_MODEL:2 * D_MODEL])
    h = h.astype(BF16)
    for c, src in enumerate(Z_SRC):
        z_ref[:, c * Z_TILE:(c + 1) * Z_TILE] = jnp.dot(
            h, w_ref[:, src:src + Z_TILE], preferred_element_type=F32).astype(BF16)


def _inproj_call(x, mod, g1, w_in, *, layer, seq, tm):
    t = x.shape[0]
    per = max(seq // tm, 1)
    mod_map = (lambda i: (0, 0, 0)) if tm >= seq else (lambda i: (i // per, 0, 0))
    return pl.pallas_call(
        _inproj_kernel,
        grid=(t // tm,),
        in_specs=[pl.BlockSpec((tm, D_MODEL), lambda i: (i, 0)),
                  pl.BlockSpec((None, 1, 6 * D_MODEL), mod_map),
                  pl.BlockSpec((1, D_MODEL), lambda i: (0, 0)),
                  pl.BlockSpec((None, D_MODEL, IN_COLS), lambda i: (layer, 0, 0),
                               pipeline_mode=pl.Buffered(1))],
        out_specs=pl.BlockSpec((tm, IN_COLS), lambda i: (i, 0)),
        out_shape=jax.ShapeDtypeStruct((t, IN_COLS), BF16),
        compiler_params=_cparams(("arbitrary",)),
        name="in_proj",
    )(x, mod, g1, w_in)


def _head_ssq(x, bd):
    parts = []
    for t in range(x.shape[1] // (2 * LANES)):
        xs = x[:, 2 * LANES * t:2 * LANES * (t + 1)]
        parts.append(jnp.dot((xs * xs).astype(BF16), bd, preferred_element_type=F32))
    return parts[0] if len(parts) == 1 else jnp.concatenate(parts, axis=1)


def _rope(x, cos, sin_up, sin_dn):
    w = x.shape[1]
    rep = w // LANES
    cos = jnp.tile(cos, (1, rep))
    sin_up = jnp.tile(sin_up, (1, rep))
    sin_dn = jnp.tile(sin_dn, (1, rep))
    return x * cos + pltpu.roll(x, w - 16, 1) * sin_up + pltpu.roll(x, 16, 1) * sin_dn


def _dup_heads(x, lane_lo):
    xr = pltpu.roll(x, HEAD_DIM, 1)
    return jnp.where(lane_lo, x, xr), jnp.where(lane_lo, xr, x)


def _attn_kernel(*refs, seq, past, rope, emit_kv, tok, kchunk, wave_units, barrier):
    it = iter(refs)
    q_ref, kv_ref = next(it), next(it)
    kc_ref = vc_ref = cos_ref = sup_ref = sdn_ref = cosk_ref = supk_ref = sdnk_ref = None
    if past:
        kc_ref, vc_ref = next(it), next(it)
    qg_ref, kg_ref, bd_ref = next(it), next(it), next(it)
    if rope:
        cos_ref, sup_ref, sdn_ref = next(it), next(it), next(it)
        cosk_ref, supk_ref, sdnk_ref = next(it), next(it), next(it)
    if emit_kv:
        next(it), next(it)
    o_ref = next(it)
    kout_ref = vout_ref = None
    if emit_kv:
        kout_ref, vout_ref = next(it), next(it)
    k2_scr, vt_scr, qst_scr, s_scr, m_scr, p_scr, ot_scr, o_scr = (next(it) for _ in range(8))

    qi = pl.program_id(1)
    bd = bd_ref[...]

    @pl.when(qi == 0)
    def _prep():
        kv = kv_ref[...].astype(F32)
        k = kv[:, :2 * LANES]
        v = kv[:, 2 * LANES:]
        kn = k * lax.rsqrt(_head_ssq(k, bd) * (1.0 / HEAD_DIM) + RMS_EPS) * kg_ref[...]
        if emit_kv:
            kout_ref[...] = kn
            vout_ref[...] = v
        if rope:
            kn = _rope(kn, cosk_ref[...], supk_ref[...], sdnk_ref[...])
        lane_lo = lax.broadcasted_iota(jnp.int32, (seq, LANES), 1) < HEAD_DIM
        for t in range(2):
            ka, kb = _dup_heads(kn[:, LANES * t:LANES * (t + 1)], lane_lo)
            k2_scr[2 * t, 0:seq, :] = ka.astype(BF16)
            k2_scr[2 * t + 1, 0:seq, :] = kb.astype(BF16)
        vt = v.T.astype(BF16)
        for kvh in range(N_KV_HEADS):
            vt_scr[kvh * VT_ROWS:kvh * VT_ROWS + HEAD_DIM, 0:seq] = vt[kvh * HEAD_DIM:(kvh + 1) * HEAD_DIM]
            vt_scr[kvh * VT_ROWS + HEAD_DIM:(kvh + 1) * VT_ROWS, :] = jnp.ones(
                (VT_ROWS - HEAD_DIM, vt_scr.shape[1]), BF16)
        if past:
            kc = kc_ref[...]
            lane_lo_p = lax.broadcasted_iota(jnp.int32, (past, LANES), 1) < HEAD_DIM
            for t in range(2):
                ka, kb = _dup_heads(kc[:, LANES * t:LANES * (t + 1)], lane_lo_p)
                k2_scr[2 * t, seq:seq + past, :] = ka.astype(BF16)
                k2_scr[2 * t + 1, seq:seq + past, :] = kb.astype(BF16)
            vct = vc_ref[...].T.astype(BF16)
            for kvh in range(N_KV_HEADS):
                vt_scr[kvh * VT_ROWS:kvh * VT_ROWS + HEAD_DIM, seq:seq + past] = (
                    vct[kvh * HEAD_DIM:(kvh + 1) * HEAD_DIM])

    q = q_ref[...].astype(F32)
    tq = q.shape[0]
    qn = q * lax.rsqrt(_head_ssq(q, bd) * (1.0 / HEAD_DIM) + RMS_EPS) * qg_ref[...]
    if rope:
        qn = _rope(qn, cos_ref[...], sup_ref[...], sdn_ref[...])
    qn = qn * (HEAD_DIM ** -0.5 * math.log2(math.e))

    n_tb = tq // tok
    n_units = N_KV_HEADS * n_tb
    keys = k2_scr.shape[1]
    n_chunks = keys // kchunk
    lane_lo = lax.broadcasted_iota(jnp.int32, (tok, LANES), 1) < HEAD_DIM
    for kvh in range(N_KV_HEADS):
        for tb in range(n_tb):
            stack = []
            for t in (2 * kvh, 2 * kvh + 1):
                qt = qn[tb * tok:(tb + 1) * tok, LANES * t:LANES * (t + 1)]
                stack += [jnp.where(lane_lo, qt, 0.0), jnp.where(lane_lo, 0.0, qt)]
            qst_scr[kvh * n_tb + tb] = jnp.concatenate(stack, axis=0).astype(BF16)

    def kv_of(u):
        return u // n_tb if isinstance(u, int) else lax.shift_right_logical(u, n_tb.bit_length() - 1)

    def scores(u, slot):
        s = lax.dot_general(k2_scr[kv_of(u)], qst_scr[u], (((1,), (1,)), ((), ())),
                            preferred_element_type=F32)
        s_scr[slot] = s
        m_run = s[0:kchunk]
        for c in range(1, n_chunks):
            m_run = jnp.maximum(m_run, s[c * kchunk:(c + 1) * kchunk])
        m_scr[slot] = jnp.broadcast_to(jnp.max(m_run, axis=0, keepdims=True), m_scr.shape[1:])

    def softmax(u, slot):
        del u
        m = jnp.broadcast_to(m_scr[slot, 0:1, :], (kchunk, 4 * tok))
        for c in range(n_chunks):
            p = jnp.exp2(s_scr[slot, c * kchunk:(c + 1) * kchunk, :] - m)
            p_scr[slot, c * kchunk:(c + 1) * kchunk, :] = p.astype(BF16)

    def values(u, slot):
        r0 = kv_of(u) * VT_ROWS
        v_rows = (slice(r0, r0 + VT_ROWS) if isinstance(r0, int)
                  else pl.ds(pl.multiple_of(r0, VT_ROWS), VT_ROWS))
        ot_scr[slot] = jnp.dot(vt_scr[v_rows, :], p_scr[slot], preferred_element_type=F32)

    def finish(u, slot):
        ot = ot_scr[slot]
        ot = ot[0:HEAD_DIM] * (1.0 / ot[HEAD_DIM:HEAD_DIM + 1])
        for pair in range(2):
            if tok % LANES == 0:
                w = jnp.concatenate([ot[:, 2 * pair * tok:(2 * pair + 1) * tok],
                                     ot[:, (2 * pair + 1) * tok:(2 * pair + 2) * tok]], axis=0)
            else:
                both = ot[:, LANES * pair:LANES * (pair + 1)]
                w = jnp.concatenate([both, pltpu.roll(both, tok, 1)], axis=0)
            o_scr[u, pair] = w.T[0:tok].astype(BF16)

    n_waves = n_units // wave_units
    assert n_tb & (n_tb - 1) == 0 and n_units % wave_units == 0
    assert n_waves >= 4 and (n_waves - 1) % 3 == 0

    def wave(g, slot_set, stage):
        for w in range(wave_units):
            stage(g * wave_units + w, slot_set * wave_units + w)

    def step_barrier():
        if barrier:
            pl.delay(1)

    wave(0, 0, scores)
    wave(0, 0, softmax)
    wave(1, 1, scores)
    for w in range(wave_units):
        ot_scr[2 * wave_units + w] = jnp.ones(ot_scr.shape[1:], F32)

    def pipelined(i, carry):
        g = 3 * i + 1
        for k in range(3):
            cur = (1 + k) % 3
            step_barrier()
            wave(jnp.minimum(g + k + 1, n_waves - 1), (cur + 1) % 3, scores)
            wave(g + k, cur, softmax)
            wave(g + k - 1, (cur + 2) % 3, values)
            wave(jnp.maximum(g + k - 2, 0), (cur + 1) % 3, finish)
        return carry

    lax.fori_loop(0, (n_waves - 1) // 3, pipelined, 0)
    step_barrier()
    wave(n_waves - 1, (n_waves - 1) % 3, values)
    wave(n_waves - 2, (n_waves - 2) % 3, finish)
    step_barrier()
    wave(n_waves - 1, (n_waves - 1) % 3, finish)

    for kvh in range(N_KV_HEADS):
        for tb in range(n_tb):
            u = kvh * n_tb + tb
            for pair, t in enumerate((2 * kvh, 2 * kvh + 1)):
                o_ref[tb * tok:(tb + 1) * tok, LANES * t:LANES * (t + 1)] = o_scr[u, pair]


def _attn_call(z, kc, vc, qg, kg, bd, tabs, *, layer, nb, seq, tq, past, emit_kv, tok, kchunk,
               wave_units, barrier):
    rope = tabs is not None
    nq = seq // tq
    keys = seq + past
    n_units = N_KV_HEADS * (tq // tok)
    q_col = 3072 // D_MODEL
    kv_col = 4096 // 512
    in_specs = [pl.BlockSpec((tq, D_MODEL), lambda b, i: (b * nq + i, q_col)),
                pl.BlockSpec((seq, 512), lambda b, i: (b, kv_col),
                             pipeline_mode=pl.Buffered(1 if nq > 1 else 2))]
    args = [z, z]
    if past:
        in_specs += [pl.BlockSpec((None, None, past, 2 * LANES), lambda b, i: (b, layer, 0, 0))] * 2
        args += [kc, vc]
    in_specs += [pl.BlockSpec((1, D_MODEL), lambda b, i: (0, 0)),
                 pl.BlockSpec((1, 2 * LANES), lambda b, i: (0, 0)),
                 pl.BlockSpec((2 * LANES, 2 * LANES), lambda b, i: (0, 0))]
    args += [qg, kg, bd]
    if rope:
        in_specs += [pl.BlockSpec((tq, LANES), lambda b, i: (i, 0))] * 3
        in_specs += [pl.BlockSpec((seq, LANES), lambda b, i: (0, 0))] * 3
        args += list(tabs) + list(tabs)
    out_specs = [pl.BlockSpec((tq, D_MODEL), lambda b, i: (b * nq + i, 0))]
    out_shape = [jax.ShapeDtypeStruct((nb * seq, D_MODEL), BF16)]
    aliases = {}
    if emit_kv:
        aliases = {len(args): 1, len(args) + 1: 2}
        in_specs += [pl.BlockSpec(memory_space=pl.ANY)] * 2
        args += list(emit_kv)
        out_specs += [pl.BlockSpec((None, None, seq, 2 * LANES), lambda b, i: (b, layer, 0, 0))] * 2
        out_shape += [jax.ShapeDtypeStruct(a.shape, a.dtype) for a in emit_kv]
    return pl.pallas_call(
        functools.partial(_attn_kernel, seq=seq, past=past, rope=rope, emit_kv=bool(emit_kv),
                          tok=tok, kchunk=kchunk, wave_units=wave_units, barrier=barrier),
        grid=(nb, nq),
        in_specs=in_specs, out_specs=out_specs, out_shape=out_shape, input_output_aliases=aliases,
        scratch_shapes=[pltpu.VMEM((N_KV_HEADS, keys, LANES), BF16),
                        pltpu.VMEM((N_KV_HEADS * VT_ROWS, keys), BF16),
                        pltpu.VMEM((n_units, 4 * tok, LANES), BF16),
                        pltpu.VMEM((3 * wave_units, keys, 4 * tok), F32),
                        pltpu.VMEM((3 * wave_units, SUBLANES, 4 * tok), F32),
                        pltpu.VMEM((3 * wave_units, keys, 4 * tok), BF16),
                        pltpu.VMEM((3 * wave_units, VT_ROWS, 4 * tok), F32),
                        pltpu.VMEM((n_units, 2, tok, LANES), BF16)],
        compiler_params=_cparams(("arbitrary", "arbitrary")),
        name="attention",
    )(*args)


def _rope_tables(length):
    f = HEAD_DIM // 4
    inv = ROPE_THETA ** (-np.arange(f, dtype=np.float32) / f)
    pos = np.arange(length)
    row = (pos // GRID_W).astype(np.float32)
    col = (pos % GRID_W).astype(np.float32)
    ang_r = row[:, None] * inv[None, :]
    ang_c = col[:, None] * inv[None, :]
    z = np.zeros_like(ang_r)
    cos64 = np.concatenate([np.cos(ang_r), np.cos(ang_r), np.cos(ang_c), np.cos(ang_c)], axis=1)
    up64 = np.concatenate([-np.sin(ang_r), z, -np.sin(ang_c), z], axis=1)
    dn64 = np.concatenate([z, np.sin(ang_r), z, np.sin(ang_c)], axis=1)
    tile2 = lambda a: jnp.asarray(np.tile(a, (1, 2)), dtype=F32)
    return tile2(cos64), tile2(up64), tile2(dn64)


def _ssm_kernel(uf_ref, ub_ref, pin_ref, pout_ref, bbt_ref, cct_ref, a_ref, h0_ref,
                yf_ref, yb_ref, ht_ref, x_scr, hc_scr, *, rows_per_step, steps):
    j = pl.program_id(1)
    r = rows_per_step
    rows = r * steps
    nbg = r // 2
    tok = nbg * steps

    @pl.when(j == 0)
    def _():
        hc_scr[...] = h0_ref[...]

    first_half = (lax.broadcasted_iota(jnp.int32, (rows, LANES), 0) & 1) == 0
    width = SUBLANES * HALF_STATE // r
    for d, (u_ref, y_ref) in enumerate(((uf_ref, yf_ref), (ub_ref, yb_ref))):
        u_bt = u_ref[...].reshape(tok, SSM_W)
        u = jnp.dot(pin_ref[...], u_bt, preferred_element_type=F32)
        kcat = []
        for q in range(2):
            u0 = u[:, LANES * q:LANES * (q + 1)]
            u1 = u[:, 2 * LANES + LANES * q:2 * LANES + LANES * (q + 1)]
            kcat.append(jnp.concatenate([jnp.where(first_half, u0, 0.0),
                                         jnp.where(first_half, 0.0, u1)], axis=1).astype(BF16))
        for n in range(8):
            x_scr[d, :, 2 * LANES * n:2 * LANES * (n + 1)] = jnp.dot(
                kcat[(n % 4) // 2], bbt_ref[d, n], preferred_element_type=F32)

        for p in range(HALF_STATE // width):
            re_cols = slice(p * width, (p + 1) * width)
            im_cols = slice(HALF_STATE + p * width, HALF_STATE + (p + 1) * width)
            reps = r // SUBLANES
            a_re = jnp.tile(a_ref[d, 0, :, re_cols], (reps, 1))
            a_im = jnp.tile(a_ref[d, 1, :, re_cols], (reps, 1))
            h_re = hc_scr[d, :, re_cols]
            h_im = hc_scr[d, :, im_cols]
            for kk in range(steps):
                k = kk if d == 0 else steps - 1 - kk
                srows = slice(k * r, (k + 1) * r)
                n_re = a_re * h_re - a_im * h_im + x_scr[d, srows, re_cols]
                n_im = a_re * h_im + a_im * h_re + x_scr[d, srows, im_cols]
                x_scr[d, srows, re_cols] = n_re
                x_scr[d, srows, im_cols] = n_im
                h_re, h_im = n_re, n_im
            hc_scr[d, :, re_cols] = h_re
            hc_scr[d, :, im_cols] = h_im

        yy = []
        for m in range(2):
            hk = jnp.concatenate(
                [x_scr[d, :, 512 * m:512 * (m + 1)],
                 x_scr[d, :, HALF_STATE + 512 * m:HALF_STATE + 512 * (m + 1)]], axis=1).astype(BF16)
            yy.append(jnp.dot(hk, cct_ref[d, m], preferred_element_type=F32))
        for hh in range(2):
            ysel = jnp.concatenate([yy[0][:, LANES * hh:LANES * (hh + 1)],
                                    yy[1][:, LANES * hh:LANES * (hh + 1)]], axis=1).astype(BF16)
            y_bt = jnp.dot(pout_ref[hh], ysel, preferred_element_type=F32)
            y_ref[:, :, 2 * LANES * hh:2 * LANES * (hh + 1)] = (
                y_bt.reshape(nbg, steps, 2 * LANES).astype(BF16))

    @pl.when(j == pl.num_programs(1) - 1)
    def _():
        ht_ref[...] = hc_scr[...]


def _ssm_perms(nbg, steps):
    tok = nbg * steps
    pin = np.zeros((2 * tok, tok), np.float32)
    for t in range(steps):
        for b in range(nbg):
            for h in range(2):
                pin[(t * nbg + b) * 2 + h, b * steps + t] = 1.0
    pout = np.zeros((2, tok, 2 * tok), np.float32)
    for h in range(2):
        pout[h] = (pin * (np.arange(2 * tok)[:, None] % 2 == h)).T
    return jnp.asarray(pin, dtype=BF16), jnp.asarray(pout, dtype=BF16)


def _ssm_call(z, bbt, cct, a8, h0, *, nb, seq, nbg):
    r = 2 * nbg
    steps = SSM_ROWS // r
    n = seq // steps
    ngrp = nb // nbg
    z3 = z.reshape(nb, seq, IN_COLS)
    pin, pout = _ssm_perms(nbg, steps)
    ys = jax.ShapeDtypeStruct((nb, seq, SSM_W), BF16)
    ublk = (nbg, steps, SSM_W)
    ucol = 6144 // SSM_W
    hblk = (2, r, 2 * HALF_STATE)
    c4 = lambda g, j: (0, 0, 0, 0)
    return pl.pallas_call(
        functools.partial(_ssm_kernel, rows_per_step=r, steps=steps),
        grid=(ngrp, n),
        in_specs=[pl.BlockSpec(ublk, lambda g, j: (g, j, ucol)),
                  pl.BlockSpec(ublk, lambda g, j: (g, n - 1 - j, ucol)),
                  pl.BlockSpec(pin.shape, lambda g, j: (0, 0)),
                  pl.BlockSpec(pout.shape, lambda g, j: (0, 0, 0)),
                  pl.BlockSpec(bbt.shape, c4),
                  pl.BlockSpec(cct.shape, c4),
                  pl.BlockSpec(a8.shape, c4),
                  pl.BlockSpec(hblk, lambda g, j: (0, g, 0))],
        out_specs=[pl.BlockSpec(ublk, lambda g, j: (g, j, 0)),
                   pl.BlockSpec(ublk, lambda g, j: (g, n - 1 - j, 0)),
                   pl.BlockSpec(hblk, lambda g, j: (0, g, 0))],
        out_shape=[ys, ys, jax.ShapeDtypeStruct(h0.shape, F32)],
        scratch_shapes=[pltpu.VMEM((2, SSM_ROWS, 2 * HALF_STATE), F32),
                        pltpu.VMEM(hblk, F32)],
        compiler_params=_cparams(("arbitrary", "arbitrary")),
        name="ssm_scan",
    )(z3, z3, pin, pout, bbt, cct, a8, h0)


def _merge_kernel(x_ref, mod_ref, zg_ref, zc_ref, zu_ref, cprev_ref, cnext_ref,
                  ya_ref, yf_ref, yb_ref, cw_ref, cb_ref, sd_ref, wglu_ref, bglu_ref,
                  wpc_ref, wpa_ref, wps_ref, wo_ref, o_ref, *, seq):
    i = pl.program_id(0)
    tm = x_ref.shape[0]
    zc = zc_ref[...].astype(F32)
    cb = zc[:, 0:CONV_W]
    prod = zc[:, CONV_W:2 * CONV_W] * zc[:, 2 * CONV_W:3 * CONV_W]
    cp = cprev_ref[...].astype(F32)
    cn = cnext_ref[...].astype(F32)
    halo_prev = cp[7:8, CONV_W:2 * CONV_W] * cp[7:8, 2 * CONV_W:3 * CONV_W]
    halo_next = cn[0:1, CONV_W:2 * CONV_W] * cn[0:1, 2 * CONV_W:3 * CONV_W]
    row = lax.broadcasted_iota(jnp.int32, (tm, CONV_W), 0)
    tpos = (i * tm + row) % seq
    prev = jnp.where(row == 0, halo_prev, pltpu.roll(prod, 1, 0))
    prev = jnp.where(tpos == 0, 0.0, prev)
    nxt = jnp.where(row == tm - 1, halo_next, pltpu.roll(prod, tm - 1, 0))
    nxt = jnp.where(tpos == seq - 1, 0.0, nxt)
    cw = cw_ref[...]
    y_conv = cb * (cw[0:1] * prev + cw[1:2] * prod + cw[2:3] * nxt + cb_ref[...])

    u = zu_ref[...].astype(F32)
    y = jax.nn.gelu(yf_ref[...].astype(F32) + yb_ref[...].astype(F32) + sd_ref[...] * u)
    y_ssm = y * jax.nn.sigmoid(
        jnp.dot(y.astype(BF16), wglu_ref[...], preferred_element_type=F32) + bglu_ref[...])

    zg = zg_ref[...].astype(F32)
    merged = (jax.nn.sigmoid(zg[:, 0:D_MODEL])
              * jnp.dot(y_conv.astype(BF16), wpc_ref[...], preferred_element_type=F32)
              + jax.nn.sigmoid(zg[:, D_MODEL:2 * D_MODEL])
              * jnp.dot(ya_ref[...], wpa_ref[...], preferred_element_type=F32)
              + jax.nn.sigmoid(zg[:, 2 * D_MODEL:3 * D_MODEL])
              * jnp.dot(y_ssm.astype(BF16), wps_ref[...], preferred_element_type=F32))
    gate1 = mod_ref[:, 2 * D_MODEL:3 * D_MODEL]
    o_ref[...] = x_ref[...] + gate1 * jnp.dot(merged.astype(BF16), wo_ref[...],
                                              preferred_element_type=F32)


def _merge_call(x, mod, z, y_attn, yf, yb, lp, *, layer, seq, tm):
    t = x.shape[0]
    n_m = t // tm
    per = max(seq // tm, 1)
    hb = tm // SUBLANES
    n_hb = t // SUBLANES
    mod_map = (lambda i: (0, 0, 0)) if tm >= seq else (lambda i: (i // per, 0, 0))
    y_spec = pl.BlockSpec((tm, SSM_W), lambda i: (i, 0))
    yf2 = yf.reshape(t, SSM_W)
    yb2 = yb.reshape(t, SSM_W)
    const2 = lambda i: (0, 0)
    lslab = lambda i: (layer, 0, 0)
    resident = dict(pipeline_mode=pl.Buffered(1))
    return pl.pallas_call(
        functools.partial(_merge_kernel, seq=seq),
        grid=(n_m,),
        in_specs=[pl.BlockSpec((tm, D_MODEL), lambda i: (i, 0)),
                  pl.BlockSpec((None, 1, 6 * D_MODEL), mod_map),
                  pl.BlockSpec((tm, 3 * D_MODEL), lambda i: (i, 0)),
                  pl.BlockSpec((tm, 3 * CONV_W), lambda i: (i, 3)),
                  pl.BlockSpec((tm, SSM_W), lambda i: (i, 6144 // SSM_W)),
                  pl.BlockSpec((SUBLANES, 3 * CONV_W), lambda i: (jnp.maximum(i * hb - 1, 0), 3)),
                  pl.BlockSpec((SUBLANES, 3 * CONV_W),
                               lambda i: (jnp.minimum((i + 1) * hb, n_hb - 1), 3)),
                  pl.BlockSpec((tm, D_MODEL), lambda i: (i, 0)),
                  y_spec, y_spec,
                  pl.BlockSpec((3, CONV_W), const2),
                  pl.BlockSpec((1, CONV_W), const2),
                  pl.BlockSpec((1, SSM_W), const2),
                  pl.BlockSpec((None, SSM_W, SSM_W), lslab, **resident),
                  pl.BlockSpec((1, SSM_W), const2),
                  pl.BlockSpec((None, CONV_W, D_MODEL), lslab, **resident),
                  pl.BlockSpec((None, D_MODEL, D_MODEL), lslab, **resident),
                  pl.BlockSpec((None, SSM_W, D_MODEL), lslab, **resident),
                  pl.BlockSpec((None, D_MODEL, D_MODEL), lslab, **resident)],
        out_specs=pl.BlockSpec((tm, D_MODEL), lambda i: (i, 0)),
        out_shape=jax.ShapeDtypeStruct((t, D_MODEL), F32),
        compiler_params=_cparams(("arbitrary",)),
        name="merge",
    )(x, mod, z, z, z, z, z, y_attn, yf2, yb2,
      lp['conv_w'], lp['conv_b'], lp['ssm_d'], lp['w_glu'], lp['b_glu'],
      lp['w_proj_conv'], lp['w_proj_attn'], lp['w_proj_ssm'], lp['w_out'])


def _ffn_kernel(x_ref, mod_ref, g_ref, wg_ref, wu_ref, wd_ref, fg_ref, o_ref, *, final, tf):
    x = x_ref[...]
    h = _mod_norm(x, g_ref[...], mod_ref[:, 3 * D_MODEL:4 * D_MODEL],
                  mod_ref[:, 4 * D_MODEL:5 * D_MODEL]).astype(BF16)
    acc = None
    for c in range(D_FF // tf):
        cols = slice(c * tf, (c + 1) * tf)
        g = jnp.dot(h, wg_ref[:, cols], preferred_element_type=F32)
        u = jnp.dot(h, wu_ref[:, cols], preferred_element_type=F32)
        part = jnp.dot((jax.nn.silu(g) * u).astype(BF16), wd_ref[cols, :], preferred_element_type=F32)
        acc = part if acc is None else acc + part
    xn = x + mod_ref[:, 5 * D_MODEL:6 * D_MODEL] * acc
    if final:
        ms = jnp.mean(xn * xn, axis=-1, keepdims=True)
        xn = xn * lax.rsqrt(ms + RMS_EPS) * fg_ref[...]
    o_ref[...] = xn


def _ffn_call(x, mod, g2, wg, wu, wd, fg, *, layer, seq, tm, tf, final):
    t = x.shape[0]
    per = max(seq // tm, 1)
    mod_map = (lambda i: (0, 0, 0)) if tm >= seq else (lambda i: (i // per, 0, 0))
    lslab = lambda i: (layer, 0, 0)
    resident = dict(pipeline_mode=pl.Buffered(1))
    return pl.pallas_call(
        functools.partial(_ffn_kernel, final=final, tf=tf),
        grid=(t // tm,),
        in_specs=[pl.BlockSpec((tm, D_MODEL), lambda i: (i, 0)),
                  pl.BlockSpec((None, 1, 6 * D_MODEL), mod_map),
                  pl.BlockSpec((1, D_MODEL), lambda i: (0, 0)),
                  pl.BlockSpec((None, D_MODEL, D_FF), lslab, **resident),
                  pl.BlockSpec((None, D_MODEL, D_FF), lslab, **resident),
                  pl.BlockSpec((None, D_FF, D_MODEL), lslab, **resident),
                  pl.BlockSpec((1, D_MODEL), lambda i: (0, 0))],
        out_specs=pl.BlockSpec((tm, D_MODEL), lambda i: (i, 0)),
        out_shape=jax.ShapeDtypeStruct((t, D_MODEL), F32),
        compiler_params=_cparams(("arbitrary",)),
        name="ffn",
    )(x, mod, g2, wg, wu, wd, fg)


def _state_rows(s):
    b = s.shape[0]
    return jnp.transpose(s, (1, 0, 2, 3)).reshape(2, 2 * b, HALF_STATE)


def _state_unrows(s, b):
    return jnp.transpose(s.reshape(2, b, SSM_GROUPS, SSM_STATE), (1, 0, 2, 3))


def kernel(x_prompt, x_sample, c, cache_k, cache_v, state_ssm_re, state_ssm_im, c_ctx, w_ada, b_ada, norm1_g, w_in, conv_w, conv_b, q_norm_g, k_norm_g, ssm_lambda_re, ssm_lambda_im, ssm_b_re, ssm_b_im, ssm_c_re, ssm_c_im, ssm_log_dt, ssm_d, w_glu, b_glu, w_proj_conv, w_proj_attn, w_proj_ssm, w_out, norm2_g, w_ffn_gate, w_ffn_up, w_ffn_down, final_norm_g):
    nb_c, seq_c, _ = x_prompt.shape
    nb_l, seq_l, _ = x_sample.shape
    past = cache_k.shape[2]

    cvec = jnp.zeros((8, D_MODEL), F32).at[0].set(c_ctx).at[1:1 + nb_l].set(c)
    mod_all = _ada_call(cvec, w_ada, b_ada)

    a_re, a_im, bb_re, bb_im = _disc_call(ssm_lambda_re, ssm_lambda_im, ssm_log_dt, ssm_b_re, ssm_b_im)
    bbt, cct, a8 = _ssm_weights(a_re, a_im, bb_re, bb_im, ssm_c_re, ssm_c_im)

    w_in_p = w_in.astype(BF16)
    wb = {k: v.astype(BF16) for k, v in dict(
        w_glu=w_glu, w_proj_conv=w_proj_conv, w_proj_attn=w_proj_attn, w_proj_ssm=w_proj_ssm,
        w_out=w_out, w_ffn_gate=w_ffn_gate, w_ffn_up=w_ffn_up, w_ffn_down=w_ffn_down).items()}

    bd = jnp.asarray(np.kron(np.eye(2 * LANES // HEAD_DIM, dtype=np.float32),
                             np.ones((HEAD_DIM, HEAD_DIM), np.float32)), dtype=BF16)
    tabs = _rope_tables(seq_l)
    fg = final_norm_g.reshape(1, D_MODEL)

    xc = x_prompt.reshape(nb_c * seq_c, D_MODEL)
    xl = x_sample.reshape(nb_l * seq_l, D_MODEL)
    h0_zero = jnp.zeros((2, 2 * nb_c, 2 * HALF_STATE), F32)
    ck = cache_k.reshape(nb_l, DEPTH, past, N_KV_HEADS * HEAD_DIM)
    cv = cache_v.reshape(nb_l, DEPTH, past, N_KV_HEADS * HEAD_DIM)
    srs, sis = [], []
    new_k = jnp.zeros((nb_c, DEPTH, seq_c, N_KV_HEADS * HEAD_DIM), F32)
    new_v = jnp.zeros((nb_c, DEPTH, seq_c, N_KV_HEADS * HEAD_DIM), F32)

    for l in range(DEPTH):
        lp = dict(conv_w=conv_w[l], conv_b=conv_b[l].reshape(1, CONV_W),
                  ssm_d=ssm_d[l].reshape(1, SSM_W), b_glu=b_glu[l].reshape(1, SSM_W),
                  w_glu=wb['w_glu'], w_proj_conv=wb['w_proj_conv'],
                  w_proj_attn=wb['w_proj_attn'], w_proj_ssm=wb['w_proj_ssm'],
                  w_out=wb['w_out'])
        g1 = norm1_g[l].reshape(1, D_MODEL)
        g2 = norm2_g[l].reshape(1, D_MODEL)
        qg = jnp.tile(q_norm_g[l], N_HEADS).reshape(1, D_MODEL)
        kg = jnp.tile(k_norm_g[l], N_KV_HEADS).reshape(1, 2 * LANES)
        mod_c = mod_all[l, 0:1].reshape(1, 1, 6 * D_MODEL)
        mod_l = mod_all[l, 1:1 + nb_l].reshape(nb_l, 1, 6 * D_MODEL)
        final = l == DEPTH - 1

        z = _inproj_call(xc, mod_c, g1, w_in_p, layer=l, seq=seq_c, tm=512)
        y_attn, new_k, new_v = _attn_call(z, None, None, qg, kg, bd, None, layer=l, nb=nb_c,
                                          seq=seq_c, tq=seq_c, past=0, emit_kv=(new_k, new_v),
                                          tok=seq_c, kchunk=SUBLANES, wave_units=1, barrier=False)
        yf, yb, ht = _ssm_call(z, bbt[l], cct[l], a8[l], h0_zero, nb=nb_c, seq=seq_c, nbg=16)
        xc = _merge_call(xc, mod_c, z, y_attn, yf, yb, lp, layer=l, seq=seq_c, tm=512)
        xc = _ffn_call(xc, mod_c, g2, wb['w_ffn_gate'], wb['w_ffn_up'], wb['w_ffn_down'],
                       fg, layer=l, seq=seq_c, tm=512, tf=1408, final=final)
        srs.append(_state_unrows(ht[:, :, :HALF_STATE], nb_c))
        sis.append(_state_unrows(ht[:, :, HALF_STATE:], nb_c))

        h0 = jnp.concatenate([_state_rows(state_ssm_re[:, l]), _state_rows(state_ssm_im[:, l])], axis=2)
        z = _inproj_call(xl, mod_l, g1, w_in_p, layer=l, seq=seq_l, tm=512)
        (y_attn,) = _attn_call(z, ck, cv, qg, kg, bd, tabs,
                               layer=l, nb=nb_l, seq=seq_l, tq=512, past=past, emit_kv=None,
                               tok=64, kchunk=32, wave_units=2, barrier=True)
        yf, yb, _ = _ssm_call(z, bbt[l], cct[l], a8[l], h0, nb=nb_l, seq=seq_l, nbg=nb_l)
        xl = _merge_call(xl, mod_l, z, y_attn, yf, yb, lp, layer=l, seq=seq_l, tm=512)
        xl = _ffn_call(xl, mod_l, g2, wb['w_ffn_gate'], wb['w_ffn_up'], wb['w_ffn_down'],
                       fg, layer=l, seq=seq_l, tm=512, tf=1408, final=final)

    y_prompt = xc.reshape(nb_c, seq_c, D_MODEL)
    y_sample = xl.reshape(nb_l, seq_l, D_MODEL)
    kv_shape = (nb_c, DEPTH, seq_c, N_KV_HEADS, HEAD_DIM)
    return (y_prompt, y_sample, new_k.reshape(kv_shape), new_v.reshape(kv_shape),
            jnp.stack(srs, axis=1), jnp.stack(sis, axis=1))
```

```python
import functools
import math

import jax
import jax.numpy as jnp
import numpy as np
from jax import lax
from jax.experimental import pallas as pl
from jax.experimental.pallas import tpu as pltpu

F32 = jnp.float32
BF16 = jnp.bfloat16

D_MODEL = 1024
DEPTH = 4
GRID_W = 64
HEAD_DIM = 64
N_HEADS = 16
N_KV_HEADS = 4
ROPE_THETA = 10000.0
CONV_W = 512
SSM_W = 512
SSM_GROUP_CH = 16
SSM_GROUPS = 32
SSM_STATE = 64
D_FF = 2816
RMS_EPS = 1e-6
IN_COLS = 6656

LANES = 128
SUBLANES = 8
VMEM_LIMIT = 56 * 1024 * 1024

Z_TILE = 512
Z_SRC = (3584, 4096, 4608, 5120, 5632, 6144, 1536, 2048, 2560, 0, 512, 1024, 3072)
Z_Q_TILES = (6, 7)
Z_KV_TILE = 8
HALF_STATE = SSM_GROUPS // 2 * SSM_STATE
SSM_ROWS = 512
VT_ROWS = HEAD_DIM + 16


def _cparams(sem):
    return pltpu.CompilerParams(dimension_semantics=sem, vmem_limit_bytes=VMEM_LIMIT)


def _mod_norm(x, g, shift, scale):
    ms = jnp.mean(x * x, axis=-1, keepdims=True)
    y = x * lax.rsqrt(ms + RMS_EPS) * g
    return y * (1.0 + scale) + shift


def _ada_kernel(c_ref, w_ref, b_ref, o_ref):
    s = jax.nn.silu(c_ref[...])
    o_ref[...] = jnp.dot(s.astype(BF16), w_ref[...].astype(BF16),
                         preferred_element_type=F32) + b_ref[...]


def _ada_call(cvec, w_ada, b_ada):
    tn = 1536
    return pl.pallas_call(
        _ada_kernel,
        grid=(DEPTH, 6 * D_MODEL // tn),
        in_specs=[pl.BlockSpec((8, D_MODEL), lambda l, j: (0, 0)),
                  pl.BlockSpec((None, D_MODEL, tn), lambda l, j: (l, 0, j)),
                  pl.BlockSpec((None, 1, tn), lambda l, j: (l, 0, j))],
        out_specs=pl.BlockSpec((None, 8, tn), lambda l, j: (l, 0, j)),
        out_shape=jax.ShapeDtypeStruct((DEPTH, 8, 6 * D_MODEL), F32),
        compiler_params=_cparams(("arbitrary", "arbitrary")),
        name="ada_mod",
    )(cvec, w_ada, b_ada.reshape(DEPTH, 1, 6 * D_MODEL))


def _disc_kernel(lre_ref, lim_ref, ldt_ref, bre_ref, bim_ref,
                 are_ref, aim_ref, bbre_ref, bbim_ref):
    lre = lre_ref[...]
    lim = lim_ref[...]
    dt = jnp.exp(ldt_ref[...])
    mag = jnp.exp(lre * dt)
    a_re = mag * jnp.cos(lim * dt)
    a_im = mag * jnp.sin(lim * dt)
    den = lre * lre + lim * lim
    n_re = a_re - 1.0
    coef_re = (n_re * lre + a_im * lim) / den
    coef_im = (a_im * lre - n_re * lim) / den
    are_ref[...] = a_re
    aim_ref[...] = a_im
    bre = bre_ref[...]
    bim = bim_ref[...]
    bbre_ref[...] = coef_re * bre - coef_im * bim
    bbim_ref[...] = coef_re * bim + coef_im * bre


def _disc_call(lam_re, lam_im, log_dt, b_re, b_im):
    n = DEPTH * 2 * SSM_GROUPS
    lre = lam_re.reshape(n, 1, SSM_STATE)
    lim = lam_im.reshape(n, 1, SSM_STATE)
    ldt = log_dt.reshape(n, 1, 1)
    bre = jnp.swapaxes(b_re.reshape(n, SSM_STATE, SSM_GROUP_CH), 1, 2)
    bim = jnp.swapaxes(b_im.reshape(n, SSM_STATE, SSM_GROUP_CH), 1, 2)
    small = jax.ShapeDtypeStruct((n, 1, SSM_STATE), F32)
    big = jax.ShapeDtypeStruct((n, SSM_GROUP_CH, SSM_STATE), F32)
    return pl.pallas_call(
        _disc_kernel, out_shape=(small, small, big, big), name="ssm_disc",
    )(lre, lim, ldt, bre, bim)


def _ssm_weights(a_re, a_im, bb_re, bb_im, c_re, c_im):
    def bb_tiles(bb):
        return bb.reshape(DEPTH, 2, 2, 4, 4, SSM_GROUP_CH, SSM_STATE)
    bb = jnp.stack([bb_tiles(bb_re), bb_tiles(bb_im)], axis=2)
    sel = np.zeros((4, 8, 4), np.float32)
    for m in range(4):
        for j in range(4):
            sel[m, 4 * (m % 2) + j, j] = 1.0
    bbt = jnp.einsum('ldrhmjcp,mkj->ldrmhkcjp', bb, jnp.asarray(sel))
    bbt = bbt.reshape(DEPTH, 2, 8, 2 * LANES, 2 * LANES).astype(BF16)
    def c_tiles(c):
        return c.reshape(DEPTH, 2, 2, 2, 8, SSM_GROUP_CH, SSM_STATE)
    cc = jnp.stack([c_tiles(c_re), -c_tiles(c_im)], axis=2)
    eye = jnp.eye(8, dtype=F32)
    cct = jnp.einsum('ldrhmkop,kj->ldmrkphjo', cc, eye)
    cct = cct.reshape(DEPTH, 2, 2, 2 * 8 * SSM_STATE, 2 * LANES).astype(BF16)
    def a_rows(a):
        a = a.reshape(DEPTH, 2, 2, HALF_STATE)
        return jnp.tile(a, (1, 1, 4, 1))
    a8 = jnp.stack([a_rows(a_re), a_rows(a_im)], axis=2)
    return bbt, cct, a8


def _inproj_kernel(*refs, rope, emit_kv):
    it = iter(refs)
    x_ref, mod_ref, g_ref, w_ref, qg_ref, kg_ref, bd_ref = (next(it) for _ in range(7))
    cos = sup = sdn = None
    if rope:
        cos, sup, sdn = next(it)[...], next(it)[...], next(it)[...]
    if emit_kv:
        next(it), next(it)
    z_ref = next(it)
    kout_ref, vout_ref = (next(it), next(it)) if emit_kv else (None, None)

    h = _mod_norm(x_ref[...], g_ref[...], mod_ref[:, 0:D_MODEL], mod_ref[:, D_MODEL:2 * D_MODEL])
    h = h.astype(BF16)
    bd = bd_ref[...]
    for c, src in enumerate(Z_SRC):
        zt = jnp.dot(h, w_ref[:, src:src + Z_TILE], preferred_element_type=F32)
        if c in Z_Q_TILES:
            qcols = slice((c - Z_Q_TILES[0]) * Z_TILE, (c - Z_Q_TILES[0] + 1) * Z_TILE)
            zt = zt * lax.rsqrt(_head_ssq(zt, bd) * (1.0 / HEAD_DIM) + RMS_EPS) * qg_ref[:, qcols]
            if rope:
                zt = _rope(zt, cos, sup, sdn)
            zt = zt * (HEAD_DIM ** -0.5 * math.log2(math.e))
        elif c == Z_KV_TILE:
            k = zt[:, :2 * LANES]
            v = zt[:, 2 * LANES:]
            kn = k * lax.rsqrt(_head_ssq(k, bd) * (1.0 / HEAD_DIM) + RMS_EPS) * kg_ref[...]
            if emit_kv:
                kout_ref[...] = kn.reshape(kout_ref.shape)
                vout_ref[...] = v.reshape(vout_ref.shape)
            if rope:
                kn = _rope(kn, cos, sup, sdn)
            zt = jnp.concatenate([kn, v], axis=1)
        z_ref[:, c * Z_TILE:(c + 1) * Z_TILE] = zt.astype(BF16)


def _inproj_call(x, mod, g1, w_in, qg, kg, bd, tabs, caches, *, layer, seq, tm):
    t = x.shape[0]
    per = max(seq // tm, 1)
    mod_map = (lambda i: (0, 0, 0)) if tm >= seq else (lambda i: (i // per, 0, 0))
    in_specs = [pl.BlockSpec((tm, D_MODEL), lambda i: (i, 0)),
                pl.BlockSpec((None, 1, 6 * D_MODEL), mod_map),
                pl.BlockSpec((1, D_MODEL), lambda i: (0, 0)),
                pl.BlockSpec((None, D_MODEL, IN_COLS), lambda i: (layer, 0, 0),
                             pipeline_mode=pl.Buffered(1)),
                pl.BlockSpec((1, D_MODEL), lambda i: (0, 0)),
                pl.BlockSpec((1, 2 * LANES), lambda i: (0, 0)),
                pl.BlockSpec((2 * LANES, 2 * LANES), lambda i: (0, 0))]
    args = [x, mod, g1, w_in, qg, kg, bd]
    if tabs is not None:
        assert tm <= seq
        in_specs += [pl.BlockSpec((tm, LANES), lambda i: (i % per, 0))] * 3
        args += list(tabs)
    out_specs = [pl.BlockSpec((tm, IN_COLS), lambda i: (i, 0))]
    out_shape = [jax.ShapeDtypeStruct((t, IN_COLS), BF16)]
    aliases = {}
    if caches is not None:
        assert tm % seq == 0
        aliases = {len(args): 1, len(args) + 1: 2}
        in_specs += [pl.BlockSpec(memory_space=pl.ANY)] * 2
        args += list(caches)
        out_specs += [pl.BlockSpec((tm // seq, None, seq, 2 * LANES), lambda i: (i, layer, 0, 0))] * 2
        out_shape += [jax.ShapeDtypeStruct(a.shape, a.dtype) for a in caches]
    return pl.pallas_call(
        functools.partial(_inproj_kernel, rope=tabs is not None, emit_kv=caches is not None),
        grid=(t // tm,),
        in_specs=in_specs, out_specs=out_specs, out_shape=out_shape, input_output_aliases=aliases,
        compiler_params=_cparams(("arbitrary",)),
        name="in_proj",
    )(*args)


def _head_ssq(x, bd):
    parts = []
    for t in range(x.shape[1] // (2 * LANES)):
        xs = x[:, 2 * LANES * t:2 * LANES * (t + 1)]
        parts.append(jnp.dot((xs * xs).astype(BF16), bd, preferred_element_type=F32))
    return parts[0] if len(parts) == 1 else jnp.concatenate(parts, axis=1)


def _rope(x, cos, sin_up, sin_dn):
    w = x.shape[1]
    rep = w // LANES
    cos = jnp.tile(cos, (1, rep))
    sin_up = jnp.tile(sin_up, (1, rep))
    sin_dn = jnp.tile(sin_dn, (1, rep))
    return x * cos + pltpu.roll(x, w - 16, 1) * sin_up + pltpu.roll(x, 16, 1) * sin_dn


def _dup_heads(x, lane_lo):
    xr = pltpu.roll(x, HEAD_DIM, 1)
    return jnp.where(lane_lo, x, xr), jnp.where(lane_lo, xr, x)


def _attn_kernel(*refs, seq, past, tok, kchunk, wave_units, barrier):
    it = iter(refs)
    q_ref, kv_ref = next(it), next(it)
    kc_ref = vc_ref = None
    if past:
        kc_ref, vc_ref = next(it), next(it)
    o_ref = next(it)
    k2_scr, vt_scr, qst_scr, s_scr, m_scr, p_scr, ot_scr, o_scr = (next(it) for _ in range(8))

    qi = pl.program_id(1)

    @pl.when(qi == 0)
    def _prep():
        kv = kv_ref[...].astype(F32)
        kn = kv[:, :2 * LANES]
        v = kv[:, 2 * LANES:]
        lane_lo = lax.broadcasted_iota(jnp.int32, (seq, LANES), 1) < HEAD_DIM
        for t in range(2):
            ka, kb = _dup_heads(kn[:, LANES * t:LANES * (t + 1)], lane_lo)
            k2_scr[2 * t, 0:seq, :] = ka.astype(BF16)
            k2_scr[2 * t + 1, 0:seq, :] = kb.astype(BF16)
        vt = v.T.astype(BF16)
        for kvh in range(N_KV_HEADS):
            vt_scr[kvh * VT_ROWS:kvh * VT_ROWS + HEAD_DIM, 0:seq] = vt[kvh * HEAD_DIM:(kvh + 1) * HEAD_DIM]
            vt_scr[kvh * VT_ROWS + HEAD_DIM:(kvh + 1) * VT_ROWS, :] = jnp.ones(
                (VT_ROWS - HEAD_DIM, vt_scr.shape[1]), BF16)
        if past:
            kc = kc_ref[...]
            lane_lo_p = lax.broadcasted_iota(jnp.int32, (past, LANES), 1) < HEAD_DIM
            for t in range(2):
                ka, kb = _dup_heads(kc[:, LANES * t:LANES * (t + 1)], lane_lo_p)
                k2_scr[2 * t, seq:seq + past, :] = ka.astype(BF16)
                k2_scr[2 * t + 1, seq:seq + past, :] = kb.astype(BF16)
            vct = vc_ref[...].T.astype(BF16)
            for kvh in range(N_KV_HEADS):
                vt_scr[kvh * VT_ROWS:kvh * VT_ROWS + HEAD_DIM, seq:seq + past] = (
                    vct[kvh * HEAD_DIM:(kvh + 1) * HEAD_DIM])

    qn = q_ref[...].astype(F32)
    tq = qn.shape[0]

    n_tb = tq // tok
    n_units = N_KV_HEADS * n_tb
    keys = k2_scr.shape[1]
    n_chunks = keys // kchunk
    lane_lo = lax.broadcasted_iota(jnp.int32, (tok, LANES), 1) < HEAD_DIM
    for kvh in range(N_KV_HEADS):
        for tb in range(n_tb):
            stack = []
            for t in (2 * kvh, 2 * kvh + 1):
                qt = qn[tb * tok:(tb + 1) * tok, LANES * t:LANES * (t + 1)]
                stack += [jnp.where(lane_lo, qt, 0.0), jnp.where(lane_lo, 0.0, qt)]
            qst_scr[kvh * n_tb + tb] = jnp.concatenate(stack, axis=0).astype(BF16)

    def kv_of(u):
        return u // n_tb if isinstance(u, int) else lax.shift_right_logical(u, n_tb.bit_length() - 1)

    def scores(u, slot):
        s = lax.dot_general(k2_scr[kv_of(u)], qst_scr[u], (((1,), (1,)), ((), ())),
                            preferred_element_type=F32)
        s_scr[slot] = s
        m_run = s[0:kchunk]
        for c in range(1, n_chunks):
            m_run = jnp.maximum(m_run, s[c * kchunk:(c + 1) * kchunk])
        m_scr[slot] = jnp.broadcast_to(jnp.max(m_run, axis=0, keepdims=True), m_scr.shape[1:])

    def softmax(u, slot):
        del u
        m = jnp.broadcast_to(m_scr[slot, 0:1, :], (kchunk, 4 * tok))
        for c in range(n_chunks):
            p = jnp.exp2(s_scr[slot, c * kchunk:(c + 1) * kchunk, :] - m)
            p_scr[slot, c * kchunk:(c + 1) * kchunk, :] = p.astype(BF16)

    def values(u, slot):
        r0 = kv_of(u) * VT_ROWS
        v_rows = (slice(r0, r0 + VT_ROWS) if isinstance(r0, int)
                  else pl.ds(pl.multiple_of(r0, VT_ROWS), VT_ROWS))
        ot_scr[slot] = jnp.dot(vt_scr[v_rows, :], p_scr[slot], preferred_element_type=F32)

    def finish(u, slot):
        ot = ot_scr[slot]
        ot = ot[0:HEAD_DIM] * (1.0 / ot[HEAD_DIM:HEAD_DIM + 1])
        for pair in range(2):
            if tok % LANES == 0:
                w = jnp.concatenate([ot[:, 2 * pair * tok:(2 * pair + 1) * tok],
                                     ot[:, (2 * pair + 1) * tok:(2 * pair + 2) * tok]], axis=0)
            else:
                both = ot[:, LANES * pair:LANES * (pair + 1)]
                w = jnp.concatenate([both, pltpu.roll(both, tok, 1)], axis=0)
            o_scr[u, pair] = w.T[0:tok].astype(BF16)

    n_waves = n_units // wave_units
    assert n_tb & (n_tb - 1) == 0 and n_units % wave_units == 0
    assert n_waves >= 4 and (n_waves - 1) % 3 == 0

    def wave(g, slot_set, stage):
        for w in range(wave_units):
            stage(g * wave_units + w, slot_set * wave_units + w)

    def step_barrier():
        if barrier:
            pl.delay(1)

    wave(0, 0, scores)
    wave(0, 0, softmax)
    wave(1, 1, scores)
    for w in range(wave_units):
        ot_scr[2 * wave_units + w] = jnp.ones(ot_scr.shape[1:], F32)

    def pipelined(i, carry):
        g = 3 * i + 1
        for k in range(3):
            cur = (1 + k) % 3
            step_barrier()
            wave(jnp.minimum(g + k + 1, n_waves - 1), (cur + 1) % 3, scores)
            wave(g + k, cur, softmax)
            wave(g + k - 1, (cur + 2) % 3, values)
            wave(jnp.maximum(g + k - 2, 0), (cur + 1) % 3, finish)
        return carry

    lax.fori_loop(0, (n_waves - 1) // 3, pipelined, 0)
    step_barrier()
    wave(n_waves - 1, (n_waves - 1) % 3, values)
    wave(n_waves - 2, (n_waves - 2) % 3, finish)
    step_barrier()
    wave(n_waves - 1, (n_waves - 1) % 3, finish)

    for kvh in range(N_KV_HEADS):
        for tb in range(n_tb):
            u = kvh * n_tb + tb
            for pair, t in enumerate((2 * kvh, 2 * kvh + 1)):
                o_ref[tb * tok:(tb + 1) * tok, LANES * t:LANES * (t + 1)] = o_scr[u, pair]


def _attn_call(z, kc, vc, *, layer, nb, seq, tq, past, tok, kchunk, wave_units, barrier):
    nq = seq // tq
    keys = seq + past
    n_units = N_KV_HEADS * (tq // tok)
    q_col = 3072 // D_MODEL
    kv_col = 4096 // 512
    in_specs = [pl.BlockSpec((tq, D_MODEL), lambda b, i: (b * nq + i, q_col)),
                pl.BlockSpec((seq, 512), lambda b, i: (b, kv_col),
                             pipeline_mode=pl.Buffered(1 if nq > 1 else 2))]
    args = [z, z]
    if past:
        in_specs += [pl.BlockSpec((None, None, past, 2 * LANES), lambda b, i: (b, layer, 0, 0))] * 2
        args += [kc, vc]
    return pl.pallas_call(
        functools.partial(_attn_kernel, seq=seq, past=past,
                          tok=tok, kchunk=kchunk, wave_units=wave_units, barrier=barrier),
        grid=(nb, nq),
        in_specs=in_specs,
        out_specs=pl.BlockSpec((tq, D_MODEL), lambda b, i: (b * nq + i, 0)),
        out_shape=jax.ShapeDtypeStruct((nb * seq, D_MODEL), BF16),
        scratch_shapes=[pltpu.VMEM((N_KV_HEADS, keys, LANES), BF16),
                        pltpu.VMEM((N_KV_HEADS * VT_ROWS, keys), BF16),
                        pltpu.VMEM((n_units, 4 * tok, LANES), BF16),
                        pltpu.VMEM((3 * wave_units, keys, 4 * tok), F32),
                        pltpu.VMEM((3 * wave_units, SUBLANES, 4 * tok), F32),
                        pltpu.VMEM((3 * wave_units, keys, 4 * tok), BF16),
                        pltpu.VMEM((3 * wave_units, VT_ROWS, 4 * tok), F32),
                        pltpu.VMEM((n_units, 2, tok, LANES), BF16)],
        compiler_params=_cparams(("arbitrary", "arbitrary")),
        name="attention",
    )(*args)


def _rope_tables(length):
    f = HEAD_DIM // 4
    inv = ROPE_THETA ** (-np.arange(f, dtype=np.float32) / f)
    pos = np.arange(length)
    row = (pos // GRID_W).astype(np.float32)
    col = (pos % GRID_W).astype(np.float32)
    ang_r = row[:, None] * inv[None, :]
    ang_c = col[:, None] * inv[None, :]
    z = np.zeros_like(ang_r)
    cos64 = np.concatenate([np.cos(ang_r), np.cos(ang_r), np.cos(ang_c), np.cos(ang_c)], axis=1)
    up64 = np.concatenate([-np.sin(ang_r), z, -np.sin(ang_c), z], axis=1)
    dn64 = np.concatenate([z, np.sin(ang_r), z, np.sin(ang_c)], axis=1)
    tile2 = lambda a: jnp.asarray(np.tile(a, (1, 2)), dtype=F32)
    return tile2(cos64), tile2(up64), tile2(dn64)


def _ssm_kernel(uf_ref, ub_ref, pin_ref, pout_ref, bbt_ref, cct_ref, a_ref, h0_ref,
                yf_ref, yb_ref, ht_ref, x_scr, hc_scr, *, rows_per_step, steps):
    j = pl.program_id(1)
    r = rows_per_step
    rows = r * steps
    nbg = r // 2
    tok = nbg * steps

    @pl.when(j == 0)
    def _():
        hc_scr[...] = h0_ref[...]

    first_half = (lax.broadcasted_iota(jnp.int32, (rows, LANES), 0) & 1) == 0
    width = SUBLANES * HALF_STATE // r
    u_refs = (uf_ref, ub_ref)
    y_refs = (yf_ref, yb_ref)

    def project_in(d):
        u_bt = u_refs[d][...].reshape(tok, SSM_W)
        u = jnp.dot(pin_ref[...], u_bt, preferred_element_type=F32)
        kcat = []
        for q in range(2):
            u0 = u[:, LANES * q:LANES * (q + 1)]
            u1 = u[:, 2 * LANES + LANES * q:2 * LANES + LANES * (q + 1)]
            kcat.append(jnp.concatenate([jnp.where(first_half, u0, 0.0),
                                         jnp.where(first_half, 0.0, u1)], axis=1).astype(BF16))
        for n in range(8):
            x_scr[d, :, 2 * LANES * n:2 * LANES * (n + 1)] = jnp.dot(
                kcat[(n % 4) // 2], bbt_ref[d, n], preferred_element_type=F32)

    def scan(d):
        for p in range(HALF_STATE // width):
            re_cols = slice(p * width, (p + 1) * width)
            im_cols = slice(HALF_STATE + p * width, HALF_STATE + (p + 1) * width)
            reps = r // SUBLANES
            a_re = jnp.tile(a_ref[d, 0, :, re_cols], (reps, 1))
            a_im = jnp.tile(a_ref[d, 1, :, re_cols], (reps, 1))
            h_re = hc_scr[d, :, re_cols]
            h_im = hc_scr[d, :, im_cols]
            for kk in range(steps):
                k = kk if d == 0 else steps - 1 - kk
                srows = slice(k * r, (k + 1) * r)
                n_re = a_re * h_re - a_im * h_im + x_scr[d, srows, re_cols]
                n_im = a_re * h_im + a_im * h_re + x_scr[d, srows, im_cols]
                x_scr[d, srows, re_cols] = n_re
                x_scr[d, srows, im_cols] = n_im
                h_re, h_im = n_re, n_im
            hc_scr[d, :, re_cols] = h_re
            hc_scr[d, :, im_cols] = h_im

    def project_out(d):
        yy = []
        for m in range(2):
            hk = jnp.concatenate(
                [x_scr[d, :, 512 * m:512 * (m + 1)],
                 x_scr[d, :, HALF_STATE + 512 * m:HALF_STATE + 512 * (m + 1)]], axis=1).astype(BF16)
            yy.append(jnp.dot(hk, cct_ref[d, m], preferred_element_type=F32))
        for hh in range(2):
            ysel = jnp.concatenate([yy[0][:, LANES * hh:LANES * (hh + 1)],
                                    yy[1][:, LANES * hh:LANES * (hh + 1)]], axis=1).astype(BF16)
            y_bt = jnp.dot(pout_ref[hh], ysel, preferred_element_type=F32)
            y_refs[d][:, :, 2 * LANES * hh:2 * LANES * (hh + 1)] = (
                y_bt.reshape(nbg, steps, 2 * LANES).astype(BF16))

    for d in range(2):
        project_in(d)
        scan(d)
        project_out(d)

    @pl.when(j == pl.num_programs(1) - 1)
    def _():
        ht_ref[...] = hc_scr[...]


def _ssm_perms(nbg, steps):
    tok = nbg * steps
    pin = np.zeros((2 * tok, tok), np.float32)
    for t in range(steps):
        for b in range(nbg):
            for h in range(2):
                pin[(t * nbg + b) * 2 + h, b * steps + t] = 1.0
    pout = np.zeros((2, tok, 2 * tok), np.float32)
    for h in range(2):
        pout[h] = (pin * (np.arange(2 * tok)[:, None] % 2 == h)).T
    return jnp.asarray(pin, dtype=BF16), jnp.asarray(pout, dtype=BF16)


def _ssm_call(z, bbt, cct, a8, h0, *, nb, seq, nbg):
    r = 2 * nbg
    steps = SSM_ROWS // r
    n = seq // steps
    ngrp = nb // nbg
    z3 = z.reshape(nb, seq, IN_COLS)
    pin, pout = _ssm_perms(nbg, steps)
    ys = jax.ShapeDtypeStruct((nb, seq, SSM_W), BF16)
    ublk = (nbg, steps, SSM_W)
    ucol = 6144 // SSM_W
    hblk = (2, r, 2 * HALF_STATE)
    c4 = lambda g, j: (0, 0, 0, 0)
    return pl.pallas_call(
        functools.partial(_ssm_kernel, rows_per_step=r, steps=steps),
        grid=(ngrp, n),
        in_specs=[pl.BlockSpec(ublk, lambda g, j: (g, j, ucol)),
                  pl.BlockSpec(ublk, lambda g, j: (g, n - 1 - j, ucol)),
                  pl.BlockSpec(pin.shape, lambda g, j: (0, 0)),
                  pl.BlockSpec(pout.shape, lambda g, j: (0, 0, 0)),
                  pl.BlockSpec(bbt.shape, c4),
                  pl.BlockSpec(cct.shape, c4),
                  pl.BlockSpec(a8.shape, c4),
                  pl.BlockSpec(hblk, lambda g, j: (0, g, 0))],
        out_specs=[pl.BlockSpec(ublk, lambda g, j: (g, j, 0)),
                   pl.BlockSpec(ublk, lambda g, j: (g, n - 1 - j, 0)),
                   pl.BlockSpec(hblk, lambda g, j: (0, g, 0))],
        out_shape=[ys, ys, jax.ShapeDtypeStruct(h0.shape, F32)],
        scratch_shapes=[pltpu.VMEM((2, SSM_ROWS, 2 * HALF_STATE), F32),
                        pltpu.VMEM(hblk, F32)],
        compiler_params=_cparams(("arbitrary", "arbitrary")),
        name="ssm_scan",
    )(z3, z3, pin, pout, bbt, cct, a8, h0)


def _merge_kernel(x_ref, mod_ref, zg_ref, zc_ref, zu_ref, cprev_ref, cnext_ref,
                  ya_ref, yf_ref, yb_ref, cw_ref, cb_ref, sd_ref, wglu_ref, bglu_ref,
                  wpc_ref, wpa_ref, wps_ref, wo_ref, o_ref, *, seq):
    i = pl.program_id(0)
    tm = x_ref.shape[0]
    zc = zc_ref[...].astype(F32)
    cb = zc[:, 0:CONV_W]
    prod = zc[:, CONV_W:2 * CONV_W] * zc[:, 2 * CONV_W:3 * CONV_W]
    cp = cprev_ref[...].astype(F32)
    cn = cnext_ref[...].astype(F32)
    halo_prev = cp[7:8, CONV_W:2 * CONV_W] * cp[7:8, 2 * CONV_W:3 * CONV_W]
    halo_next = cn[0:1, CONV_W:2 * CONV_W] * cn[0:1, 2 * CONV_W:3 * CONV_W]
    row = lax.broadcasted_iota(jnp.int32, (tm, CONV_W), 0)
    tpos = (i * tm + row) % seq
    prev = jnp.where(row == 0, halo_prev, pltpu.roll(prod, 1, 0))
    prev = jnp.where(tpos == 0, 0.0, prev)
    nxt = jnp.where(row == tm - 1, halo_next, pltpu.roll(prod, tm - 1, 0))
    nxt = jnp.where(tpos == seq - 1, 0.0, nxt)
    cw = cw_ref[...]
    y_conv = cb * (cw[0:1] * prev + cw[1:2] * prod + cw[2:3] * nxt + cb_ref[...])

    u = zu_ref[...].astype(F32)
    y = jax.nn.gelu(yf_ref[...].astype(F32) + yb_ref[...].astype(F32) + sd_ref[...] * u)
    y_ssm = y * jax.nn.sigmoid(
        jnp.dot(y.astype(BF16), wglu_ref[...], preferred_element_type=F32) + bglu_ref[...])

    zg = zg_ref[...].astype(F32)
    merged = (jax.nn.sigmoid(zg[:, 0:D_MODEL])
              * jnp.dot(y_conv.astype(BF16), wpc_ref[...], preferred_element_type=F32)
              + jax.nn.sigmoid(zg[:, D_MODEL:2 * D_MODEL])
              * jnp.dot(ya_ref[...], wpa_ref[...], preferred_element_type=F32)
              + jax.nn.sigmoid(zg[:, 2 * D_MODEL:3 * D_MODEL])
              * jnp.dot(y_ssm.astype(BF16), wps_ref[...], preferred_element_type=F32))
    gate1 = mod_ref[:, 2 * D_MODEL:3 * D_MODEL]
    o_ref[...] = x_ref[...] + gate1 * jnp.dot(merged.astype(BF16), wo_ref[...],
                                              preferred_element_type=F32)


def _merge_call(x, mod, z, y_attn, yf, yb, lp, *, layer, seq, tm):
    t = x.shape[0]
    n_m = t // tm
    per = max(seq // tm, 1)
    hb = tm // SUBLANES
    n_hb = t // SUBLANES
    mod_map = (lambda i: (0, 0, 0)) if tm >= seq else (lambda i: (i // per, 0, 0))
    y_spec = pl.BlockSpec((tm, SSM_W), lambda i: (i, 0))
    yf2 = yf.reshape(t, SSM_W)
    yb2 = yb.reshape(t, SSM_W)
    const2 = lambda i: (0, 0)
    lslab = lambda i: (layer, 0, 0)
    resident = dict(pipeline_mode=pl.Buffered(1))
    return pl.pallas_call(
        functools.partial(_merge_kernel, seq=seq),
        grid=(n_m,),
        in_specs=[pl.BlockSpec((tm, D_MODEL), lambda i: (i, 0)),
                  pl.BlockSpec((None, 1, 6 * D_MODEL), mod_map),
                  pl.BlockSpec((tm, 3 * D_MODEL), lambda i: (i, 0)),
                  pl.BlockSpec((tm, 3 * CONV_W), lambda i: (i, 3)),
                  pl.BlockSpec((tm, SSM_W), lambda i: (i, 6144 // SSM_W)),
                  pl.BlockSpec((SUBLANES, 3 * CONV_W), lambda i: (jnp.maximum(i * hb - 1, 0), 3)),
                  pl.BlockSpec((SUBLANES, 3 * CONV_W),
                               lambda i: (jnp.minimum((i + 1) * hb, n_hb - 1), 3)),
                  pl.BlockSpec((tm, D_MODEL), lambda i: (i, 0)),
                  y_spec, y_spec,
                  pl.BlockSpec((3, CONV_W), const2),
                  pl.BlockSpec((1, CONV_W), const2),
                  pl.BlockSpec((1, SSM_W), const2),
                  pl.BlockSpec((None, SSM_W, SSM_W), lslab, **resident),
                  pl.BlockSpec((1, SSM_W), const2),
                  pl.BlockSpec((None, CONV_W, D_MODEL), lslab, **resident),
                  pl.BlockSpec((None, D_MODEL, D_MODEL), lslab, **resident),
                  pl.BlockSpec((None, SSM_W, D_MODEL), lslab, **resident),
                  pl.BlockSpec((None, D_MODEL, D_MODEL), lslab, **resident)],
        out_specs=pl.BlockSpec((tm, D_MODEL), lambda i: (i, 0)),
        out_shape=jax.ShapeDtypeStruct((t, D_MODEL), F32),
        compiler_params=_cparams(("arbitrary",)),
        name="merge",
    )(x, mod, z, z, z, z, z, y_attn, yf2, yb2,
      lp['conv_w'], lp['conv_b'], lp['ssm_d'], lp['w_glu'], lp['b_glu'],
      lp['w_proj_conv'], lp['w_proj_attn'], lp['w_proj_ssm'], lp['w_out'])


def _ffn_kernel(x_ref, mod_ref, g_ref, wg_ref, wu_ref, wd_ref, fg_ref, o_ref, *, final, tf):
    x = x_ref[...]
    h = _mod_norm(x, g_ref[...], mod_ref[:, 3 * D_MODEL:4 * D_MODEL],
                  mod_ref[:, 4 * D_MODEL:5 * D_MODEL]).astype(BF16)
    acc = None
    bounds = list(range(0, D_FF, tf)) + [D_FF]
    for lo, hi in zip(bounds[:-1], bounds[1:]):
        cols = slice(lo, hi)
        g = jnp.dot(h, wg_ref[:, cols], preferred_element_type=F32)
        u = jnp.dot(h, wu_ref[:, cols], preferred_element_type=F32)
        part = jnp.dot((jax.nn.silu(g) * u).astype(BF16), wd_ref[cols, :], preferred_element_type=F32)
        acc = part if acc is None else acc + part
    xn = x + mod_ref[:, 5 * D_MODEL:6 * D_MODEL] * acc
    if final:
        ms = jnp.mean(xn * xn, axis=-1, keepdims=True)
        xn = xn * lax.rsqrt(ms + RMS_EPS) * fg_ref[...]
    o_ref[...] = xn


def _ffn_call(x, mod, g2, wg, wu, wd, fg, *, layer, seq, tm, tf, final):
    t = x.shape[0]
    per = max(seq // tm, 1)
    mod_map = (lambda i: (0, 0, 0)) if tm >= seq else (lambda i: (i // per, 0, 0))
    lslab = lambda i: (layer, 0, 0)
    resident = dict(pipeline_mode=pl.Buffered(1))
    return pl.pallas_call(
        functools.partial(_ffn_kernel, final=final, tf=tf),
        grid=(t // tm,),
        in_specs=[pl.BlockSpec((tm, D_MODEL), lambda i: (i, 0)),
                  pl.BlockSpec((None, 1, 6 * D_MODEL), mod_map),
                  pl.BlockSpec((1, D_MODEL), lambda i: (0, 0)),
                  pl.BlockSpec((None, D_MODEL, D_FF), lslab, **resident),
                  pl.BlockSpec((None, D_MODEL, D_FF), lslab, **resident),
                  pl.BlockSpec((None, D_FF, D_MODEL), lslab, **resident),
                  pl.BlockSpec((1, D_MODEL), lambda i: (0, 0))],
        out_specs=pl.BlockSpec((tm, D_MODEL), lambda i: (i, 0)),
        out_shape=jax.ShapeDtypeStruct((t, D_MODEL), F32),
        compiler_params=_cparams(("arbitrary",)),
        name="ffn",
    )(x, mod, g2, wg, wu, wd, fg)


def _state_rows(s):
    b = s.shape[0]
    return jnp.transpose(s, (1, 0, 2, 3)).reshape(2, 2 * b, HALF_STATE)


def _state_unrows(s, b):
    return jnp.transpose(s.reshape(2, b, SSM_GROUPS, SSM_STATE), (1, 0, 2, 3))


def kernel(x_prompt, x_sample, c, cache_k, cache_v, state_ssm_re, state_ssm_im, c_ctx, w_ada, b_ada, norm1_g, w_in, conv_w, conv_b, q_norm_g, k_norm_g, ssm_lambda_re, ssm_lambda_im, ssm_b_re, ssm_b_im, ssm_c_re, ssm_c_im, ssm_log_dt, ssm_d, w_glu, b_glu, w_proj_conv, w_proj_attn, w_proj_ssm, w_out, norm2_g, w_ffn_gate, w_ffn_up, w_ffn_down, final_norm_g):
    nb_c, seq_c, _ = x_prompt.shape
    nb_l, seq_l, _ = x_sample.shape
    past = cache_k.shape[2]

    cvec = jnp.zeros((8, D_MODEL), F32).at[0].set(c_ctx).at[1:1 + nb_l].set(c)
    mod_all = _ada_call(cvec, w_ada, b_ada)

    a_re, a_im, bb_re, bb_im = _disc_call(ssm_lambda_re, ssm_lambda_im, ssm_log_dt, ssm_b_re, ssm_b_im)
    bbt, cct, a8 = _ssm_weights(a_re, a_im, bb_re, bb_im, ssm_c_re, ssm_c_im)

    w_in_p = w_in.astype(BF16)
    wb = {k: v.astype(BF16) for k, v in dict(
        w_glu=w_glu, w_proj_conv=w_proj_conv, w_proj_attn=w_proj_attn, w_proj_ssm=w_proj_ssm,
        w_out=w_out, w_ffn_gate=w_ffn_gate, w_ffn_up=w_ffn_up, w_ffn_down=w_ffn_down).items()}

    bd = jnp.asarray(np.kron(np.eye(2 * LANES // HEAD_DIM, dtype=np.float32),
                             np.ones((HEAD_DIM, HEAD_DIM), np.float32)), dtype=BF16)
    tabs = _rope_tables(seq_l)
    fg = final_norm_g.reshape(1, D_MODEL)

    xc = x_prompt.reshape(nb_c * seq_c, D_MODEL)
    xl = x_sample.reshape(nb_l * seq_l, D_MODEL)
    h0_zero = jnp.zeros((2, 2 * nb_c, 2 * HALF_STATE), F32)
    ck = cache_k.reshape(nb_l, DEPTH, past, N_KV_HEADS * HEAD_DIM)
    cv = cache_v.reshape(nb_l, DEPTH, past, N_KV_HEADS * HEAD_DIM)
    srs, sis = [], []
    new_k = jnp.zeros((nb_c, DEPTH, seq_c, N_KV_HEADS * HEAD_DIM), F32)
    new_v = jnp.zeros((nb_c, DEPTH, seq_c, N_KV_HEADS * HEAD_DIM), F32)

    for l in range(DEPTH):
        lp = dict(conv_w=conv_w[l], conv_b=conv_b[l].reshape(1, CONV_W),
                  ssm_d=ssm_d[l].reshape(1, SSM_W), b_glu=b_glu[l].reshape(1, SSM_W),
                  w_glu=wb['w_glu'], w_proj_conv=wb['w_proj_conv'],
                  w_proj_attn=wb['w_proj_attn'], w_proj_ssm=wb['w_proj_ssm'],
                  w_out=wb['w_out'])
        g1 = norm1_g[l].reshape(1, D_MODEL)
        g2 = norm2_g[l].reshape(1, D_MODEL)
        qg = jnp.tile(q_norm_g[l], N_HEADS).reshape(1, D_MODEL)
        kg = jnp.tile(k_norm_g[l], N_KV_HEADS).reshape(1, 2 * LANES)
        mod_c = mod_all[l, 0:1].reshape(1, 1, 6 * D_MODEL)
        mod_l = mod_all[l, 1:1 + nb_l].reshape(nb_l, 1, 6 * D_MODEL)
        final = l == DEPTH - 1

        z, new_k, new_v = _inproj_call(xc, mod_c, g1, w_in_p, qg, kg, bd, None, (new_k, new_v),
                                       layer=l, seq=seq_c, tm=512)
        y_attn = _attn_call(z, None, None, layer=l, nb=nb_c, seq=seq_c, tq=seq_c, past=0,
                            tok=seq_c, kchunk=SUBLANES, wave_units=1, barrier=False)
        yf, yb, ht = _ssm_call(z, bbt[l], cct[l], a8[l], h0_zero, nb=nb_c, seq=seq_c, nbg=16)
        xc = _merge_call(xc, mod_c, z, y_attn, yf, yb, lp, layer=l, seq=seq_c, tm=512)
        xc = _ffn_call(xc, mod_c, g2, wb['w_ffn_gate'], wb['w_ffn_up'], wb['w_ffn_down'],
                       fg, layer=l, seq=seq_c, tm=512, tf=1536, final=final)
        srs.append(_state_unrows(ht[:, :, :HALF_STATE], nb_c))
        sis.append(_state_unrows(ht[:, :, HALF_STATE:], nb_c))

        h0 = jnp.concatenate([_state_rows(state_ssm_re[:, l]), _state_rows(state_ssm_im[:, l])], axis=2)
        (z,) = _inproj_call(xl, mod_l, g1, w_in_p, qg, kg, bd, tabs, None,
                            layer=l, seq=seq_l, tm=512)
        y_attn = _attn_call(z, ck, cv, layer=l, nb=nb_l, seq=seq_l, tq=512, past=past,
                            tok=64, kchunk=32, wave_units=2, barrier=True)
        yf, yb, _ = _ssm_call(z, bbt[l], cct[l], a8[l], h0, nb=nb_l, seq=seq_l, nbg=nb_l)
        xl = _merge_call(xl, mod_l, z, y_attn, yf, yb, lp, layer=l, seq=seq_l, tm=512)
        xl = _ffn_call(xl, mod_l, g2, wb['w_ffn_gate'], wb['w_ffn_up'], wb['w_ffn_down'],
                       fg, layer=l, seq=seq_l, tm=512, tf=1536, final=final)

    y_prompt = xc.reshape(nb_c, seq_c, D_MODEL)
    y_sample = xl.reshape(nb_l, seq_l, D_MODEL)
    kv_shape = (nb_c, DEPTH, seq_c, N_KV_HEADS, HEAD_DIM)
    return (y_prompt, y_sample, new_k.reshape(kv_shape), new_v.reshape(kv_shape),
            jnp.stack(srs, axis=1), jnp.stack(sis, axis=1))
```

```python
import functools
import math

import jax
import jax.numpy as jnp
import numpy as np
from jax import lax
from jax.experimental import pallas as pl
from jax.experimental.pallas import tpu as pltpu

F32 = jnp.float32
BF16 = jnp.bfloat16

D_MODEL = 1024
DEPTH = 4
GRID_W = 64
HEAD_DIM = 64
N_HEADS = 16
N_KV_HEADS = 4
ROPE_THETA = 10000.0
CONV_W = 512
SSM_W = 512
SSM_GROUP_CH = 16
SSM_GROUPS = 32
SSM_STATE = 64
D_FF = 2816
RMS_EPS = 1e-6
IN_COLS = 6656

LANES = 128
SUBLANES = 8
VMEM_LIMIT = 56 * 1024 * 1024

Z_TILE = 512
Z_SRC = (3584, 4096, 4608, 5120, 5632, 6144, 1536, 2048, 2560, 0, 512, 1024, 3072)
Z_Q_TILES = (6, 7)
Z_KV_TILE = 8
HALF_STATE = SSM_GROUPS // 2 * SSM_STATE
SSM_ROWS = 512
VT_ROWS = HEAD_DIM + 16

TOKEN_TILE = 512
FFN_CHUNK = 6 * 2 * LANES
SSM_MAX_BATCH = 16


def _attn_tiling(seq):
    if seq <= 2 * LANES:
        return dict(tq=seq, tok=seq, kchunk=SUBLANES, wave_units=1, barrier=False)
    return dict(tq=8 * LANES, tok=HEAD_DIM, kchunk=4 * SUBLANES, wave_units=2, barrier=True)


def _cparams(sem):
    return pltpu.CompilerParams(dimension_semantics=sem, vmem_limit_bytes=VMEM_LIMIT)


def _mod_norm(x, g, shift, scale):
    ms = jnp.mean(x * x, axis=-1, keepdims=True)
    y = x * lax.rsqrt(ms + RMS_EPS) * g
    return y * (1.0 + scale) + shift


def _ada_kernel(c_ref, w_ref, b_ref, o_ref):
    s = jax.nn.silu(c_ref[...])
    o_ref[...] = jnp.dot(s.astype(BF16), w_ref[...].astype(BF16),
                         preferred_element_type=F32) + b_ref[...]


def _ada_call(cvec, w_ada, b_ada):
    tn = 1536
    return pl.pallas_call(
        _ada_kernel,
        grid=(DEPTH, 6 * D_MODEL // tn),
        in_specs=[pl.BlockSpec((8, D_MODEL), lambda l, j: (0, 0)),
                  pl.BlockSpec((None, D_MODEL, tn), lambda l, j: (l, 0, j)),
                  pl.BlockSpec((None, 1, tn), lambda l, j: (l, 0, j))],
        out_specs=pl.BlockSpec((None, 8, tn), lambda l, j: (l, 0, j)),
        out_shape=jax.ShapeDtypeStruct((DEPTH, 8, 6 * D_MODEL), F32),
        compiler_params=_cparams(("arbitrary", "arbitrary")),
        name="ada_mod",
    )(cvec, w_ada, b_ada.reshape(DEPTH, 1, 6 * D_MODEL))


def _disc_kernel(lre_ref, lim_ref, ldt_ref, bre_ref, bim_ref,
                 are_ref, aim_ref, bbre_ref, bbim_ref):
    lre = lre_ref[...]
    lim = lim_ref[...]
    dt = jnp.exp(ldt_ref[...])
    mag = jnp.exp(lre * dt)
    a_re = mag * jnp.cos(lim * dt)
    a_im = mag * jnp.sin(lim * dt)
    den = lre * lre + lim * lim
    n_re = a_re - 1.0
    coef_re = (n_re * lre + a_im * lim) / den
    coef_im = (a_im * lre - n_re * lim) / den
    are_ref[...] = a_re
    aim_ref[...] = a_im
    bre = bre_ref[...]
    bim = bim_ref[...]
    bbre_ref[...] = coef_re * bre - coef_im * bim
    bbim_ref[...] = coef_re * bim + coef_im * bre


def _disc_call(lam_re, lam_im, log_dt, b_re, b_im):
    n = DEPTH * 2 * SSM_GROUPS
    lre = lam_re.reshape(n, 1, SSM_STATE)
    lim = lam_im.reshape(n, 1, SSM_STATE)
    ldt = log_dt.reshape(n, 1, 1)
    bre = jnp.swapaxes(b_re.reshape(n, SSM_STATE, SSM_GROUP_CH), 1, 2)
    bim = jnp.swapaxes(b_im.reshape(n, SSM_STATE, SSM_GROUP_CH), 1, 2)
    small = jax.ShapeDtypeStruct((n, 1, SSM_STATE), F32)
    big = jax.ShapeDtypeStruct((n, SSM_GROUP_CH, SSM_STATE), F32)
    return pl.pallas_call(
        _disc_kernel, out_shape=(small, small, big, big), name="ssm_disc",
    )(lre, lim, ldt, bre, bim)


def _ssm_weights(a_re, a_im, bb_re, bb_im, c_re, c_im):
    def bb_tiles(bb):
        return bb.reshape(DEPTH, 2, 2, 4, 4, SSM_GROUP_CH, SSM_STATE)
    bb = jnp.stack([bb_tiles(bb_re), bb_tiles(bb_im)], axis=2)
    sel = np.zeros((4, 8, 4), np.float32)
    for m in range(4):
        for j in range(4):
            sel[m, 4 * (m % 2) + j, j] = 1.0
    bbt = jnp.einsum('ldrhmjcp,mkj->ldrmhkcjp', bb, jnp.asarray(sel))
    bbt = bbt.reshape(DEPTH, 2, 8, 2 * LANES, 2 * LANES).astype(BF16)
    def c_tiles(c):
        return c.reshape(DEPTH, 2, 2, 2, 8, SSM_GROUP_CH, SSM_STATE)
    cc = jnp.stack([c_tiles(c_re), -c_tiles(c_im)], axis=2)
    eye = jnp.eye(8, dtype=F32)
    cct = jnp.einsum('ldrhmkop,kj->ldmrkphjo', cc, eye)
    cct = cct.reshape(DEPTH, 2, 2, 2 * 8 * SSM_STATE, 2 * LANES).astype(BF16)
    def a_rows(a):
        a = a.reshape(DEPTH, 2, 2, HALF_STATE)
        return jnp.tile(a, (1, 1, 4, 1))
    a8 = jnp.stack([a_rows(a_re), a_rows(a_im)], axis=2)
    return bbt, cct, a8


def _inproj_kernel(*refs, rope, emit_kv):
    it = iter(refs)
    x_ref, mod_ref, g_ref, w_ref, qg_ref, kg_ref, bd_ref = (next(it) for _ in range(7))
    cos = sup = sdn = None
    if rope:
        cos, sup, sdn = next(it)[...], next(it)[...], next(it)[...]
    if emit_kv:
        next(it), next(it)
    z_ref = next(it)
    kout_ref, vout_ref = (next(it), next(it)) if emit_kv else (None, None)

    h = _mod_norm(x_ref[...], g_ref[...], mod_ref[:, 0:D_MODEL], mod_ref[:, D_MODEL:2 * D_MODEL])
    h = h.astype(BF16)
    bd = bd_ref[...]
    for c, src in enumerate(Z_SRC):
        zt = jnp.dot(h, w_ref[:, src:src + Z_TILE], preferred_element_type=F32)
        if c in Z_Q_TILES:
            qcols = slice((c - Z_Q_TILES[0]) * Z_TILE, (c - Z_Q_TILES[0] + 1) * Z_TILE)
            zt = zt * lax.rsqrt(_head_ssq(zt, bd) * (1.0 / HEAD_DIM) + RMS_EPS) * qg_ref[:, qcols]
            if rope:
                zt = _rope(zt, cos, sup, sdn)
            zt = zt * (HEAD_DIM ** -0.5 * math.log2(math.e))
        elif c == Z_KV_TILE:
            k = zt[:, :2 * LANES]
            v = zt[:, 2 * LANES:]
            kn = k * lax.rsqrt(_head_ssq(k, bd) * (1.0 / HEAD_DIM) + RMS_EPS) * kg_ref[...]
            if emit_kv:
                kout_ref[...] = kn.reshape(kout_ref.shape)
                vout_ref[...] = v.reshape(vout_ref.shape)
            if rope:
                kn = _rope(kn, cos, sup, sdn)
            zt = jnp.concatenate([kn, v], axis=1)
        z_ref[:, c * Z_TILE:(c + 1) * Z_TILE] = zt.astype(BF16)


def _inproj_call(x, mod, g1, w_in, qg, kg, bd, tabs, caches, *, layer, seq, tm):
    t = x.shape[0]
    per = max(seq // tm, 1)
    mod_map = (lambda i: (0, 0, 0)) if tm >= seq else (lambda i: (i // per, 0, 0))
    in_specs = [pl.BlockSpec((tm, D_MODEL), lambda i: (i, 0)),
                pl.BlockSpec((None, 1, 6 * D_MODEL), mod_map),
                pl.BlockSpec((1, D_MODEL), lambda i: (0, 0)),
                pl.BlockSpec((None, D_MODEL, IN_COLS), lambda i: (layer, 0, 0),
                             pipeline_mode=pl.Buffered(1)),
                pl.BlockSpec((1, D_MODEL), lambda i: (0, 0)),
                pl.BlockSpec((1, 2 * LANES), lambda i: (0, 0)),
                pl.BlockSpec((2 * LANES, 2 * LANES), lambda i: (0, 0))]
    args = [x, mod, g1, w_in, qg, kg, bd]
    if tabs is not None:
        assert tm <= seq
        in_specs += [pl.BlockSpec((tm, LANES), lambda i: (i % per, 0))] * 3
        args += list(tabs)
    out_specs = [pl.BlockSpec((tm, IN_COLS), lambda i: (i, 0))]
    out_shape = [jax.ShapeDtypeStruct((t, IN_COLS), BF16)]
    aliases = {}
    if caches is not None:
        assert tm % seq == 0
        aliases = {len(args): 1, len(args) + 1: 2}
        in_specs += [pl.BlockSpec(memory_space=pl.ANY)] * 2
        args += list(caches)
        out_specs += [pl.BlockSpec((tm // seq, None, seq, 2 * LANES), lambda i: (i, layer, 0, 0))] * 2
        out_shape += [jax.ShapeDtypeStruct(a.shape, a.dtype) for a in caches]
    return pl.pallas_call(
        functools.partial(_inproj_kernel, rope=tabs is not None, emit_kv=caches is not None),
        grid=(t // tm,),
        in_specs=in_specs, out_specs=out_specs, out_shape=out_shape, input_output_aliases=aliases,
        compiler_params=_cparams(("arbitrary",)),
        name="in_proj",
    )(*args)


def _head_ssq(x, bd):
    parts = []
    for t in range(x.shape[1] // (2 * LANES)):
        xs = x[:, 2 * LANES * t:2 * LANES * (t + 1)]
        parts.append(jnp.dot((xs * xs).astype(BF16), bd, preferred_element_type=F32))
    return parts[0] if len(parts) == 1 else jnp.concatenate(parts, axis=1)


def _rope(x, cos, sin_up, sin_dn):
    w = x.shape[1]
    rep = w // LANES
    cos = jnp.tile(cos, (1, rep))
    sin_up = jnp.tile(sin_up, (1, rep))
    sin_dn = jnp.tile(sin_dn, (1, rep))
    return x * cos + pltpu.roll(x, w - 16, 1) * sin_up + pltpu.roll(x, 16, 1) * sin_dn


def _dup_heads(x, lane_lo):
    xr = pltpu.roll(x, HEAD_DIM, 1)
    return jnp.where(lane_lo, x, xr), jnp.where(lane_lo, xr, x)


def _attn_kernel(*refs, seq, past, tok, kchunk, wave_units, barrier):
    it = iter(refs)
    q_ref, kv_ref = next(it), next(it)
    kc_ref = vc_ref = None
    if past:
        kc_ref, vc_ref = next(it), next(it)
    o_ref = next(it)
    k2_scr, vt_scr, qst_scr, s_scr, m_scr, p_scr, ot_scr, o_scr = (next(it) for _ in range(8))

    qi = pl.program_id(1)

    @pl.when(qi == 0)
    def _prep():
        kv = kv_ref[...].astype(F32)
        kn = kv[:, :2 * LANES]
        v = kv[:, 2 * LANES:]
        lane_lo = lax.broadcasted_iota(jnp.int32, (seq, LANES), 1) < HEAD_DIM
        for t in range(2):
            ka, kb = _dup_heads(kn[:, LANES * t:LANES * (t + 1)], lane_lo)
            k2_scr[2 * t, 0:seq, :] = ka.astype(BF16)
            k2_scr[2 * t + 1, 0:seq, :] = kb.astype(BF16)
        vt = v.T.astype(BF16)
        for kvh in range(N_KV_HEADS):
            vt_scr[kvh * VT_ROWS:kvh * VT_ROWS + HEAD_DIM, 0:seq] = vt[kvh * HEAD_DIM:(kvh + 1) * HEAD_DIM]
            vt_scr[kvh * VT_ROWS + HEAD_DIM:(kvh + 1) * VT_ROWS, :] = jnp.ones(
                (VT_ROWS - HEAD_DIM, vt_scr.shape[1]), BF16)
        if past:
            kc = kc_ref[...]
            lane_lo_p = lax.broadcasted_iota(jnp.int32, (past, LANES), 1) < HEAD_DIM
            for t in range(2):
                ka, kb = _dup_heads(kc[:, LANES * t:LANES * (t + 1)], lane_lo_p)
                k2_scr[2 * t, seq:seq + past, :] = ka.astype(BF16)
                k2_scr[2 * t + 1, seq:seq + past, :] = kb.astype(BF16)
            vct = vc_ref[...].T.astype(BF16)
            for kvh in range(N_KV_HEADS):
                vt_scr[kvh * VT_ROWS:kvh * VT_ROWS + HEAD_DIM, seq:seq + past] = (
                    vct[kvh * HEAD_DIM:(kvh + 1) * HEAD_DIM])

    qn = q_ref[...].astype(F32)
    tq = qn.shape[0]

    n_tb = tq // tok
    n_units = N_KV_HEADS * n_tb
    keys = k2_scr.shape[1]
    n_chunks = keys // kchunk
    lane_lo = lax.broadcasted_iota(jnp.int32, (tok, LANES), 1) < HEAD_DIM
    for kvh in range(N_KV_HEADS):
        for tb in range(n_tb):
            stack = []
            for t in (2 * kvh, 2 * kvh + 1):
                qt = qn[tb * tok:(tb + 1) * tok, LANES * t:LANES * (t + 1)]
                stack += [jnp.where(lane_lo, qt, 0.0), jnp.where(lane_lo, 0.0, qt)]
            qst_scr[kvh * n_tb + tb] = jnp.concatenate(stack, axis=0).astype(BF16)

    def kv_of(u):
        return u // n_tb if isinstance(u, int) else lax.shift_right_logical(u, n_tb.bit_length() - 1)

    def scores(u, slot):
        s = lax.dot_general(k2_scr[kv_of(u)], qst_scr[u], (((1,), (1,)), ((), ())),
                            preferred_element_type=F32)
        s_scr[slot] = s
        m_run = s[0:kchunk]
        for c in range(1, n_chunks):
            m_run = jnp.maximum(m_run, s[c * kchunk:(c + 1) * kchunk])
        m_scr[slot] = jnp.broadcast_to(jnp.max(m_run, axis=0, keepdims=True), m_scr.shape[1:])

    def softmax(u, slot):
        del u
        m = jnp.broadcast_to(m_scr[slot, 0:1, :], (kchunk, 4 * tok))
        for c in range(n_chunks):
            p = jnp.exp2(s_scr[slot, c * kchunk:(c + 1) * kchunk, :] - m)
            p_scr[slot, c * kchunk:(c + 1) * kchunk, :] = p.astype(BF16)

    def values(u, slot):
        r0 = kv_of(u) * VT_ROWS
        v_rows = (slice(r0, r0 + VT_ROWS) if isinstance(r0, int)
                  else pl.ds(pl.multiple_of(r0, VT_ROWS), VT_ROWS))
        ot_scr[slot] = jnp.dot(vt_scr[v_rows, :], p_scr[slot], preferred_element_type=F32)

    def finish(u, slot):
        ot = ot_scr[slot]
        ot = ot[0:HEAD_DIM] * (1.0 / ot[HEAD_DIM:HEAD_DIM + 1])
        for pair in range(2):
            if tok % LANES == 0:
                w = jnp.concatenate([ot[:, 2 * pair * tok:(2 * pair + 1) * tok],
                                     ot[:, (2 * pair + 1) * tok:(2 * pair + 2) * tok]], axis=0)
            else:
                both = ot[:, LANES * pair:LANES * (pair + 1)]
                w = jnp.concatenate([both, pltpu.roll(both, tok, 1)], axis=0)
            o_scr[u, pair] = w.T[0:tok].astype(BF16)

    n_waves = n_units // wave_units
    assert n_tb & (n_tb - 1) == 0 and n_units % wave_units == 0
    assert n_waves >= 4

    def wave(g, slot_set, stage):
        for w in range(wave_units):
            stage(g * wave_units + w, slot_set * wave_units + w)

    def step_barrier():
        if barrier:
            pl.delay(1)

    wave(0, 0, scores)
    wave(0, 0, softmax)
    wave(1, 1, scores)
    for w in range(wave_units):
        ot_scr[2 * wave_units + w] = jnp.ones(ot_scr.shape[1:], F32)

    def step(c, cur):
        step_barrier()
        wave(jnp.minimum(c + 1, n_waves - 1), (cur + 1) % 3, scores)
        wave(c, cur, softmax)
        wave(c - 1, (cur + 2) % 3, values)
        wave(jnp.maximum(c - 2, 0), (cur + 1) % 3, finish)

    def pipelined(i, carry):
        for k in range(3):
            step(3 * i + 1 + k, (1 + k) % 3)
        return carry

    rolled = (n_waves - 1) // 3
    lax.fori_loop(0, rolled, pipelined, 0)
    for c in range(3 * rolled + 1, n_waves):
        step(c, c % 3)
    step_barrier()
    wave(n_waves - 1, (n_waves - 1) % 3, values)
    wave(n_waves - 2, (n_waves - 2) % 3, finish)
    step_barrier()
    wave(n_waves - 1, (n_waves - 1) % 3, finish)

    for kvh in range(N_KV_HEADS):
        for tb in range(n_tb):
            u = kvh * n_tb + tb
            for pair, t in enumerate((2 * kvh, 2 * kvh + 1)):
                o_ref[tb * tok:(tb + 1) * tok, LANES * t:LANES * (t + 1)] = o_scr[u, pair]


def _attn_call(z, kc, vc, *, layer, nb, seq, tq, past, tok, kchunk, wave_units, barrier):
    nq = seq // tq
    keys = seq + past
    n_units = N_KV_HEADS * (tq // tok)
    q_col = 3072 // D_MODEL
    kv_col = 4096 // 512
    in_specs = [pl.BlockSpec((tq, D_MODEL), lambda b, i: (b * nq + i, q_col)),
                pl.BlockSpec((seq, 512), lambda b, i: (b, kv_col),
                             pipeline_mode=pl.Buffered(1 if nq > 1 else 2))]
    args = [z, z]
    if past:
        in_specs += [pl.BlockSpec((None, None, past, 2 * LANES), lambda b, i: (b, layer, 0, 0))] * 2
        args += [kc, vc]
    return pl.pallas_call(
        functools.partial(_attn_kernel, seq=seq, past=past,
                          tok=tok, kchunk=kchunk, wave_units=wave_units, barrier=barrier),
        grid=(nb, nq),
        in_specs=in_specs,
        out_specs=pl.BlockSpec((tq, D_MODEL), lambda b, i: (b * nq + i, 0)),
        out_shape=jax.ShapeDtypeStruct((nb * seq, D_MODEL), BF16),
        scratch_shapes=[pltpu.VMEM((N_KV_HEADS, keys, LANES), BF16),
                        pltpu.VMEM((N_KV_HEADS * VT_ROWS, keys), BF16),
                        pltpu.VMEM((n_units, 4 * tok, LANES), BF16),
                        pltpu.VMEM((3 * wave_units, keys, 4 * tok), F32),
                        pltpu.VMEM((3 * wave_units, SUBLANES, 4 * tok), F32),
                        pltpu.VMEM((3 * wave_units, keys, 4 * tok), BF16),
                        pltpu.VMEM((3 * wave_units, VT_ROWS, 4 * tok), F32),
                        pltpu.VMEM((n_units, 2, tok, LANES), BF16)],
        compiler_params=_cparams(("arbitrary", "arbitrary")),
        name="attention",
    )(*args)


def _rope_tables(length):
    f = HEAD_DIM // 4
    inv = ROPE_THETA ** (-np.arange(f, dtype=np.float32) / f)
    pos = np.arange(length)
    row = (pos // GRID_W).astype(np.float32)
    col = (pos % GRID_W).astype(np.float32)
    ang_r = row[:, None] * inv[None, :]
    ang_c = col[:, None] * inv[None, :]
    z = np.zeros_like(ang_r)
    cos64 = np.concatenate([np.cos(ang_r), np.cos(ang_r), np.cos(ang_c), np.cos(ang_c)], axis=1)
    up64 = np.concatenate([-np.sin(ang_r), z, -np.sin(ang_c), z], axis=1)
    dn64 = np.concatenate([z, np.sin(ang_r), z, np.sin(ang_c)], axis=1)
    tile2 = lambda a: jnp.asarray(np.tile(a, (1, 2)), dtype=F32)
    return tile2(cos64), tile2(up64), tile2(dn64)


def _ssm_kernel(uf_ref, ub_ref, pin_ref, pout_ref, bbt_ref, cct_ref, a_ref, h0_ref,
                yf_ref, yb_ref, ht_ref, x_scr, hc_scr, *, rows_per_step, steps):
    j = pl.program_id(1)
    r = rows_per_step
    rows = r * steps
    nbg = r // 2
    tok = nbg * steps

    @pl.when(j == 0)
    def _():
        hc_scr[...] = h0_ref[...]

    first_half = (lax.broadcasted_iota(jnp.int32, (rows, LANES), 0) & 1) == 0
    width = SUBLANES * HALF_STATE // r
    u_refs = (uf_ref, ub_ref)
    y_refs = (yf_ref, yb_ref)

    def project_in(d):
        u_bt = u_refs[d][...].reshape(tok, SSM_W)
        u = jnp.dot(pin_ref[...], u_bt, preferred_element_type=F32)
        kcat = []
        for q in range(2):
            u0 = u[:, LANES * q:LANES * (q + 1)]
            u1 = u[:, 2 * LANES + LANES * q:2 * LANES + LANES * (q + 1)]
            kcat.append(jnp.concatenate([jnp.where(first_half, u0, 0.0),
                                         jnp.where(first_half, 0.0, u1)], axis=1).astype(BF16))
        for n in range(8):
            x_scr[d, :, 2 * LANES * n:2 * LANES * (n + 1)] = jnp.dot(
                kcat[(n % 4) // 2], bbt_ref[d, n], preferred_element_type=F32)

    def scan(d):
        for p in range(HALF_STATE // width):
            re_cols = slice(p * width, (p + 1) * width)
            im_cols = slice(HALF_STATE + p * width, HALF_STATE + (p + 1) * width)
            reps = r // SUBLANES
            a_re = jnp.tile(a_ref[d, 0, :, re_cols], (reps, 1))
            a_im = jnp.tile(a_ref[d, 1, :, re_cols], (reps, 1))
            h_re = hc_scr[d, :, re_cols]
            h_im = hc_scr[d, :, im_cols]
            for kk in range(steps):
                k = kk if d == 0 else steps - 1 - kk
                srows = slice(k * r, (k + 1) * r)
                n_re = a_re * h_re - a_im * h_im + x_scr[d, srows, re_cols]
                n_im = a_re * h_im + a_im * h_re + x_scr[d, srows, im_cols]
                x_scr[d, srows, re_cols] = n_re
                x_scr[d, srows, im_cols] = n_im
                h_re, h_im = n_re, n_im
            hc_scr[d, :, re_cols] = h_re
            hc_scr[d, :, im_cols] = h_im

    def project_out(d):
        yy = []
        for m in range(2):
            hk = jnp.concatenate(
                [x_scr[d, :, 512 * m:512 * (m + 1)],
                 x_scr[d, :, HALF_STATE + 512 * m:HALF_STATE + 512 * (m + 1)]], axis=1).astype(BF16)
            yy.append(jnp.dot(hk, cct_ref[d, m], preferred_element_type=F32))
        for hh in range(2):
            ysel = jnp.concatenate([yy[0][:, LANES * hh:LANES * (hh + 1)],
                                    yy[1][:, LANES * hh:LANES * (hh + 1)]], axis=1).astype(BF16)
            y_bt = jnp.dot(pout_ref[hh], ysel, preferred_element_type=F32)
            y_refs[d][:, :, 2 * LANES * hh:2 * LANES * (hh + 1)] = (
                y_bt.reshape(nbg, steps, 2 * LANES).astype(BF16))

    for d in range(2):
        project_in(d)
        scan(d)
        project_out(d)

    @pl.when(j == pl.num_programs(1) - 1)
    def _():
        ht_ref[...] = hc_scr[...]


def _ssm_perms(nbg, steps):
    tok = nbg * steps
    pin = np.zeros((2 * tok, tok), np.float32)
    for t in range(steps):
        for b in range(nbg):
            for h in range(2):
                pin[(t * nbg + b) * 2 + h, b * steps + t] = 1.0
    pout = np.zeros((2, tok, 2 * tok), np.float32)
    for h in range(2):
        pout[h] = (pin * (np.arange(2 * tok)[:, None] % 2 == h)).T
    return jnp.asarray(pin, dtype=BF16), jnp.asarray(pout, dtype=BF16)


def _ssm_call(z, bbt, cct, a8, h0, *, nb, seq, nbg):
    r = 2 * nbg
    steps = SSM_ROWS // r
    n = seq // steps
    ngrp = nb // nbg
    z3 = z.reshape(nb, seq, IN_COLS)
    pin, pout = _ssm_perms(nbg, steps)
    ys = jax.ShapeDtypeStruct((nb, seq, SSM_W), BF16)
    ublk = (nbg, steps, SSM_W)
    ucol = 6144 // SSM_W
    hblk = (2, r, 2 * HALF_STATE)
    c4 = lambda g, j: (0, 0, 0, 0)
    return pl.pallas_call(
        functools.partial(_ssm_kernel, rows_per_step=r, steps=steps),
        grid=(ngrp, n),
        in_specs=[pl.BlockSpec(ublk, lambda g, j: (g, j, ucol)),
                  pl.BlockSpec(ublk, lambda g, j: (g, n - 1 - j, ucol)),
                  pl.BlockSpec(pin.shape, lambda g, j: (0, 0)),
                  pl.BlockSpec(pout.shape, lambda g, j: (0, 0, 0)),
                  pl.BlockSpec(bbt.shape, c4),
                  pl.BlockSpec(cct.shape, c4),
                  pl.BlockSpec(a8.shape, c4),
                  pl.BlockSpec(hblk, lambda g, j: (0, g, 0))],
        out_specs=[pl.BlockSpec(ublk, lambda g, j: (g, j, 0)),
                   pl.BlockSpec(ublk, lambda g, j: (g, n - 1 - j, 0)),
                   pl.BlockSpec(hblk, lambda g, j: (0, g, 0))],
        out_shape=[ys, ys, jax.ShapeDtypeStruct(h0.shape, F32)],
        scratch_shapes=[pltpu.VMEM((2, SSM_ROWS, 2 * HALF_STATE), F32),
                        pltpu.VMEM(hblk, F32)],
        compiler_params=_cparams(("arbitrary", "arbitrary")),
        name="ssm_scan",
    )(z3, z3, pin, pout, bbt, cct, a8, h0)


def _merge_kernel(x_ref, mod_ref, zg_ref, zc_ref, zu_ref, cprev_ref, cnext_ref,
                  ya_ref, yf_ref, yb_ref, cw_ref, cb_ref, sd_ref, wglu_ref, bglu_ref,
                  wpc_ref, wpa_ref, wps_ref, wo_ref, o_ref, *, seq):
    i = pl.program_id(0)
    tm = x_ref.shape[0]
    zc = zc_ref[...].astype(F32)
    cb = zc[:, 0:CONV_W]
    prod = zc[:, CONV_W:2 * CONV_W] * zc[:, 2 * CONV_W:3 * CONV_W]
    cp = cprev_ref[...].astype(F32)
    cn = cnext_ref[...].astype(F32)
    halo_prev = cp[7:8, CONV_W:2 * CONV_W] * cp[7:8, 2 * CONV_W:3 * CONV_W]
    halo_next = cn[0:1, CONV_W:2 * CONV_W] * cn[0:1, 2 * CONV_W:3 * CONV_W]
    row = lax.broadcasted_iota(jnp.int32, (tm, CONV_W), 0)
    tpos = (i * tm + row) % seq
    prev = jnp.where(row == 0, halo_prev, pltpu.roll(prod, 1, 0))
    prev = jnp.where(tpos == 0, 0.0, prev)
    nxt = jnp.where(row == tm - 1, halo_next, pltpu.roll(prod, tm - 1, 0))
    nxt = jnp.where(tpos == seq - 1, 0.0, nxt)
    cw = cw_ref[...]
    y_conv = cb * (cw[0:1] * prev + cw[1:2] * prod + cw[2:3] * nxt + cb_ref[...])

    u = zu_ref[...].astype(F32)
    y = jax.nn.gelu(yf_ref[...].astype(F32) + yb_ref[...].astype(F32) + sd_ref[...] * u)
    y_ssm = y * jax.nn.sigmoid(
        jnp.dot(y.astype(BF16), wglu_ref[...], preferred_element_type=F32) + bglu_ref[...])

    zg = zg_ref[...].astype(F32)
    merged = (jax.nn.sigmoid(zg[:, 0:D_MODEL])
              * jnp.dot(y_conv.astype(BF16), wpc_ref[...], preferred_element_type=F32)
              + jax.nn.sigmoid(zg[:, D_MODEL:2 * D_MODEL])
              * jnp.dot(ya_ref[...], wpa_ref[...], preferred_element_type=F32)
              + jax.nn.sigmoid(zg[:, 2 * D_MODEL:3 * D_MODEL])
              * jnp.dot(y_ssm.astype(BF16), wps_ref[...], preferred_element_type=F32))
    gate1 = mod_ref[:, 2 * D_MODEL:3 * D_MODEL]
    o_ref[...] = x_ref[...] + gate1 * jnp.dot(merged.astype(BF16), wo_ref[...],
                                              preferred_element_type=F32)


def _merge_call(x, mod, z, y_attn, yf, yb, lp, *, layer, seq, tm):
    t = x.shape[0]
    n_m = t // tm
    per = max(seq // tm, 1)
    hb = tm // SUBLANES
    n_hb = t // SUBLANES
    mod_map = (lambda i: (0, 0, 0)) if tm >= seq else (lambda i: (i // per, 0, 0))
    y_spec = pl.BlockSpec((tm, SSM_W), lambda i: (i, 0))
    yf2 = yf.reshape(t, SSM_W)
    yb2 = yb.reshape(t, SSM_W)
    const2 = lambda i: (0, 0)
    lslab = lambda i: (layer, 0, 0)
    resident = dict(pipeline_mode=pl.Buffered(1))
    return pl.pallas_call(
        functools.partial(_merge_kernel, seq=seq),
        grid=(n_m,),
        in_specs=[pl.BlockSpec((tm, D_MODEL), lambda i: (i, 0)),
                  pl.BlockSpec((None, 1, 6 * D_MODEL), mod_map),
                  pl.BlockSpec((tm, 3 * D_MODEL), lambda i: (i, 0)),
                  pl.BlockSpec((tm, 3 * CONV_W), lambda i: (i, 3)),
                  pl.BlockSpec((tm, SSM_W), lambda i: (i, 6144 // SSM_W)),
                  pl.BlockSpec((SUBLANES, 3 * CONV_W), lambda i: (jnp.maximum(i * hb - 1, 0), 3)),
                  pl.BlockSpec((SUBLANES, 3 * CONV_W),
                               lambda i: (jnp.minimum((i + 1) * hb, n_hb - 1), 3)),
                  pl.BlockSpec((tm, D_MODEL), lambda i: (i, 0)),
                  y_spec, y_spec,
                  pl.BlockSpec((3, CONV_W), const2),
                  pl.BlockSpec((1, CONV_W), const2),
                  pl.BlockSpec((1, SSM_W), const2),
                  pl.BlockSpec((None, SSM_W, SSM_W), lslab, **resident),
                  pl.BlockSpec((1, SSM_W), const2),
                  pl.BlockSpec((None, CONV_W, D_MODEL), lslab, **resident),
                  pl.BlockSpec((None, D_MODEL, D_MODEL), lslab, **resident),
                  pl.BlockSpec((None, SSM_W, D_MODEL), lslab, **resident),
                  pl.BlockSpec((None, D_MODEL, D_MODEL), lslab, **resident)],
        out_specs=pl.BlockSpec((tm, D_MODEL), lambda i: (i, 0)),
        out_shape=jax.ShapeDtypeStruct((t, D_MODEL), F32),
        compiler_params=_cparams(("arbitrary",)),
        name="merge",
    )(x, mod, z, z, z, z, z, y_attn, yf2, yb2,
      lp['conv_w'], lp['conv_b'], lp['ssm_d'], lp['w_glu'], lp['b_glu'],
      lp['w_proj_conv'], lp['w_proj_attn'], lp['w_proj_ssm'], lp['w_out'])


def _ffn_kernel(x_ref, mod_ref, g_ref, wg_ref, wu_ref, wd_ref, fg_ref, o_ref, *, final, tf):
    x = x_ref[...]
    h = _mod_norm(x, g_ref[...], mod_ref[:, 3 * D_MODEL:4 * D_MODEL],
                  mod_ref[:, 4 * D_MODEL:5 * D_MODEL]).astype(BF16)
    acc = None
    bounds = list(range(0, D_FF, tf)) + [D_FF]
    for lo, hi in zip(bounds[:-1], bounds[1:]):
        cols = slice(lo, hi)
        g = jnp.dot(h, wg_ref[:, cols], preferred_element_type=F32)
        u = jnp.dot(h, wu_ref[:, cols], preferred_element_type=F32)
        part = jnp.dot((jax.nn.silu(g) * u).astype(BF16), wd_ref[cols, :], preferred_element_type=F32)
        acc = part if acc is None else acc + part
    xn = x + mod_ref[:, 5 * D_MODEL:6 * D_MODEL] * acc
    if final:
        ms = jnp.mean(xn * xn, axis=-1, keepdims=True)
        xn = xn * lax.rsqrt(ms + RMS_EPS) * fg_ref[...]
    o_ref[...] = xn


def _ffn_call(x, mod, g2, wg, wu, wd, fg, *, layer, seq, tm, tf, final):
    t = x.shape[0]
    per = max(seq // tm, 1)
    mod_map = (lambda i: (0, 0, 0)) if tm >= seq else (lambda i: (i // per, 0, 0))
    lslab = lambda i: (layer, 0, 0)
    resident = dict(pipeline_mode=pl.Buffered(1))
    return pl.pallas_call(
        functools.partial(_ffn_kernel, final=final, tf=tf),
        grid=(t // tm,),
        in_specs=[pl.BlockSpec((tm, D_MODEL), lambda i: (i, 0)),
                  pl.BlockSpec((None, 1, 6 * D_MODEL), mod_map),
                  pl.BlockSpec((1, D_MODEL), lambda i: (0, 0)),
                  pl.BlockSpec((None, D_MODEL, D_FF), lslab, **resident),
                  pl.BlockSpec((None, D_MODEL, D_FF), lslab, **resident),
                  pl.BlockSpec((None, D_FF, D_MODEL), lslab, **resident),
                  pl.BlockSpec((1, D_MODEL), lambda i: (0, 0))],
        out_specs=pl.BlockSpec((tm, D_MODEL), lambda i: (i, 0)),
        out_shape=jax.ShapeDtypeStruct((t, D_MODEL), F32),
        compiler_params=_cparams(("arbitrary",)),
        name="ffn",
    )(x, mod, g2, wg, wu, wd, fg)


def _state_rows(s):
    b = s.shape[0]
    return jnp.transpose(s, (1, 0, 2, 3)).reshape(2, 2 * b, HALF_STATE)


def _state_unrows(s, b):
    return jnp.transpose(s.reshape(2, b, SSM_GROUPS, SSM_STATE), (1, 0, 2, 3))


def kernel(x_prompt, x_sample, c, cache_k, cache_v, state_ssm_re, state_ssm_im, c_ctx, w_ada, b_ada, norm1_g, w_in, conv_w, conv_b, q_norm_g, k_norm_g, ssm_lambda_re, ssm_lambda_im, ssm_b_re, ssm_b_im, ssm_c_re, ssm_c_im, ssm_log_dt, ssm_d, w_glu, b_glu, w_proj_conv, w_proj_attn, w_proj_ssm, w_out, norm2_g, w_ffn_gate, w_ffn_up, w_ffn_down, final_norm_g):
    nb_c, seq_c, _ = x_prompt.shape
    nb_l, seq_l, _ = x_sample.shape
    past = cache_k.shape[2]

    cvec = jnp.zeros((8, D_MODEL), F32).at[0].set(c_ctx).at[1:1 + nb_l].set(c)
    mod_all = _ada_call(cvec, w_ada, b_ada)

    a_re, a_im, bb_re, bb_im = _disc_call(ssm_lambda_re, ssm_lambda_im, ssm_log_dt, ssm_b_re, ssm_b_im)
    bbt, cct, a8 = _ssm_weights(a_re, a_im, bb_re, bb_im, ssm_c_re, ssm_c_im)

    w_in_p = w_in.astype(BF16)
    wb = {k: v.astype(BF16) for k, v in dict(
        w_glu=w_glu, w_proj_conv=w_proj_conv, w_proj_attn=w_proj_attn, w_proj_ssm=w_proj_ssm,
        w_out=w_out, w_ffn_gate=w_ffn_gate, w_ffn_up=w_ffn_up, w_ffn_down=w_ffn_down).items()}

    bd = jnp.asarray(np.kron(np.eye(2 * LANES // HEAD_DIM, dtype=np.float32),
                             np.ones((HEAD_DIM, HEAD_DIM), np.float32)), dtype=BF16)
    tabs = _rope_tables(seq_l)
    fg = final_norm_g.reshape(1, D_MODEL)

    xc = x_prompt.reshape(nb_c * seq_c, D_MODEL)
    xl = x_sample.reshape(nb_l * seq_l, D_MODEL)
    h0_zero = jnp.zeros((2, 2 * nb_c, 2 * HALF_STATE), F32)
    ck = cache_k.reshape(nb_l, DEPTH, past, N_KV_HEADS * HEAD_DIM)
    cv = cache_v.reshape(nb_l, DEPTH, past, N_KV_HEADS * HEAD_DIM)
    srs, sis = [], []
    new_k = jnp.zeros((nb_c, DEPTH, seq_c, N_KV_HEADS * HEAD_DIM), F32)
    new_v = jnp.zeros((nb_c, DEPTH, seq_c, N_KV_HEADS * HEAD_DIM), F32)

    for l in range(DEPTH):
        lp = dict(conv_w=conv_w[l], conv_b=conv_b[l].reshape(1, CONV_W),
                  ssm_d=ssm_d[l].reshape(1, SSM_W), b_glu=b_glu[l].reshape(1, SSM_W),
                  w_glu=wb['w_glu'], w_proj_conv=wb['w_proj_conv'],
                  w_proj_attn=wb['w_proj_attn'], w_proj_ssm=wb['w_proj_ssm'],
                  w_out=wb['w_out'])
        g1 = norm1_g[l].reshape(1, D_MODEL)
        g2 = norm2_g[l].reshape(1, D_MODEL)
        qg = jnp.tile(q_norm_g[l], N_HEADS).reshape(1, D_MODEL)
        kg = jnp.tile(k_norm_g[l], N_KV_HEADS).reshape(1, 2 * LANES)
        mod_c = mod_all[l, 0:1].reshape(1, 1, 6 * D_MODEL)
        mod_l = mod_all[l, 1:1 + nb_l].reshape(nb_l, 1, 6 * D_MODEL)
        final = l == DEPTH - 1

        z, new_k, new_v = _inproj_call(xc, mod_c, g1, w_in_p, qg, kg, bd, None, (new_k, new_v),
                                       layer=l, seq=seq_c, tm=TOKEN_TILE)
        y_attn = _attn_call(z, None, None, layer=l, nb=nb_c, seq=seq_c, past=0,
                            **_attn_tiling(seq_c))
        yf, yb, ht = _ssm_call(z, bbt[l], cct[l], a8[l], h0_zero, nb=nb_c, seq=seq_c,
                               nbg=min(nb_c, SSM_MAX_BATCH))
        xc = _merge_call(xc, mod_c, z, y_attn, yf, yb, lp, layer=l, seq=seq_c, tm=TOKEN_TILE)
        xc = _ffn_call(xc, mod_c, g2, wb['w_ffn_gate'], wb['w_ffn_up'], wb['w_ffn_down'],
                       fg, layer=l, seq=seq_c, tm=TOKEN_TILE, tf=FFN_CHUNK, final=final)
        srs.append(_state_unrows(ht[:, :, :HALF_STATE], nb_c))
        sis.append(_state_unrows(ht[:, :, HALF_STATE:], nb_c))

        h0 = jnp.concatenate([_state_rows(state_ssm_re[:, l]), _state_rows(state_ssm_im[:, l])], axis=2)
        (z,) = _inproj_call(xl, mod_l, g1, w_in_p, qg, kg, bd, tabs, None,
                            layer=l, seq=seq_l, tm=TOKEN_TILE)
        y_attn = _attn_call(z, ck, cv, layer=l, nb=nb_l, seq=seq_l, past=past,
                            **_attn_tiling(seq_l))
        yf, yb, _ = _ssm_call(z, bbt[l], cct[l], a8[l], h0, nb=nb_l, seq=seq_l,
                              nbg=min(nb_l, SSM_MAX_BATCH))
        xl = _merge_call(xl, mod_l, z, y_attn, yf, yb, lp, layer=l, seq=seq_l, tm=TOKEN_TILE)
        xl = _ffn_call(xl, mod_l, g2, wb['w_ffn_gate'], wb['w_ffn_up'], wb['w_ffn_down'],
                       fg, layer=l, seq=seq_l, tm=TOKEN_TILE, tf=FFN_CHUNK, final=final)

    y_prompt = xc.reshape(nb_c, seq_c, D_MODEL)
    y_sample = xl.reshape(nb_l, seq_l, D_MODEL)
    kv_shape = (nb_c, DEPTH, seq_c, N_KV_HEADS, HEAD_DIM)
    return (y_prompt, y_sample, new_k.reshape(kv_shape), new_v.reshape(kv_shape),
            jnp.stack(srs, axis=1), jnp.stack(sis, axis=1))
```

```python
import functools
import math

import jax
import jax.numpy as jnp
import numpy as np
from jax import lax
from jax.experimental import pallas as pl
from jax.experimental.pallas import tpu as pltpu

F32 = jnp.float32
BF16 = jnp.bfloat16

D_MODEL = 1024
DEPTH = 4
GRID_W = 64
HEAD_DIM = 64
N_HEADS = 16
N_KV_HEADS = 4
ROPE_THETA = 10000.0
CONV_W = 512
SSM_W = 512
SSM_GROUP_CH = 16
SSM_GROUPS = 32
SSM_STATE = 64
D_FF = 2816
RMS_EPS = 1e-6
IN_COLS = 6656

LANES = 128
SUBLANES = 8
VMEM_LIMIT = 56 * 1024 * 1024

Z_TILE = 512
Z_SRC = (3584, 4096, 4608, 5120, 5632, 6144, 1536, 2048, 2560, 0, 512, 1024, 3072)
Z_Q_TILES = (6, 7)
Z_KV_TILE = 8
HALF_STATE = SSM_GROUPS // 2 * SSM_STATE
SSM_ROWS = 512
VT_ROWS = HEAD_DIM + 16

TOKEN_TILE = 512
FFN_CHUNK = 6 * 2 * LANES
FFN_TOKEN_TILE = 1024
SSM_MAX_BATCH = 16


def _attn_tiling(seq):
    if seq <= 2 * LANES:
        return dict(tq=seq, tok=seq, kchunk=SUBLANES, wave_units=1, barrier=False)
    return dict(tq=8 * LANES, tok=HEAD_DIM, kchunk=4 * SUBLANES, wave_units=2, barrier=True)


def _cparams(sem):
    return pltpu.CompilerParams(dimension_semantics=sem, vmem_limit_bytes=VMEM_LIMIT)


def _mod_norm(x, g, shift, scale):
    ms = jnp.mean(x * x, axis=-1, keepdims=True)
    y = x * lax.rsqrt(ms + RMS_EPS) * g
    return y * (1.0 + scale) + shift


def _ada_kernel(c_ref, w_ref, b_ref, o_ref):
    s = jax.nn.silu(c_ref[...])
    o_ref[...] = jnp.dot(s.astype(BF16), w_ref[...].astype(BF16),
                         preferred_element_type=F32) + b_ref[...]


def _ada_call(cvec, w_ada, b_ada):
    tn = 1536
    return pl.pallas_call(
        _ada_kernel,
        grid=(DEPTH, 6 * D_MODEL // tn),
        in_specs=[pl.BlockSpec((8, D_MODEL), lambda l, j: (0, 0)),
                  pl.BlockSpec((None, D_MODEL, tn), lambda l, j: (l, 0, j)),
                  pl.BlockSpec((None, 1, tn), lambda l, j: (l, 0, j))],
        out_specs=pl.BlockSpec((None, 8, tn), lambda l, j: (l, 0, j)),
        out_shape=jax.ShapeDtypeStruct((DEPTH, 8, 6 * D_MODEL), F32),
        compiler_params=_cparams(("arbitrary", "arbitrary")),
        name="ada_mod",
    )(cvec, w_ada, b_ada.reshape(DEPTH, 1, 6 * D_MODEL))


def _disc_kernel(lre_ref, lim_ref, ldt_ref, bre_ref, bim_ref,
                 are_ref, aim_ref, bbre_ref, bbim_ref):
    lre = lre_ref[...]
    lim = lim_ref[...]
    dt = jnp.exp(ldt_ref[...])
    mag = jnp.exp(lre * dt)
    a_re = mag * jnp.cos(lim * dt)
    a_im = mag * jnp.sin(lim * dt)
    den = lre * lre + lim * lim
    n_re = a_re - 1.0
    coef_re = (n_re * lre + a_im * lim) / den
    coef_im = (a_im * lre - n_re * lim) / den
    are_ref[...] = a_re
    aim_ref[...] = a_im
    bre = bre_ref[...]
    bim = bim_ref[...]
    bbre_ref[...] = coef_re * bre - coef_im * bim
    bbim_ref[...] = coef_re * bim + coef_im * bre


def _disc_call(lam_re, lam_im, log_dt, b_re, b_im):
    n = DEPTH * 2 * SSM_GROUPS
    lre = lam_re.reshape(n, 1, SSM_STATE)
    lim = lam_im.reshape(n, 1, SSM_STATE)
    ldt = log_dt.reshape(n, 1, 1)
    bre = jnp.swapaxes(b_re.reshape(n, SSM_STATE, SSM_GROUP_CH), 1, 2)
    bim = jnp.swapaxes(b_im.reshape(n, SSM_STATE, SSM_GROUP_CH), 1, 2)
    small = jax.ShapeDtypeStruct((n, 1, SSM_STATE), F32)
    big = jax.ShapeDtypeStruct((n, SSM_GROUP_CH, SSM_STATE), F32)
    return pl.pallas_call(
        _disc_kernel, out_shape=(small, small, big, big), name="ssm_disc",
    )(lre, lim, ldt, bre, bim)


def _ssm_weights(a_re, a_im, bb_re, bb_im, c_re, c_im):
    def bb_tiles(bb):
        return bb.reshape(DEPTH, 2, 2, 4, 4, SSM_GROUP_CH, SSM_STATE)
    bb = jnp.stack([bb_tiles(bb_re), bb_tiles(bb_im)], axis=2)
    sel = np.zeros((4, 8, 4), np.float32)
    for m in range(4):
        for j in range(4):
            sel[m, 4 * (m % 2) + j, j] = 1.0
    bbt = jnp.einsum('ldrhmjcp,mkj->ldrmhkcjp', bb, jnp.asarray(sel))
    bbt = bbt.reshape(DEPTH, 2, 8, 2 * LANES, 2 * LANES).astype(BF16)
    def c_tiles(c):
        return c.reshape(DEPTH, 2, 2, 2, 8, SSM_GROUP_CH, SSM_STATE)
    cc = jnp.stack([c_tiles(c_re), -c_tiles(c_im)], axis=2)
    eye = jnp.eye(8, dtype=F32)
    cct = jnp.einsum('ldrhmkop,kj->ldmrkphjo', cc, eye)
    cct = cct.reshape(DEPTH, 2, 2, 2 * 8 * SSM_STATE, 2 * LANES).astype(BF16)
    def a_rows(a):
        a = a.reshape(DEPTH, 2, 2, HALF_STATE)
        return jnp.tile(a, (1, 1, 4, 1))
    a8 = jnp.stack([a_rows(a_re), a_rows(a_im)], axis=2)
    return bbt, cct, a8


def _inproj_kernel(*refs, rope, emit_kv):
    it = iter(refs)
    x_ref, mod_ref, g_ref, w_ref, qg_ref, kg_ref, bd_ref = (next(it) for _ in range(7))
    cos = sup = sdn = None
    if rope:
        cos, sup, sdn = next(it)[...], next(it)[...], next(it)[...]
    if emit_kv:
        next(it), next(it)
    z_ref = next(it)
    kout_ref, vout_ref = (next(it), next(it)) if emit_kv else (None, None)

    h = _mod_norm(x_ref[...], g_ref[...], mod_ref[:, 0:D_MODEL], mod_ref[:, D_MODEL:2 * D_MODEL])
    h = h.astype(BF16)
    bd = bd_ref[...]
    for c, src in enumerate(Z_SRC):
        zt = jnp.dot(h, w_ref[:, src:src + Z_TILE], preferred_element_type=F32)
        if c in Z_Q_TILES:
            qcols = slice((c - Z_Q_TILES[0]) * Z_TILE, (c - Z_Q_TILES[0] + 1) * Z_TILE)
            zt = zt * lax.rsqrt(_head_ssq(zt, bd) * (1.0 / HEAD_DIM) + RMS_EPS) * qg_ref[:, qcols]
            if rope:
                zt = _rope(zt, cos, sup, sdn)
            zt = zt * (HEAD_DIM ** -0.5 * math.log2(math.e))
        elif c == Z_KV_TILE:
            k = zt[:, :2 * LANES]
            v = zt[:, 2 * LANES:]
            kn = k * lax.rsqrt(_head_ssq(k, bd) * (1.0 / HEAD_DIM) + RMS_EPS) * kg_ref[...]
            if emit_kv:
                kout_ref[...] = kn.reshape(kout_ref.shape)
                vout_ref[...] = v.reshape(vout_ref.shape)
            if rope:
                kn = _rope(kn, cos, sup, sdn)
            zt = jnp.concatenate([kn, v], axis=1)
        z_ref[:, c * Z_TILE:(c + 1) * Z_TILE] = zt.astype(BF16)


def _inproj_call(x, mod, g1, w_in, qg, kg, bd, tabs, caches, *, layer, seq, tm):
    t = x.shape[0]
    per = max(seq // tm, 1)
    mod_map = (lambda i: (0, 0, 0)) if tm >= seq else (lambda i: (i // per, 0, 0))
    in_specs = [pl.BlockSpec((tm, D_MODEL), lambda i: (i, 0)),
                pl.BlockSpec((None, 1, 6 * D_MODEL), mod_map),
                pl.BlockSpec((1, D_MODEL), lambda i: (0, 0)),
                pl.BlockSpec((None, D_MODEL, IN_COLS), lambda i: (layer, 0, 0),
                             pipeline_mode=pl.Buffered(1)),
                pl.BlockSpec((1, D_MODEL), lambda i: (0, 0)),
                pl.BlockSpec((1, 2 * LANES), lambda i: (0, 0)),
                pl.BlockSpec((2 * LANES, 2 * LANES), lambda i: (0, 0))]
    args = [x, mod, g1, w_in, qg, kg, bd]
    if tabs is not None:
        assert tm <= seq
        in_specs += [pl.BlockSpec((tm, LANES), lambda i: (i % per, 0))] * 3
        args += list(tabs)
    out_specs = [pl.BlockSpec((tm, IN_COLS), lambda i: (i, 0))]
    out_shape = [jax.ShapeDtypeStruct((t, IN_COLS), BF16)]
    aliases = {}
    if caches is not None:
        assert tm % seq == 0
        aliases = {len(args): 1, len(args) + 1: 2}
        in_specs += [pl.BlockSpec(memory_space=pl.ANY)] * 2
        args += list(caches)
        out_specs += [pl.BlockSpec((tm // seq, None, seq, 2 * LANES), lambda i: (i, layer, 0, 0))] * 2
        out_shape += [jax.ShapeDtypeStruct(a.shape, a.dtype) for a in caches]
    return pl.pallas_call(
        functools.partial(_inproj_kernel, rope=tabs is not None, emit_kv=caches is not None),
        grid=(t // tm,),
        in_specs=in_specs, out_specs=out_specs, out_shape=out_shape, input_output_aliases=aliases,
        compiler_params=_cparams(("arbitrary",)),
        name="in_proj",
    )(*args)


def _head_ssq(x, bd):
    parts = []
    for t in range(x.shape[1] // (2 * LANES)):
        xs = x[:, 2 * LANES * t:2 * LANES * (t + 1)]
        parts.append(jnp.dot((xs * xs).astype(BF16), bd, preferred_element_type=F32))
    return parts[0] if len(parts) == 1 else jnp.concatenate(parts, axis=1)


def _rope(x, cos, sin_up, sin_dn):
    w = x.shape[1]
    rep = w // LANES
    cos = jnp.tile(cos, (1, rep))
    sin_up = jnp.tile(sin_up, (1, rep))
    sin_dn = jnp.tile(sin_dn, (1, rep))
    return x * cos + pltpu.roll(x, w - 16, 1) * sin_up + pltpu.roll(x, 16, 1) * sin_dn


def _dup_heads(x, lane_lo):
    xr = pltpu.roll(x, HEAD_DIM, 1)
    return jnp.where(lane_lo, x, xr), jnp.where(lane_lo, xr, x)


def _attn_kernel(*refs, seq, past, tok, kchunk, wave_units, barrier):
    it = iter(refs)
    q_ref, kv_ref = next(it), next(it)
    kc_ref = vc_ref = None
    if past:
        kc_ref, vc_ref = next(it), next(it)
    o_ref = next(it)
    k2_scr, vt_scr, qst_scr, s_scr, m_scr, p_scr, ot_scr, o_scr = (next(it) for _ in range(8))

    qi = pl.program_id(1)

    @pl.when(qi == 0)
    def _prep():
        kv = kv_ref[...].astype(F32)
        kn = kv[:, :2 * LANES]
        v = kv[:, 2 * LANES:]
        lane_lo = lax.broadcasted_iota(jnp.int32, (seq, LANES), 1) < HEAD_DIM
        for t in range(2):
            ka, kb = _dup_heads(kn[:, LANES * t:LANES * (t + 1)], lane_lo)
            k2_scr[2 * t, 0:seq, :] = ka.astype(BF16)
            k2_scr[2 * t + 1, 0:seq, :] = kb.astype(BF16)
        vt = v.T.astype(BF16)
        for kvh in range(N_KV_HEADS):
            vt_scr[kvh * VT_ROWS:kvh * VT_ROWS + HEAD_DIM, 0:seq] = vt[kvh * HEAD_DIM:(kvh + 1) * HEAD_DIM]
            vt_scr[kvh * VT_ROWS + HEAD_DIM:(kvh + 1) * VT_ROWS, :] = jnp.ones(
                (VT_ROWS - HEAD_DIM, vt_scr.shape[1]), BF16)
        if past:
            kc = kc_ref[...]
            lane_lo_p = lax.broadcasted_iota(jnp.int32, (past, LANES), 1) < HEAD_DIM
            for t in range(2):
                ka, kb = _dup_heads(kc[:, LANES * t:LANES * (t + 1)], lane_lo_p)
                k2_scr[2 * t, seq:seq + past, :] = ka.astype(BF16)
                k2_scr[2 * t + 1, seq:seq + past, :] = kb.astype(BF16)
            vct = vc_ref[...].T.astype(BF16)
            for kvh in range(N_KV_HEADS):
                vt_scr[kvh * VT_ROWS:kvh * VT_ROWS + HEAD_DIM, seq:seq + past] = (
                    vct[kvh * HEAD_DIM:(kvh + 1) * HEAD_DIM])

    qn = q_ref[...].astype(F32)
    tq = qn.shape[0]

    n_tb = tq // tok
    n_units = N_KV_HEADS * n_tb
    keys = k2_scr.shape[1]
    n_chunks = keys // kchunk
    lane_lo = lax.broadcasted_iota(jnp.int32, (tok, LANES), 1) < HEAD_DIM
    for kvh in range(N_KV_HEADS):
        for tb in range(n_tb):
            stack = []
            for t in (2 * kvh, 2 * kvh + 1):
                qt = qn[tb * tok:(tb + 1) * tok, LANES * t:LANES * (t + 1)]
                stack += [jnp.where(lane_lo, qt, 0.0), jnp.where(lane_lo, 0.0, qt)]
            qst_scr[kvh * n_tb + tb] = jnp.concatenate(stack, axis=0).astype(BF16)

    def kv_of(u):
        return u // n_tb if isinstance(u, int) else lax.shift_right_logical(u, n_tb.bit_length() - 1)

    def scores(u, slot):
        s = lax.dot_general(k2_scr[kv_of(u)], qst_scr[u], (((1,), (1,)), ((), ())),
                            preferred_element_type=F32)
        s_scr[slot] = s
        m_run = s[0:kchunk]
        for c in range(1, n_chunks):
            m_run = jnp.maximum(m_run, s[c * kchunk:(c + 1) * kchunk])
        m_scr[slot] = jnp.broadcast_to(jnp.max(m_run, axis=0, keepdims=True), m_scr.shape[1:])

    def softmax(u, slot):
        del u
        m = jnp.broadcast_to(m_scr[slot, 0:1, :], (kchunk, 4 * tok))
        for c in range(n_chunks):
            p = jnp.exp2(s_scr[slot, c * kchunk:(c + 1) * kchunk, :] - m)
            p_scr[slot, c * kchunk:(c + 1) * kchunk, :] = p.astype(BF16)

    def values(u, slot):
        r0 = kv_of(u) * VT_ROWS
        v_rows = (slice(r0, r0 + VT_ROWS) if isinstance(r0, int)
                  else pl.ds(pl.multiple_of(r0, VT_ROWS), VT_ROWS))
        ot_scr[slot] = jnp.dot(vt_scr[v_rows, :], p_scr[slot], preferred_element_type=F32)

    def finish(u, slot):
        ot = ot_scr[slot]
        ot = ot[0:HEAD_DIM] * (1.0 / ot[HEAD_DIM:HEAD_DIM + 1])
        for pair in range(2):
            if tok % LANES == 0:
                w = jnp.concatenate([ot[:, 2 * pair * tok:(2 * pair + 1) * tok],
                                     ot[:, (2 * pair + 1) * tok:(2 * pair + 2) * tok]], axis=0)
            else:
                both = ot[:, LANES * pair:LANES * (pair + 1)]
                w = jnp.concatenate([both, pltpu.roll(both, tok, 1)], axis=0)
            o_scr[u, pair] = w.T[0:tok].astype(BF16)

    n_waves = n_units // wave_units
    assert n_tb & (n_tb - 1) == 0 and n_units % wave_units == 0
    assert n_waves >= 4

    def wave(g, slot_set, stage):
        for w in range(wave_units):
            stage(g * wave_units + w, slot_set * wave_units + w)

    def step_barrier():
        if barrier:
            pl.delay(1)

    wave(0, 0, scores)
    wave(0, 0, softmax)
    wave(1, 1, scores)
    for w in range(wave_units):
        ot_scr[2 * wave_units + w] = jnp.ones(ot_scr.shape[1:], F32)

    def step(c, cur):
        step_barrier()
        wave(jnp.minimum(c + 1, n_waves - 1), (cur + 1) % 3, scores)
        wave(c, cur, softmax)
        wave(c - 1, (cur + 2) % 3, values)
        wave(jnp.maximum(c - 2, 0), (cur + 1) % 3, finish)

    def pipelined(i, carry):
        for k in range(3):
            step(3 * i + 1 + k, (1 + k) % 3)
        return carry

    rolled = (n_waves - 1) // 3
    lax.fori_loop(0, rolled, pipelined, 0)
    for c in range(3 * rolled + 1, n_waves):
        step(c, c % 3)
    step_barrier()
    wave(n_waves - 1, (n_waves - 1) % 3, values)
    wave(n_waves - 2, (n_waves - 2) % 3, finish)
    step_barrier()
    wave(n_waves - 1, (n_waves - 1) % 3, finish)

    for kvh in range(N_KV_HEADS):
        for tb in range(n_tb):
            u = kvh * n_tb + tb
            for pair, t in enumerate((2 * kvh, 2 * kvh + 1)):
                o_ref[tb * tok:(tb + 1) * tok, LANES * t:LANES * (t + 1)] = o_scr[u, pair]


def _attn_call(z, kc, vc, *, layer, nb, seq, tq, past, tok, kchunk, wave_units, barrier):
    nq = seq // tq
    keys = seq + past
    n_units = N_KV_HEADS * (tq // tok)
    q_col = 3072 // D_MODEL
    kv_col = 4096 // 512
    in_specs = [pl.BlockSpec((tq, D_MODEL), lambda b, i: (b * nq + i, q_col)),
                pl.BlockSpec((seq, 512), lambda b, i: (b, kv_col),
                             pipeline_mode=pl.Buffered(1 if nq > 1 else 2))]
    args = [z, z]
    if past:
        in_specs += [pl.BlockSpec((None, None, past, 2 * LANES), lambda b, i: (b, layer, 0, 0))] * 2
        args += [kc, vc]
    return pl.pallas_call(
        functools.partial(_attn_kernel, seq=seq, past=past,
                          tok=tok, kchunk=kchunk, wave_units=wave_units, barrier=barrier),
        grid=(nb, nq),
        in_specs=in_specs,
        out_specs=pl.BlockSpec((tq, D_MODEL), lambda b, i: (b * nq + i, 0)),
        out_shape=jax.ShapeDtypeStruct((nb * seq, D_MODEL), BF16),
        scratch_shapes=[pltpu.VMEM((N_KV_HEADS, keys, LANES), BF16),
                        pltpu.VMEM((N_KV_HEADS * VT_ROWS, keys), BF16),
                        pltpu.VMEM((n_units, 4 * tok, LANES), BF16),
                        pltpu.VMEM((3 * wave_units, keys, 4 * tok), F32),
                        pltpu.VMEM((3 * wave_units, SUBLANES, 4 * tok), F32),
                        pltpu.VMEM((3 * wave_units, keys, 4 * tok), BF16),
                        pltpu.VMEM((3 * wave_units, VT_ROWS, 4 * tok), F32),
                        pltpu.VMEM((n_units, 2, tok, LANES), BF16)],
        compiler_params=_cparams(("arbitrary", "arbitrary")),
        name="attention",
    )(*args)


def _rope_tables(length):
    f = HEAD_DIM // 4
    inv = ROPE_THETA ** (-np.arange(f, dtype=np.float32) / f)
    pos = np.arange(length)
    row = (pos // GRID_W).astype(np.float32)
    col = (pos % GRID_W).astype(np.float32)
    ang_r = row[:, None] * inv[None, :]
    ang_c = col[:, None] * inv[None, :]
    z = np.zeros_like(ang_r)
    cos64 = np.concatenate([np.cos(ang_r), np.cos(ang_r), np.cos(ang_c), np.cos(ang_c)], axis=1)
    up64 = np.concatenate([-np.sin(ang_r), z, -np.sin(ang_c), z], axis=1)
    dn64 = np.concatenate([z, np.sin(ang_r), z, np.sin(ang_c)], axis=1)
    tile2 = lambda a: jnp.asarray(np.tile(a, (1, 2)), dtype=F32)
    return tile2(cos64), tile2(up64), tile2(dn64)


def _ssm_kernel(uf_ref, ub_ref, pin_ref, pout_ref, bbt_ref, cct_ref, a_ref, h0_ref,
                yf_ref, yb_ref, ht_ref, x_scr, hc_scr, *, rows_per_step, steps):
    j = pl.program_id(1)
    r = rows_per_step
    rows = r * steps
    nbg = r // 2
    tok = nbg * steps

    @pl.when(j == 0)
    def _():
        hc_scr[...] = h0_ref[...]

    first_half = (lax.broadcasted_iota(jnp.int32, (rows, LANES), 0) & 1) == 0
    width = SUBLANES * HALF_STATE // r
    u_refs = (uf_ref, ub_ref)
    y_refs = (yf_ref, yb_ref)

    def project_in(d):
        u_bt = u_refs[d][...].reshape(tok, SSM_W)
        u = jnp.dot(pin_ref[...], u_bt, preferred_element_type=F32)
        kcat = []
        for q in range(2):
            u0 = u[:, LANES * q:LANES * (q + 1)]
            u1 = u[:, 2 * LANES + LANES * q:2 * LANES + LANES * (q + 1)]
            kcat.append(jnp.concatenate([jnp.where(first_half, u0, 0.0),
                                         jnp.where(first_half, 0.0, u1)], axis=1).astype(BF16))
        for n in range(8):
            x_scr[d, :, 2 * LANES * n:2 * LANES * (n + 1)] = jnp.dot(
                kcat[(n % 4) // 2], bbt_ref[d, n], preferred_element_type=F32)

    def scan(d):
        for p in range(HALF_STATE // width):
            re_cols = slice(p * width, (p + 1) * width)
            im_cols = slice(HALF_STATE + p * width, HALF_STATE + (p + 1) * width)
            reps = r // SUBLANES
            a_re = jnp.tile(a_ref[d, 0, :, re_cols], (reps, 1))
            a_im = jnp.tile(a_ref[d, 1, :, re_cols], (reps, 1))
            h_re = hc_scr[d, :, re_cols]
            h_im = hc_scr[d, :, im_cols]
            for kk in range(steps):
                k = kk if d == 0 else steps - 1 - kk
                srows = slice(k * r, (k + 1) * r)
                n_re = a_re * h_re - a_im * h_im + x_scr[d, srows, re_cols]
                n_im = a_re * h_im + a_im * h_re + x_scr[d, srows, im_cols]
                x_scr[d, srows, re_cols] = n_re
                x_scr[d, srows, im_cols] = n_im
                h_re, h_im = n_re, n_im
            hc_scr[d, :, re_cols] = h_re
            hc_scr[d, :, im_cols] = h_im

    def project_out(d):
        yy = []
        for m in range(2):
            hk = jnp.concatenate(
                [x_scr[d, :, 512 * m:512 * (m + 1)],
                 x_scr[d, :, HALF_STATE + 512 * m:HALF_STATE + 512 * (m + 1)]], axis=1).astype(BF16)
            yy.append(jnp.dot(hk, cct_ref[d, m], preferred_element_type=F32))
        for hh in range(2):
            ysel = jnp.concatenate([yy[0][:, LANES * hh:LANES * (hh + 1)],
                                    yy[1][:, LANES * hh:LANES * (hh + 1)]], axis=1).astype(BF16)
            y_bt = jnp.dot(pout_ref[hh], ysel, preferred_element_type=F32)
            y_refs[d][:, :, 2 * LANES * hh:2 * LANES * (hh + 1)] = (
                y_bt.reshape(nbg, steps, 2 * LANES).astype(BF16))

    for d in range(2):
        project_in(d)
        scan(d)
        project_out(d)

    @pl.when(j == pl.num_programs(1) - 1)
    def _():
        ht_ref[...] = hc_scr[...]


def _ssm_perms(nbg, steps):
    tok = nbg * steps
    pin = np.zeros((2 * tok, tok), np.float32)
    for t in range(steps):
        for b in range(nbg):
            for h in range(2):
                pin[(t * nbg + b) * 2 + h, b * steps + t] = 1.0
    pout = np.zeros((2, tok, 2 * tok), np.float32)
    for h in range(2):
        pout[h] = (pin * (np.arange(2 * tok)[:, None] % 2 == h)).T
    return jnp.asarray(pin, dtype=BF16), jnp.asarray(pout, dtype=BF16)


def _ssm_call(z, bbt, cct, a8, h0, *, nb, seq, nbg):
    r = 2 * nbg
    steps = SSM_ROWS // r
    n = seq // steps
    ngrp = nb // nbg
    z3 = z.reshape(nb, seq, IN_COLS)
    pin, pout = _ssm_perms(nbg, steps)
    ys = jax.ShapeDtypeStruct((nb, seq, SSM_W), BF16)
    ublk = (nbg, steps, SSM_W)
    ucol = 6144 // SSM_W
    hblk = (2, r, 2 * HALF_STATE)
    c4 = lambda g, j: (0, 0, 0, 0)
    return pl.pallas_call(
        functools.partial(_ssm_kernel, rows_per_step=r, steps=steps),
        grid=(ngrp, n),
        in_specs=[pl.BlockSpec(ublk, lambda g, j: (g, j, ucol)),
                  pl.BlockSpec(ublk, lambda g, j: (g, n - 1 - j, ucol)),
                  pl.BlockSpec(pin.shape, lambda g, j: (0, 0)),
                  pl.BlockSpec(pout.shape, lambda g, j: (0, 0, 0)),
                  pl.BlockSpec(bbt.shape, c4),
                  pl.BlockSpec(cct.shape, c4),
                  pl.BlockSpec(a8.shape, c4),
                  pl.BlockSpec(hblk, lambda g, j: (0, g, 0))],
        out_specs=[pl.BlockSpec(ublk, lambda g, j: (g, j, 0)),
                   pl.BlockSpec(ublk, lambda g, j: (g, n - 1 - j, 0)),
                   pl.BlockSpec(hblk, lambda g, j: (0, g, 0))],
        out_shape=[ys, ys, jax.ShapeDtypeStruct(h0.shape, F32)],
        scratch_shapes=[pltpu.VMEM((2, SSM_ROWS, 2 * HALF_STATE), F32),
                        pltpu.VMEM(hblk, F32)],
        compiler_params=_cparams(("arbitrary", "arbitrary")),
        name="ssm_scan",
    )(z3, z3, pin, pout, bbt, cct, a8, h0)


def _merge_kernel(x_ref, mod_ref, zg_ref, zc_ref, zu_ref, cprev_ref, cnext_ref,
                  ya_ref, yf_ref, yb_ref, cw_ref, cb_ref, sd_ref, wglu_ref, bglu_ref,
                  wpc_ref, wpa_ref, wps_ref, wo_ref, o_ref, *, seq):
    i = pl.program_id(0)
    tm = x_ref.shape[0]
    zc = zc_ref[...].astype(F32)
    cb = zc[:, 0:CONV_W]
    prod = zc[:, CONV_W:2 * CONV_W] * zc[:, 2 * CONV_W:3 * CONV_W]
    cp = cprev_ref[...].astype(F32)
    cn = cnext_ref[...].astype(F32)
    halo_prev = cp[7:8, CONV_W:2 * CONV_W] * cp[7:8, 2 * CONV_W:3 * CONV_W]
    halo_next = cn[0:1, CONV_W:2 * CONV_W] * cn[0:1, 2 * CONV_W:3 * CONV_W]
    row = lax.broadcasted_iota(jnp.int32, (tm, CONV_W), 0)
    tpos = (i * tm + row) % seq
    prev = jnp.where(row == 0, halo_prev, pltpu.roll(prod, 1, 0))
    prev = jnp.where(tpos == 0, 0.0, prev)
    nxt = jnp.where(row == tm - 1, halo_next, pltpu.roll(prod, tm - 1, 0))
    nxt = jnp.where(tpos == seq - 1, 0.0, nxt)
    cw = cw_ref[...]
    y_conv = cb * (cw[0:1] * prev + cw[1:2] * prod + cw[2:3] * nxt + cb_ref[...])

    u = zu_ref[...].astype(F32)
    y = jax.nn.gelu(yf_ref[...].astype(F32) + yb_ref[...].astype(F32) + sd_ref[...] * u)
    y_ssm = y * jax.nn.sigmoid(
        jnp.dot(y.astype(BF16), wglu_ref[...], preferred_element_type=F32) + bglu_ref[...])

    zg = zg_ref[...].astype(F32)
    merged = (jax.nn.sigmoid(zg[:, 0:D_MODEL])
              * jnp.dot(y_conv.astype(BF16), wpc_ref[...], preferred_element_type=F32)
              + jax.nn.sigmoid(zg[:, D_MODEL:2 * D_MODEL])
              * jnp.dot(ya_ref[...], wpa_ref[...], preferred_element_type=F32)
              + jax.nn.sigmoid(zg[:, 2 * D_MODEL:3 * D_MODEL])
              * jnp.dot(y_ssm.astype(BF16), wps_ref[...], preferred_element_type=F32))
    gate1 = mod_ref[:, 2 * D_MODEL:3 * D_MODEL]
    o_ref[...] = x_ref[...] + gate1 * jnp.dot(merged.astype(BF16), wo_ref[...],
                                              preferred_element_type=F32)


def _merge_call(x, mod, z, y_attn, yf, yb, lp, *, layer, seq, tm):
    t = x.shape[0]
    n_m = t // tm
    per = max(seq // tm, 1)
    hb = tm // SUBLANES
    n_hb = t // SUBLANES
    mod_map = (lambda i: (0, 0, 0)) if tm >= seq else (lambda i: (i // per, 0, 0))
    y_spec = pl.BlockSpec((tm, SSM_W), lambda i: (i, 0))
    yf2 = yf.reshape(t, SSM_W)
    yb2 = yb.reshape(t, SSM_W)
    const2 = lambda i: (0, 0)
    lslab = lambda i: (layer, 0, 0)
    resident = dict(pipeline_mode=pl.Buffered(1))
    return pl.pallas_call(
        functools.partial(_merge_kernel, seq=seq),
        grid=(n_m,),
        in_specs=[pl.BlockSpec((tm, D_MODEL), lambda i: (i, 0)),
                  pl.BlockSpec((None, 1, 6 * D_MODEL), mod_map),
                  pl.BlockSpec((tm, 3 * D_MODEL), lambda i: (i, 0)),
                  pl.BlockSpec((tm, 3 * CONV_W), lambda i: (i, 3)),
                  pl.BlockSpec((tm, SSM_W), lambda i: (i, 6144 // SSM_W)),
                  pl.BlockSpec((SUBLANES, 3 * CONV_W), lambda i: (jnp.maximum(i * hb - 1, 0), 3)),
                  pl.BlockSpec((SUBLANES, 3 * CONV_W),
                               lambda i: (jnp.minimum((i + 1) * hb, n_hb - 1), 3)),
                  pl.BlockSpec((tm, D_MODEL), lambda i: (i, 0)),
                  y_spec, y_spec,
                  pl.BlockSpec((3, CONV_W), const2),
                  pl.BlockSpec((1, CONV_W), const2),
                  pl.BlockSpec((1, SSM_W), const2),
                  pl.BlockSpec((None, SSM_W, SSM_W), lslab, **resident),
                  pl.BlockSpec((1, SSM_W), const2),
                  pl.BlockSpec((None, CONV_W, D_MODEL), lslab, **resident),
                  pl.BlockSpec((None, D_MODEL, D_MODEL), lslab, **resident),
                  pl.BlockSpec((None, SSM_W, D_MODEL), lslab, **resident),
                  pl.BlockSpec((None, D_MODEL, D_MODEL), lslab, **resident)],
        out_specs=pl.BlockSpec((tm, D_MODEL), lambda i: (i, 0)),
        out_shape=jax.ShapeDtypeStruct((t, D_MODEL), F32),
        compiler_params=_cparams(("arbitrary",)),
        name="merge",
    )(x, mod, z, z, z, z, z, y_attn, yf2, yb2,
      lp['conv_w'], lp['conv_b'], lp['ssm_d'], lp['w_glu'], lp['b_glu'],
      lp['w_proj_conv'], lp['w_proj_attn'], lp['w_proj_ssm'], lp['w_out'])


def _ffn_kernel(x_ref, mod_ref, g_ref, wg_ref, wu_ref, wd_ref, fg_ref, o_ref, *, final, tf):
    x = x_ref[...]
    h = _mod_norm(x, g_ref[...], mod_ref[:, 3 * D_MODEL:4 * D_MODEL],
                  mod_ref[:, 4 * D_MODEL:5 * D_MODEL]).astype(BF16)
    acc = None
    bounds = list(range(0, D_FF, tf)) + [D_FF]
    for lo, hi in zip(bounds[:-1], bounds[1:]):
        cols = slice(lo, hi)
        g = jnp.dot(h, wg_ref[:, cols], preferred_element_type=F32)
        u = jnp.dot(h, wu_ref[:, cols], preferred_element_type=F32)
        part = jnp.dot((jax.nn.silu(g) * u).astype(BF16), wd_ref[cols, :], preferred_element_type=F32)
        acc = part if acc is None else acc + part
    xn = x + mod_ref[:, 5 * D_MODEL:6 * D_MODEL] * acc
    if final:
        ms = jnp.mean(xn * xn, axis=-1, keepdims=True)
        xn = xn * lax.rsqrt(ms + RMS_EPS) * fg_ref[...]
    o_ref[...] = xn


def _ffn_call(x, mod, g2, wg, wu, wd, fg, *, layer, seq, tm, tf, final):
    t = x.shape[0]
    per = max(seq // tm, 1)
    mod_map = (lambda i: (0, 0, 0)) if tm >= seq else (lambda i: (i // per, 0, 0))
    lslab = lambda i: (layer, 0, 0)
    resident = dict(pipeline_mode=pl.Buffered(1))
    return pl.pallas_call(
        functools.partial(_ffn_kernel, final=final, tf=tf),
        grid=(t // tm,),
        in_specs=[pl.BlockSpec((tm, D_MODEL), lambda i: (i, 0)),
                  pl.BlockSpec((None, 1, 6 * D_MODEL), mod_map),
                  pl.BlockSpec((1, D_MODEL), lambda i: (0, 0)),
                  pl.BlockSpec((None, D_MODEL, D_FF), lslab, **resident),
                  pl.BlockSpec((None, D_MODEL, D_FF), lslab, **resident),
                  pl.BlockSpec((None, D_FF, D_MODEL), lslab, **resident),
                  pl.BlockSpec((1, D_MODEL), lambda i: (0, 0))],
        out_specs=pl.BlockSpec((tm, D_MODEL), lambda i: (i, 0)),
        out_shape=jax.ShapeDtypeStruct((t, D_MODEL), F32),
        compiler_params=_cparams(("arbitrary",)),
        name="ffn",
    )(x, mod, g2, wg, wu, wd, fg)


def _state_rows(s):
    b = s.shape[0]
    return jnp.transpose(s, (1, 0, 2, 3)).reshape(2, 2 * b, HALF_STATE)


def _state_unrows(s, b):
    return jnp.transpose(s.reshape(2, b, SSM_GROUPS, SSM_STATE), (1, 0, 2, 3))


def kernel(x_prompt, x_sample, c, cache_k, cache_v, state_ssm_re, state_ssm_im, c_ctx, w_ada, b_ada, norm1_g, w_in, conv_w, conv_b, q_norm_g, k_norm_g, ssm_lambda_re, ssm_lambda_im, ssm_b_re, ssm_b_im, ssm_c_re, ssm_c_im, ssm_log_dt, ssm_d, w_glu, b_glu, w_proj_conv, w_proj_attn, w_proj_ssm, w_out, norm2_g, w_ffn_gate, w_ffn_up, w_ffn_down, final_norm_g):
    nb_c, seq_c, _ = x_prompt.shape
    nb_l, seq_l, _ = x_sample.shape
    past = cache_k.shape[2]

    cvec = jnp.zeros((8, D_MODEL), F32).at[0].set(c_ctx).at[1:1 + nb_l].set(c)
    mod_all = _ada_call(cvec, w_ada, b_ada)

    a_re, a_im, bb_re, bb_im = _disc_call(ssm_lambda_re, ssm_lambda_im, ssm_log_dt, ssm_b_re, ssm_b_im)
    bbt, cct, a8 = _ssm_weights(a_re, a_im, bb_re, bb_im, ssm_c_re, ssm_c_im)

    w_in_p = w_in.astype(BF16)
    wb = {k: v.astype(BF16) for k, v in dict(
        w_glu=w_glu, w_proj_conv=w_proj_conv, w_proj_attn=w_proj_attn, w_proj_ssm=w_proj_ssm,
        w_out=w_out, w_ffn_gate=w_ffn_gate, w_ffn_up=w_ffn_up, w_ffn_down=w_ffn_down).items()}

    bd = jnp.asarray(np.kron(np.eye(2 * LANES // HEAD_DIM, dtype=np.float32),
                             np.ones((HEAD_DIM, HEAD_DIM), np.float32)), dtype=BF16)
    tabs = _rope_tables(seq_l)
    fg = final_norm_g.reshape(1, D_MODEL)

    xc = x_prompt.reshape(nb_c * seq_c, D_MODEL)
    xl = x_sample.reshape(nb_l * seq_l, D_MODEL)
    h0_zero = jnp.zeros((2, 2 * nb_c, 2 * HALF_STATE), F32)
    ck = cache_k.reshape(nb_l, DEPTH, past, N_KV_HEADS * HEAD_DIM)
    cv = cache_v.reshape(nb_l, DEPTH, past, N_KV_HEADS * HEAD_DIM)
    srs, sis = [], []
    new_k = jnp.zeros((nb_c, DEPTH, seq_c, N_KV_HEADS * HEAD_DIM), F32)
    new_v = jnp.zeros((nb_c, DEPTH, seq_c, N_KV_HEADS * HEAD_DIM), F32)

    for l in range(DEPTH):
        lp = dict(conv_w=conv_w[l], conv_b=conv_b[l].reshape(1, CONV_W),
                  ssm_d=ssm_d[l].reshape(1, SSM_W), b_glu=b_glu[l].reshape(1, SSM_W),
                  w_glu=wb['w_glu'], w_proj_conv=wb['w_proj_conv'],
                  w_proj_attn=wb['w_proj_attn'], w_proj_ssm=wb['w_proj_ssm'],
                  w_out=wb['w_out'])
        g1 = norm1_g[l].reshape(1, D_MODEL)
        g2 = norm2_g[l].reshape(1, D_MODEL)
        qg = jnp.tile(q_norm_g[l], N_HEADS).reshape(1, D_MODEL)
        kg = jnp.tile(k_norm_g[l], N_KV_HEADS).reshape(1, 2 * LANES)
        mod_c = mod_all[l, 0:1].reshape(1, 1, 6 * D_MODEL)
        mod_l = mod_all[l, 1:1 + nb_l].reshape(nb_l, 1, 6 * D_MODEL)
        final = l == DEPTH - 1

        z, new_k, new_v = _inproj_call(xc, mod_c, g1, w_in_p, qg, kg, bd, None, (new_k, new_v),
                                       layer=l, seq=seq_c, tm=TOKEN_TILE)
        y_attn = _attn_call(z, None, None, layer=l, nb=nb_c, seq=seq_c, past=0,
                            **_attn_tiling(seq_c))
        yf, yb, ht = _ssm_call(z, bbt[l], cct[l], a8[l], h0_zero, nb=nb_c, seq=seq_c,
                               nbg=min(nb_c, SSM_MAX_BATCH))
        xc = _merge_call(xc, mod_c, z, y_attn, yf, yb, lp, layer=l, seq=seq_c, tm=TOKEN_TILE)
        xc = _ffn_call(xc, mod_c, g2, wb['w_ffn_gate'], wb['w_ffn_up'], wb['w_ffn_down'],
                       fg, layer=l, seq=seq_c, tm=FFN_TOKEN_TILE, tf=FFN_CHUNK, final=final)
        srs.append(_state_unrows(ht[:, :, :HALF_STATE], nb_c))
        sis.append(_state_unrows(ht[:, :, HALF_STATE:], nb_c))

        h0 = jnp.concatenate([_state_rows(state_ssm_re[:, l]), _state_rows(state_ssm_im[:, l])], axis=2)
        (z,) = _inproj_call(xl, mod_l, g1, w_in_p, qg, kg, bd, tabs, None,
                            layer=l, seq=seq_l, tm=TOKEN_TILE)
        y_attn = _attn_call(z, ck, cv, layer=l, nb=nb_l, seq=seq_l, past=past,
                            **_attn_tiling(seq_l))
        yf, yb, _ = _ssm_call(z, bbt[l], cct[l], a8[l], h0, nb=nb_l, seq=seq_l,
                              nbg=min(nb_l, SSM_MAX_BATCH))
        xl = _merge_call(xl, mod_l, z, y_attn, yf, yb, lp, layer=l, seq=seq_l, tm=TOKEN_TILE)
        xl = _ffn_call(xl, mod_l, g2, wb['w_ffn_gate'], wb['w_ffn_up'], wb['w_ffn_down'],
                       fg, layer=l, seq=seq_l, tm=FFN_TOKEN_TILE, tf=FFN_CHUNK, final=final)

    y_prompt = xc.reshape(nb_c, seq_c, D_MODEL)
    y_sample = xl.reshape(nb_l, seq_l, D_MODEL)
    kv_shape = (nb_c, DEPTH, seq_c, N_KV_HEADS, HEAD_DIM)
    return (y_prompt, y_sample, new_k.reshape(kv_shape), new_v.reshape(kv_shape),
            jnp.stack(srs, axis=1), jnp.stack(sis, axis=1))
```

```python
import functools
import math

import jax
import jax.numpy as jnp
import numpy as np
from jax import lax
from jax.experimental import pallas as pl
from jax.experimental.pallas import tpu as pltpu

F32 = jnp.float32
BF16 = jnp.bfloat16

D_MODEL = 1024
DEPTH = 4
GRID_W = 64
HEAD_DIM = 64
N_HEADS = 16
N_KV_HEADS = 4
ROPE_THETA = 10000.0
CONV_W = 512
SSM_W = 512
SSM_GROUP_CH = 16
SSM_GROUPS = 32
SSM_STATE = 64
D_FF = 2816
RMS_EPS = 1e-6
IN_COLS = 6656

LANES = 128
SUBLANES = 8
VMEM_LIMIT = 56 * 1024 * 1024

Z_TILE = 512
Z_SRC = (3584, 4096, 4608, 5120, 5632, 6144, 1536, 2048, 2560, 0, 512, 1024, 3072)
Z_Q_TILES = (6, 7)
Z_KV_TILE = 8
HALF_STATE = SSM_GROUPS // 2 * SSM_STATE
SSM_ROWS = 512
VT_ROWS = HEAD_DIM + 16

TOKEN_TILE = 512
FFN_CHUNK = 6 * 2 * LANES
FFN_TOKEN_TILE = 1024
MERGE_CHUNK = 4 * LANES
SSM_MAX_BATCH = 16


def _attn_tiling(seq):
    if seq <= 2 * LANES:
        return dict(tq=seq, tok=seq, kchunk=SUBLANES, wave_units=1, barrier=False)
    return dict(tq=8 * LANES, tok=HEAD_DIM, kchunk=4 * SUBLANES, wave_units=2, barrier=True)


def _cparams(sem):
    return pltpu.CompilerParams(dimension_semantics=sem, vmem_limit_bytes=VMEM_LIMIT)


def _mod_norm(x, g, shift, scale):
    ms = jnp.mean(x * x, axis=-1, keepdims=True)
    y = x * lax.rsqrt(ms + RMS_EPS) * g
    return y * (1.0 + scale) + shift


def _ada_kernel(c_ref, w_ref, b_ref, o_ref):
    s = jax.nn.silu(c_ref[...])
    o_ref[...] = jnp.dot(s.astype(BF16), w_ref[...].astype(BF16),
                         preferred_element_type=F32) + b_ref[...]


def _ada_call(cvec, w_ada, b_ada):
    tn = 1536
    return pl.pallas_call(
        _ada_kernel,
        grid=(DEPTH, 6 * D_MODEL // tn),
        in_specs=[pl.BlockSpec((8, D_MODEL), lambda l, j: (0, 0)),
                  pl.BlockSpec((None, D_MODEL, tn), lambda l, j: (l, 0, j)),
                  pl.BlockSpec((None, 1, tn), lambda l, j: (l, 0, j))],
        out_specs=pl.BlockSpec((None, 8, tn), lambda l, j: (l, 0, j)),
        out_shape=jax.ShapeDtypeStruct((DEPTH, 8, 6 * D_MODEL), F32),
        compiler_params=_cparams(("arbitrary", "arbitrary")),
        name="ada_mod",
    )(cvec, w_ada, b_ada.reshape(DEPTH, 1, 6 * D_MODEL))


def _disc_kernel(lre_ref, lim_ref, ldt_ref, bre_ref, bim_ref,
                 are_ref, aim_ref, bbre_ref, bbim_ref):
    lre = lre_ref[...]
    lim = lim_ref[...]
    dt = jnp.exp(ldt_ref[...])
    mag = jnp.exp(lre * dt)
    a_re = mag * jnp.cos(lim * dt)
    a_im = mag * jnp.sin(lim * dt)
    den = lre * lre + lim * lim
    n_re = a_re - 1.0
    coef_re = (n_re * lre + a_im * lim) / den
    coef_im = (a_im * lre - n_re * lim) / den
    are_ref[...] = a_re
    aim_ref[...] = a_im
    bre = bre_ref[...]
    bim = bim_ref[...]
    bbre_ref[...] = coef_re * bre - coef_im * bim
    bbim_ref[...] = coef_re * bim + coef_im * bre


def _disc_call(lam_re, lam_im, log_dt, b_re, b_im):
    n = DEPTH * 2 * SSM_GROUPS
    lre = lam_re.reshape(n, 1, SSM_STATE)
    lim = lam_im.reshape(n, 1, SSM_STATE)
    ldt = log_dt.reshape(n, 1, 1)
    bre = jnp.swapaxes(b_re.reshape(n, SSM_STATE, SSM_GROUP_CH), 1, 2)
    bim = jnp.swapaxes(b_im.reshape(n, SSM_STATE, SSM_GROUP_CH), 1, 2)
    small = jax.ShapeDtypeStruct((n, 1, SSM_STATE), F32)
    big = jax.ShapeDtypeStruct((n, SSM_GROUP_CH, SSM_STATE), F32)
    return pl.pallas_call(
        _disc_kernel, out_shape=(small, small, big, big), name="ssm_disc",
    )(lre, lim, ldt, bre, bim)


def _ssm_weights(a_re, a_im, bb_re, bb_im, c_re, c_im):
    def bb_tiles(bb):
        return bb.reshape(DEPTH, 2, 2, 4, 4, SSM_GROUP_CH, SSM_STATE)
    bb = jnp.stack([bb_tiles(bb_re), bb_tiles(bb_im)], axis=2)
    sel = np.zeros((4, 8, 4), np.float32)
    for m in range(4):
        for j in range(4):
            sel[m, 4 * (m % 2) + j, j] = 1.0
    bbt = jnp.einsum('ldrhmjcp,mkj->ldrmhkcjp', bb, jnp.asarray(sel))
    bbt = bbt.reshape(DEPTH, 2, 8, 2 * LANES, 2 * LANES).astype(BF16)
    def c_tiles(c):
        return c.reshape(DEPTH, 2, 2, 2, 8, SSM_GROUP_CH, SSM_STATE)
    cc = jnp.stack([c_tiles(c_re), -c_tiles(c_im)], axis=2)
    eye = jnp.eye(8, dtype=F32)
    cct = jnp.einsum('ldrhmkop,kj->ldmrkphjo', cc, eye)
    cct = cct.reshape(DEPTH, 2, 2, 2 * 8 * SSM_STATE, 2 * LANES).astype(BF16)
    def a_rows(a):
        a = a.reshape(DEPTH, 2, 2, HALF_STATE)
        return jnp.tile(a, (1, 1, 4, 1))
    a8 = jnp.stack([a_rows(a_re), a_rows(a_im)], axis=2)
    return bbt, cct, a8


def _inproj_kernel(*refs, rope, emit_kv):
    it = iter(refs)
    x_ref, mod_ref, g_ref, w_ref, qg_ref, kg_ref, bd_ref = (next(it) for _ in range(7))
    cos = sup = sdn = None
    if rope:
        cos, sup, sdn = next(it)[...], next(it)[...], next(it)[...]
    if emit_kv:
        next(it), next(it)
    z_ref = next(it)
    kout_ref, vout_ref = (next(it), next(it)) if emit_kv else (None, None)

    h = _mod_norm(x_ref[...], g_ref[...], mod_ref[:, 0:D_MODEL], mod_ref[:, D_MODEL:2 * D_MODEL])
    h = h.astype(BF16)
    bd = bd_ref[...]
    for c, src in enumerate(Z_SRC):
        zt = jnp.dot(h, w_ref[:, src:src + Z_TILE], preferred_element_type=F32)
        if c in Z_Q_TILES:
            qcols = slice((c - Z_Q_TILES[0]) * Z_TILE, (c - Z_Q_TILES[0] + 1) * Z_TILE)
            zt = zt * lax.rsqrt(_head_ssq(zt, bd) * (1.0 / HEAD_DIM) + RMS_EPS) * qg_ref[:, qcols]
            if rope:
                zt = _rope(zt, cos, sup, sdn)
            zt = zt * (HEAD_DIM ** -0.5 * math.log2(math.e))
        elif c == Z_KV_TILE:
            k = zt[:, :2 * LANES]
            v = zt[:, 2 * LANES:]
            kn = k * lax.rsqrt(_head_ssq(k, bd) * (1.0 / HEAD_DIM) + RMS_EPS) * kg_ref[...]
            if emit_kv:
                kout_ref[...] = kn.reshape(kout_ref.shape)
                vout_ref[...] = v.reshape(vout_ref.shape)
            if rope:
                kn = _rope(kn, cos, sup, sdn)
            zt = jnp.concatenate([kn, v], axis=1)
        z_ref[:, c * Z_TILE:(c + 1) * Z_TILE] = zt.astype(BF16)


def _inproj_call(x, mod, g1, w_in, qg, kg, bd, tabs, caches, *, layer, seq, tm):
    t = x.shape[0]
    per = max(seq // tm, 1)
    mod_map = (lambda i: (0, 0, 0)) if tm >= seq else (lambda i: (i // per, 0, 0))
    in_specs = [pl.BlockSpec((tm, D_MODEL), lambda i: (i, 0)),
                pl.BlockSpec((None, 1, 6 * D_MODEL), mod_map),
                pl.BlockSpec((1, D_MODEL), lambda i: (0, 0)),
                pl.BlockSpec((None, D_MODEL, IN_COLS), lambda i: (layer, 0, 0),
                             pipeline_mode=pl.Buffered(1)),
                pl.BlockSpec((1, D_MODEL), lambda i: (0, 0)),
                pl.BlockSpec((1, 2 * LANES), lambda i: (0, 0)),
                pl.BlockSpec((2 * LANES, 2 * LANES), lambda i: (0, 0))]
    args = [x, mod, g1, w_in, qg, kg, bd]
    if tabs is not None:
        assert tm <= seq
        in_specs += [pl.BlockSpec((tm, LANES), lambda i: (i % per, 0))] * 3
        args += list(tabs)
    out_specs = [pl.BlockSpec((tm, IN_COLS), lambda i: (i, 0))]
    out_shape = [jax.ShapeDtypeStruct((t, IN_COLS), BF16)]
    aliases = {}
    if caches is not None:
        assert tm % seq == 0
        aliases = {len(args): 1, len(args) + 1: 2}
        in_specs += [pl.BlockSpec(memory_space=pl.ANY)] * 2
        args += list(caches)
        out_specs += [pl.BlockSpec((tm // seq, None, seq, 2 * LANES), lambda i: (i, layer, 0, 0))] * 2
        out_shape += [jax.ShapeDtypeStruct(a.shape, a.dtype) for a in caches]
    return pl.pallas_call(
        functools.partial(_inproj_kernel, rope=tabs is not None, emit_kv=caches is not None),
        grid=(t // tm,),
        in_specs=in_specs, out_specs=out_specs, out_shape=out_shape, input_output_aliases=aliases,
        compiler_params=_cparams(("arbitrary",)),
        name="in_proj",
    )(*args)


def _head_ssq(x, bd):
    parts = []
    for t in range(x.shape[1] // (2 * LANES)):
        xs = x[:, 2 * LANES * t:2 * LANES * (t + 1)]
        parts.append(jnp.dot((xs * xs).astype(BF16), bd, preferred_element_type=F32))
    return parts[0] if len(parts) == 1 else jnp.concatenate(parts, axis=1)


def _rope(x, cos, sin_up, sin_dn):
    w = x.shape[1]
    rep = w // LANES
    cos = jnp.tile(cos, (1, rep))
    sin_up = jnp.tile(sin_up, (1, rep))
    sin_dn = jnp.tile(sin_dn, (1, rep))
    return x * cos + pltpu.roll(x, w - 16, 1) * sin_up + pltpu.roll(x, 16, 1) * sin_dn


def _dup_heads(x, lane_lo):
    xr = pltpu.roll(x, HEAD_DIM, 1)
    return jnp.where(lane_lo, x, xr), jnp.where(lane_lo, xr, x)


def _attn_kernel(*refs, seq, past, tok, kchunk, wave_units, barrier):
    it = iter(refs)
    q_ref, kv_ref = next(it), next(it)
    kc_ref = vc_ref = None
    if past:
        kc_ref, vc_ref = next(it), next(it)
    o_ref = next(it)
    k2_scr, vt_scr, qst_scr, s_scr, m_scr, p_scr, ot_scr, o_scr = (next(it) for _ in range(8))

    qi = pl.program_id(1)

    @pl.when(qi == 0)
    def _prep():
        kv = kv_ref[...].astype(F32)
        kn = kv[:, :2 * LANES]
        v = kv[:, 2 * LANES:]
        lane_lo = lax.broadcasted_iota(jnp.int32, (seq, LANES), 1) < HEAD_DIM
        for t in range(2):
            ka, kb = _dup_heads(kn[:, LANES * t:LANES * (t + 1)], lane_lo)
            k2_scr[2 * t, 0:seq, :] = ka.astype(BF16)
            k2_scr[2 * t + 1, 0:seq, :] = kb.astype(BF16)
        vt = v.T.astype(BF16)
        for kvh in range(N_KV_HEADS):
            vt_scr[kvh * VT_ROWS:kvh * VT_ROWS + HEAD_DIM, 0:seq] = vt[kvh * HEAD_DIM:(kvh + 1) * HEAD_DIM]
            vt_scr[kvh * VT_ROWS + HEAD_DIM:(kvh + 1) * VT_ROWS, :] = jnp.ones(
                (VT_ROWS - HEAD_DIM, vt_scr.shape[1]), BF16)
        if past:
            kc = kc_ref[...]
            lane_lo_p = lax.broadcasted_iota(jnp.int32, (past, LANES), 1) < HEAD_DIM
            for t in range(2):
                ka, kb = _dup_heads(kc[:, LANES * t:LANES * (t + 1)], lane_lo_p)
                k2_scr[2 * t, seq:seq + past, :] = ka.astype(BF16)
                k2_scr[2 * t + 1, seq:seq + past, :] = kb.astype(BF16)
            vct = vc_ref[...].T.astype(BF16)
            for kvh in range(N_KV_HEADS):
                vt_scr[kvh * VT_ROWS:kvh * VT_ROWS + HEAD_DIM, seq:seq + past] = (
                    vct[kvh * HEAD_DIM:(kvh + 1) * HEAD_DIM])

    qn = q_ref[...].astype(F32)
    tq = qn.shape[0]

    n_tb = tq // tok
    n_units = N_KV_HEADS * n_tb
    keys = k2_scr.shape[1]
    n_chunks = keys // kchunk
    lane_lo = lax.broadcasted_iota(jnp.int32, (tok, LANES), 1) < HEAD_DIM
    for kvh in range(N_KV_HEADS):
        for tb in range(n_tb):
            stack = []
            for t in (2 * kvh, 2 * kvh + 1):
                qt = qn[tb * tok:(tb + 1) * tok, LANES * t:LANES * (t + 1)]
                stack += [jnp.where(lane_lo, qt, 0.0), jnp.where(lane_lo, 0.0, qt)]
            qst_scr[kvh * n_tb + tb] = jnp.concatenate(stack, axis=0).astype(BF16)

    def kv_of(u):
        return u // n_tb if isinstance(u, int) else lax.shift_right_logical(u, n_tb.bit_length() - 1)

    def scores(u, slot):
        s = lax.dot_general(k2_scr[kv_of(u)], qst_scr[u], (((1,), (1,)), ((), ())),
                            preferred_element_type=F32)
        s_scr[slot] = s
        m_run = s[0:kchunk]
        for c in range(1, n_chunks):
            m_run = jnp.maximum(m_run, s[c * kchunk:(c + 1) * kchunk])
        m_scr[slot] = jnp.broadcast_to(jnp.max(m_run, axis=0, keepdims=True), m_scr.shape[1:])

    def softmax(u, slot):
        del u
        m = jnp.broadcast_to(m_scr[slot, 0:1, :], (kchunk, 4 * tok))
        for c in range(n_chunks):
            p = jnp.exp2(s_scr[slot, c * kchunk:(c + 1) * kchunk, :] - m)
            p_scr[slot, c * kchunk:(c + 1) * kchunk, :] = p.astype(BF16)

    def values(u, slot):
        r0 = kv_of(u) * VT_ROWS
        v_rows = (slice(r0, r0 + VT_ROWS) if isinstance(r0, int)
                  else pl.ds(pl.multiple_of(r0, VT_ROWS), VT_ROWS))
        ot_scr[slot] = jnp.dot(vt_scr[v_rows, :], p_scr[slot], preferred_element_type=F32)

    def finish(u, slot):
        ot = ot_scr[slot]
        ot = ot[0:HEAD_DIM] * (1.0 / ot[HEAD_DIM:HEAD_DIM + 1])
        for pair in range(2):
            if tok % LANES == 0:
                w = jnp.concatenate([ot[:, 2 * pair * tok:(2 * pair + 1) * tok],
                                     ot[:, (2 * pair + 1) * tok:(2 * pair + 2) * tok]], axis=0)
            else:
                both = ot[:, LANES * pair:LANES * (pair + 1)]
                w = jnp.concatenate([both, pltpu.roll(both, tok, 1)], axis=0)
            o_scr[u, pair] = w.T[0:tok].astype(BF16)

    n_waves = n_units // wave_units
    assert n_tb & (n_tb - 1) == 0 and n_units % wave_units == 0
    assert n_waves >= 4

    def wave(g, slot_set, stage):
        for w in range(wave_units):
            stage(g * wave_units + w, slot_set * wave_units + w)

    def step_barrier():
        if barrier:
            pl.delay(1)

    wave(0, 0, scores)
    wave(0, 0, softmax)
    wave(1, 1, scores)
    for w in range(wave_units):
        ot_scr[2 * wave_units + w] = jnp.ones(ot_scr.shape[1:], F32)

    def step(c, cur):
        step_barrier()
        wave(jnp.minimum(c + 1, n_waves - 1), (cur + 1) % 3, scores)
        wave(c, cur, softmax)
        wave(c - 1, (cur + 2) % 3, values)
        wave(jnp.maximum(c - 2, 0), (cur + 1) % 3, finish)

    def pipelined(i, carry):
        for k in range(3):
            step(3 * i + 1 + k, (1 + k) % 3)
        return carry

    rolled = (n_waves - 1) // 3
    lax.fori_loop(0, rolled, pipelined, 0)
    for c in range(3 * rolled + 1, n_waves):
        step(c, c % 3)
    step_barrier()
    wave(n_waves - 1, (n_waves - 1) % 3, values)
    wave(n_waves - 2, (n_waves - 2) % 3, finish)
    step_barrier()
    wave(n_waves - 1, (n_waves - 1) % 3, finish)

    for kvh in range(N_KV_HEADS):
        for tb in range(n_tb):
            u = kvh * n_tb + tb
            for pair, t in enumerate((2 * kvh, 2 * kvh + 1)):
                o_ref[tb * tok:(tb + 1) * tok, LANES * t:LANES * (t + 1)] = o_scr[u, pair]


def _attn_call(z, kc, vc, *, layer, nb, seq, tq, past, tok, kchunk, wave_units, barrier):
    nq = seq // tq
    keys = seq + past
    n_units = N_KV_HEADS * (tq // tok)
    q_col = 3072 // D_MODEL
    kv_col = 4096 // 512
    in_specs = [pl.BlockSpec((tq, D_MODEL), lambda b, i: (b * nq + i, q_col)),
                pl.BlockSpec((seq, 512), lambda b, i: (b, kv_col),
                             pipeline_mode=pl.Buffered(1 if nq > 1 else 2))]
    args = [z, z]
    if past:
        in_specs += [pl.BlockSpec((None, None, past, 2 * LANES), lambda b, i: (b, layer, 0, 0))] * 2
        args += [kc, vc]
    return pl.pallas_call(
        functools.partial(_attn_kernel, seq=seq, past=past,
                          tok=tok, kchunk=kchunk, wave_units=wave_units, barrier=barrier),
        grid=(nb, nq),
        in_specs=in_specs,
        out_specs=pl.BlockSpec((tq, D_MODEL), lambda b, i: (b * nq + i, 0)),
        out_shape=jax.ShapeDtypeStruct((nb * seq, D_MODEL), BF16),
        scratch_shapes=[pltpu.VMEM((N_KV_HEADS, keys, LANES), BF16),
                        pltpu.VMEM((N_KV_HEADS * VT_ROWS, keys), BF16),
                        pltpu.VMEM((n_units, 4 * tok, LANES), BF16),
                        pltpu.VMEM((3 * wave_units, keys, 4 * tok), F32),
                        pltpu.VMEM((3 * wave_units, SUBLANES, 4 * tok), F32),
                        pltpu.VMEM((3 * wave_units, keys, 4 * tok), BF16),
                        pltpu.VMEM((3 * wave_units, VT_ROWS, 4 * tok), F32),
                        pltpu.VMEM((n_units, 2, tok, LANES), BF16)],
        compiler_params=_cparams(("arbitrary", "arbitrary")),
        name="attention",
    )(*args)


def _rope_tables(length):
    f = HEAD_DIM // 4
    inv = ROPE_THETA ** (-np.arange(f, dtype=np.float32) / f)
    pos = np.arange(length)
    row = (pos // GRID_W).astype(np.float32)
    col = (pos % GRID_W).astype(np.float32)
    ang_r = row[:, None] * inv[None, :]
    ang_c = col[:, None] * inv[None, :]
    z = np.zeros_like(ang_r)
    cos64 = np.concatenate([np.cos(ang_r), np.cos(ang_r), np.cos(ang_c), np.cos(ang_c)], axis=1)
    up64 = np.concatenate([-np.sin(ang_r), z, -np.sin(ang_c), z], axis=1)
    dn64 = np.concatenate([z, np.sin(ang_r), z, np.sin(ang_c)], axis=1)
    tile2 = lambda a: jnp.asarray(np.tile(a, (1, 2)), dtype=F32)
    return tile2(cos64), tile2(up64), tile2(dn64)


def _ssm_kernel(uf_ref, ub_ref, pin_ref, pout_ref, bbt_ref, cct_ref, a_ref, h0_ref,
                yf_ref, yb_ref, ht_ref, x_scr, hc_scr, *, rows_per_step, steps):
    j = pl.program_id(1)
    r = rows_per_step
    rows = r * steps
    nbg = r // 2
    tok = nbg * steps

    @pl.when(j == 0)
    def _():
        hc_scr[...] = h0_ref[...]

    first_half = (lax.broadcasted_iota(jnp.int32, (rows, LANES), 0) & 1) == 0
    width = SUBLANES * HALF_STATE // r
    u_refs = (uf_ref, ub_ref)
    y_refs = (yf_ref, yb_ref)

    def project_in(d):
        u_bt = u_refs[d][...].reshape(tok, SSM_W)
        u = jnp.dot(pin_ref[...], u_bt, preferred_element_type=F32)
        kcat = []
        for q in range(2):
            u0 = u[:, LANES * q:LANES * (q + 1)]
            u1 = u[:, 2 * LANES + LANES * q:2 * LANES + LANES * (q + 1)]
            kcat.append(jnp.concatenate([jnp.where(first_half, u0, 0.0),
                                         jnp.where(first_half, 0.0, u1)], axis=1).astype(BF16))
        for n in range(8):
            x_scr[d, :, 2 * LANES * n:2 * LANES * (n + 1)] = jnp.dot(
                kcat[(n % 4) // 2], bbt_ref[d, n], preferred_element_type=F32)

    def scan(d):
        for p in range(HALF_STATE // width):
            re_cols = slice(p * width, (p + 1) * width)
            im_cols = slice(HALF_STATE + p * width, HALF_STATE + (p + 1) * width)
            reps = r // SUBLANES
            a_re = jnp.tile(a_ref[d, 0, :, re_cols], (reps, 1))
            a_im = jnp.tile(a_ref[d, 1, :, re_cols], (reps, 1))
            h_re = hc_scr[d, :, re_cols]
            h_im = hc_scr[d, :, im_cols]
            for kk in range(steps):
                k = kk if d == 0 else steps - 1 - kk
                srows = slice(k * r, (k + 1) * r)
                n_re = a_re * h_re - a_im * h_im + x_scr[d, srows, re_cols]
                n_im = a_re * h_im + a_im * h_re + x_scr[d, srows, im_cols]
                x_scr[d, srows, re_cols] = n_re
                x_scr[d, srows, im_cols] = n_im
                h_re, h_im = n_re, n_im
            hc_scr[d, :, re_cols] = h_re
            hc_scr[d, :, im_cols] = h_im

    def project_out(d):
        yy = []
        for m in range(2):
            hk = jnp.concatenate(
                [x_scr[d, :, 512 * m:512 * (m + 1)],
                 x_scr[d, :, HALF_STATE + 512 * m:HALF_STATE + 512 * (m + 1)]], axis=1).astype(BF16)
            yy.append(jnp.dot(hk, cct_ref[d, m], preferred_element_type=F32))
        for hh in range(2):
            ysel = jnp.concatenate([yy[0][:, LANES * hh:LANES * (hh + 1)],
                                    yy[1][:, LANES * hh:LANES * (hh + 1)]], axis=1).astype(BF16)
            y_bt = jnp.dot(pout_ref[hh], ysel, preferred_element_type=F32)
            y_refs[d][:, :, 2 * LANES * hh:2 * LANES * (hh + 1)] = (
                y_bt.reshape(nbg, steps, 2 * LANES).astype(BF16))

    for d in range(2):
        project_in(d)
        scan(d)
        project_out(d)

    @pl.when(j == pl.num_programs(1) - 1)
    def _():
        ht_ref[...] = hc_scr[...]


def _ssm_perms(nbg, steps):
    tok = nbg * steps
    pin = np.zeros((2 * tok, tok), np.float32)
    for t in range(steps):
        for b in range(nbg):
            for h in range(2):
                pin[(t * nbg + b) * 2 + h, b * steps + t] = 1.0
    pout = np.zeros((2, tok, 2 * tok), np.float32)
    for h in range(2):
        pout[h] = (pin * (np.arange(2 * tok)[:, None] % 2 == h)).T
    return jnp.asarray(pin, dtype=BF16), jnp.asarray(pout, dtype=BF16)


def _ssm_call(z, bbt, cct, a8, h0, *, nb, seq, nbg):
    r = 2 * nbg
    steps = SSM_ROWS // r
    n = seq // steps
    ngrp = nb // nbg
    z3 = z.reshape(nb, seq, IN_COLS)
    pin, pout = _ssm_perms(nbg, steps)
    ys = jax.ShapeDtypeStruct((nb, seq, SSM_W), BF16)
    ublk = (nbg, steps, SSM_W)
    ucol = 6144 // SSM_W
    hblk = (2, r, 2 * HALF_STATE)
    c4 = lambda g, j: (0, 0, 0, 0)
    return pl.pallas_call(
        functools.partial(_ssm_kernel, rows_per_step=r, steps=steps),
        grid=(ngrp, n),
        in_specs=[pl.BlockSpec(ublk, lambda g, j: (g, j, ucol)),
                  pl.BlockSpec(ublk, lambda g, j: (g, n - 1 - j, ucol)),
                  pl.BlockSpec(pin.shape, lambda g, j: (0, 0)),
                  pl.BlockSpec(pout.shape, lambda g, j: (0, 0, 0)),
                  pl.BlockSpec(bbt.shape, c4),
                  pl.BlockSpec(cct.shape, c4),
                  pl.BlockSpec(a8.shape, c4),
                  pl.BlockSpec(hblk, lambda g, j: (0, g, 0))],
        out_specs=[pl.BlockSpec(ublk, lambda g, j: (g, j, 0)),
                   pl.BlockSpec(ublk, lambda g, j: (g, n - 1 - j, 0)),
                   pl.BlockSpec(hblk, lambda g, j: (0, g, 0))],
        out_shape=[ys, ys, jax.ShapeDtypeStruct(h0.shape, F32)],
        scratch_shapes=[pltpu.VMEM((2, SSM_ROWS, 2 * HALF_STATE), F32),
                        pltpu.VMEM(hblk, F32)],
        compiler_params=_cparams(("arbitrary", "arbitrary")),
        name="ssm_scan",
    )(z3, z3, pin, pout, bbt, cct, a8, h0)


def _merge_kernel(x_ref, mod_ref, zg_ref, zc_ref, zu_ref, cprev_ref, cnext_ref,
                  ya_ref, yf_ref, yb_ref, cw_ref, cb_ref, sd_ref, wglu_ref, bglu_ref,
                  wpc_ref, wpa_ref, wps_ref, wo_ref, o_ref, *, seq):
    i = pl.program_id(0)
    tm = x_ref.shape[0]
    zc = zc_ref[...].astype(F32)
    cb = zc[:, 0:CONV_W]
    prod = zc[:, CONV_W:2 * CONV_W] * zc[:, 2 * CONV_W:3 * CONV_W]
    cp = cprev_ref[...].astype(F32)
    cn = cnext_ref[...].astype(F32)
    halo_prev = cp[7:8, CONV_W:2 * CONV_W] * cp[7:8, 2 * CONV_W:3 * CONV_W]
    halo_next = cn[0:1, CONV_W:2 * CONV_W] * cn[0:1, 2 * CONV_W:3 * CONV_W]
    row = lax.broadcasted_iota(jnp.int32, (tm, CONV_W), 0)
    tpos = (i * tm + row) % seq
    prev = jnp.where(row == 0, halo_prev, pltpu.roll(prod, 1, 0))
    prev = jnp.where(tpos == 0, 0.0, prev)
    nxt = jnp.where(row == tm - 1, halo_next, pltpu.roll(prod, tm - 1, 0))
    nxt = jnp.where(tpos == seq - 1, 0.0, nxt)
    cw = cw_ref[...]
    y_conv = cb * (cw[0:1] * prev + cw[1:2] * prod + cw[2:3] * nxt + cb_ref[...])

    u = zu_ref[...].astype(F32)
    y = jax.nn.gelu(yf_ref[...].astype(F32) + yb_ref[...].astype(F32) + sd_ref[...] * u)
    y_ssm = y * jax.nn.sigmoid(
        jnp.dot(y.astype(BF16), wglu_ref[...], preferred_element_type=F32) + bglu_ref[...])

    y_conv_b = y_conv.astype(BF16)
    y_ssm_b = y_ssm.astype(BF16)
    ya = ya_ref[...]
    acc = None
    for c0 in range(0, D_MODEL, MERGE_CHUNK):
        cols = slice(c0, c0 + MERGE_CHUNK)

        def gate(branch, cols=cols, c0=c0):
            return jax.nn.sigmoid(
                zg_ref[:, branch * D_MODEL + c0:branch * D_MODEL + c0 + MERGE_CHUNK].astype(F32))

        merged = (gate(0) * jnp.dot(y_conv_b, wpc_ref[:, cols], preferred_element_type=F32)
                  + gate(1) * jnp.dot(ya, wpa_ref[:, cols], preferred_element_type=F32)
                  + gate(2) * jnp.dot(y_ssm_b, wps_ref[:, cols], preferred_element_type=F32))
        part = jnp.dot(merged.astype(BF16), wo_ref[cols, :], preferred_element_type=F32)
        acc = part if acc is None else acc + part
    gate1 = mod_ref[:, 2 * D_MODEL:3 * D_MODEL]
    o_ref[...] = x_ref[...] + gate1 * acc


def _merge_call(x, mod, z, y_attn, yf, yb, lp, *, layer, seq, tm):
    t = x.shape[0]
    n_m = t // tm
    per = max(seq // tm, 1)
    hb = tm // SUBLANES
    n_hb = t // SUBLANES
    mod_map = (lambda i: (0, 0, 0)) if tm >= seq else (lambda i: (i // per, 0, 0))
    y_spec = pl.BlockSpec((tm, SSM_W), lambda i: (i, 0))
    yf2 = yf.reshape(t, SSM_W)
    yb2 = yb.reshape(t, SSM_W)
    const2 = lambda i: (0, 0)
    lslab = lambda i: (layer, 0, 0)
    resident = dict(pipeline_mode=pl.Buffered(1))
    return pl.pallas_call(
        functools.partial(_merge_kernel, seq=seq),
        grid=(n_m,),
        in_specs=[pl.BlockSpec((tm, D_MODEL), lambda i: (i, 0)),
                  pl.BlockSpec((None, 1, 6 * D_MODEL), mod_map),
                  pl.BlockSpec((tm, 3 * D_MODEL), lambda i: (i, 0)),
                  pl.BlockSpec((tm, 3 * CONV_W), lambda i: (i, 3)),
                  pl.BlockSpec((tm, SSM_W), lambda i: (i, 6144 // SSM_W)),
                  pl.BlockSpec((SUBLANES, 3 * CONV_W), lambda i: (jnp.maximum(i * hb - 1, 0), 3)),
                  pl.BlockSpec((SUBLANES, 3 * CONV_W),
                               lambda i: (jnp.minimum((i + 1) * hb, n_hb - 1), 3)),
                  pl.BlockSpec((tm, D_MODEL), lambda i: (i, 0)),
                  y_spec, y_spec,
                  pl.BlockSpec((3, CONV_W), const2),
                  pl.BlockSpec((1, CONV_W), const2),
                  pl.BlockSpec((1, SSM_W), const2),
                  pl.BlockSpec((None, SSM_W, SSM_W), lslab, **resident),
                  pl.BlockSpec((1, SSM_W), const2),
                  pl.BlockSpec((None, CONV_W, D_MODEL), lslab, **resident),
                  pl.BlockSpec((None, D_MODEL, D_MODEL), lslab, **resident),
                  pl.BlockSpec((None, SSM_W, D_MODEL), lslab, **resident),
                  pl.BlockSpec((None, D_MODEL, D_MODEL), lslab, **resident)],
        out_specs=pl.BlockSpec((tm, D_MODEL), lambda i: (i, 0)),
        out_shape=jax.ShapeDtypeStruct((t, D_MODEL), F32),
        compiler_params=_cparams(("arbitrary",)),
        name="merge",
    )(x, mod, z, z, z, z, z, y_attn, yf2, yb2,
      lp['conv_w'], lp['conv_b'], lp['ssm_d'], lp['w_glu'], lp['b_glu'],
      lp['w_proj_conv'], lp['w_proj_attn'], lp['w_proj_ssm'], lp['w_out'])


def _ffn_kernel(x_ref, mod_ref, g_ref, wg_ref, wu_ref, wd_ref, fg_ref, o_ref, *, final, tf):
    x = x_ref[...]
    h = _mod_norm(x, g_ref[...], mod_ref[:, 3 * D_MODEL:4 * D_MODEL],
                  mod_ref[:, 4 * D_MODEL:5 * D_MODEL]).astype(BF16)
    acc = None
    bounds = list(range(0, D_FF, tf)) + [D_FF]
    for lo, hi in zip(bounds[:-1], bounds[1:]):
        cols = slice(lo, hi)
        g = jnp.dot(h, wg_ref[:, cols], preferred_element_type=F32)
        u = jnp.dot(h, wu_ref[:, cols], preferred_element_type=F32)
        part = jnp.dot((jax.nn.silu(g) * u).astype(BF16), wd_ref[cols, :], preferred_element_type=F32)
        acc = part if acc is None else acc + part
    xn = x + mod_ref[:, 5 * D_MODEL:6 * D_MODEL] * acc
    if final:
        ms = jnp.mean(xn * xn, axis=-1, keepdims=True)
        xn = xn * lax.rsqrt(ms + RMS_EPS) * fg_ref[...]
    o_ref[...] = xn


def _ffn_call(x, mod, g2, wg, wu, wd, fg, *, layer, seq, tm, tf, final):
    t = x.shape[0]
    per = max(seq // tm, 1)
    mod_map = (lambda i: (0, 0, 0)) if tm >= seq else (lambda i: (i // per, 0, 0))
    lslab = lambda i: (layer, 0, 0)
    resident = dict(pipeline_mode=pl.Buffered(1))
    return pl.pallas_call(
        functools.partial(_ffn_kernel, final=final, tf=tf),
        grid=(t // tm,),
        in_specs=[pl.BlockSpec((tm, D_MODEL), lambda i: (i, 0)),
                  pl.BlockSpec((None, 1, 6 * D_MODEL), mod_map),
                  pl.BlockSpec((1, D_MODEL), lambda i: (0, 0)),
                  pl.BlockSpec((None, D_MODEL, D_FF), lslab, **resident),
                  pl.BlockSpec((None, D_MODEL, D_FF), lslab, **resident),
                  pl.BlockSpec((None, D_FF, D_MODEL), lslab, **resident),
                  pl.BlockSpec((1, D_MODEL), lambda i: (0, 0))],
        out_specs=pl.BlockSpec((tm, D_MODEL), lambda i: (i, 0)),
        out_shape=jax.ShapeDtypeStruct((t, D_MODEL), F32),
        compiler_params=_cparams(("arbitrary",)),
        name="ffn",
    )(x, mod, g2, wg, wu, wd, fg)


def _state_rows(s):
    b = s.shape[0]
    return jnp.transpose(s, (1, 0, 2, 3)).reshape(2, 2 * b, HALF_STATE)


def _state_unrows(s, b):
    return jnp.transpose(s.reshape(2, b, SSM_GROUPS, SSM_STATE), (1, 0, 2, 3))


def kernel(x_prompt, x_sample, c, cache_k, cache_v, state_ssm_re, state_ssm_im, c_ctx, w_ada, b_ada, norm1_g, w_in, conv_w, conv_b, q_norm_g, k_norm_g, ssm_lambda_re, ssm_lambda_im, ssm_b_re, ssm_b_im, ssm_c_re, ssm_c_im, ssm_log_dt, ssm_d, w_glu, b_glu, w_proj_conv, w_proj_attn, w_proj_ssm, w_out, norm2_g, w_ffn_gate, w_ffn_up, w_ffn_down, final_norm_g):
    nb_c, seq_c, _ = x_prompt.shape
    nb_l, seq_l, _ = x_sample.shape
    past = cache_k.shape[2]

    cvec = jnp.zeros((8, D_MODEL), F32).at[0].set(c_ctx).at[1:1 + nb_l].set(c)
    mod_all = _ada_call(cvec, w_ada, b_ada)

    a_re, a_im, bb_re, bb_im = _disc_call(ssm_lambda_re, ssm_lambda_im, ssm_log_dt, ssm_b_re, ssm_b_im)
    bbt, cct, a8 = _ssm_weights(a_re, a_im, bb_re, bb_im, ssm_c_re, ssm_c_im)

    w_in_p = w_in.astype(BF16)
    wb = {k: v.astype(BF16) for k, v in dict(
        w_glu=w_glu, w_proj_conv=w_proj_conv, w_proj_attn=w_proj_attn, w_proj_ssm=w_proj_ssm,
        w_out=w_out, w_ffn_gate=w_ffn_gate, w_ffn_up=w_ffn_up, w_ffn_down=w_ffn_down).items()}

    bd = jnp.asarray(np.kron(np.eye(2 * LANES // HEAD_DIM, dtype=np.float32),
                             np.ones((HEAD_DIM, HEAD_DIM), np.float32)), dtype=BF16)
    tabs = _rope_tables(seq_l)
    fg = final_norm_g.reshape(1, D_MODEL)

    xc = x_prompt.reshape(nb_c * seq_c, D_MODEL)
    xl = x_sample.reshape(nb_l * seq_l, D_MODEL)
    h0_zero = jnp.zeros((2, 2 * nb_c, 2 * HALF_STATE), F32)
    ck = cache_k.reshape(nb_l, DEPTH, past, N_KV_HEADS * HEAD_DIM)
    cv = cache_v.reshape(nb_l, DEPTH, past, N_KV_HEADS * HEAD_DIM)
    srs, sis = [], []
    new_k = jnp.zeros((nb_c, DEPTH, seq_c, N_KV_HEADS * HEAD_DIM), F32)
    new_v = jnp.zeros((nb_c, DEPTH, seq_c, N_KV_HEADS * HEAD_DIM), F32)

    for l in range(DEPTH):
        lp = dict(conv_w=conv_w[l], conv_b=conv_b[l].reshape(1, CONV_W),
                  ssm_d=ssm_d[l].reshape(1, SSM_W), b_glu=b_glu[l].reshape(1, SSM_W),
                  w_glu=wb['w_glu'], w_proj_conv=wb['w_proj_conv'],
                  w_proj_attn=wb['w_proj_attn'], w_proj_ssm=wb['w_proj_ssm'],
                  w_out=wb['w_out'])
        g1 = norm1_g[l].reshape(1, D_MODEL)
        g2 = norm2_g[l].reshape(1, D_MODEL)
        qg = jnp.tile(q_norm_g[l], N_HEADS).reshape(1, D_MODEL)
        kg = jnp.tile(k_norm_g[l], N_KV_HEADS).reshape(1, 2 * LANES)
        mod_c = mod_all[l, 0:1].reshape(1, 1, 6 * D_MODEL)
        mod_l = mod_all[l, 1:1 + nb_l].reshape(nb_l, 1, 6 * D_MODEL)
        final = l == DEPTH - 1

        z, new_k, new_v = _inproj_call(xc, mod_c, g1, w_in_p, qg, kg, bd, None, (new_k, new_v),
                                       layer=l, seq=seq_c, tm=TOKEN_TILE)
        y_attn = _attn_call(z, None, None, layer=l, nb=nb_c, seq=seq_c, past=0,
                            **_attn_tiling(seq_c))
        yf, yb, ht = _ssm_call(z, bbt[l], cct[l], a8[l], h0_zero, nb=nb_c, seq=seq_c,
                               nbg=min(nb_c, SSM_MAX_BATCH))
        xc = _merge_call(xc, mod_c, z, y_attn, yf, yb, lp, layer=l, seq=seq_c, tm=TOKEN_TILE)
        xc = _ffn_call(xc, mod_c, g2, wb['w_ffn_gate'], wb['w_ffn_up'], wb['w_ffn_down'],
                       fg, layer=l, seq=seq_c, tm=FFN_TOKEN_TILE, tf=FFN_CHUNK, final=final)
        srs.append(_state_unrows(ht[:, :, :HALF_STATE], nb_c))
        sis.append(_state_unrows(ht[:, :, HALF_STATE:], nb_c))

        h0 = jnp.concatenate([_state_rows(state_ssm_re[:, l]), _state_rows(state_ssm_im[:, l])], axis=2)
        (z,) = _inproj_call(xl, mod_l, g1, w_in_p, qg, kg, bd, tabs, None,
                            layer=l, seq=seq_l, tm=TOKEN_TILE)
        y_attn = _attn_call(z, ck, cv, layer=l, nb=nb_l, seq=seq_l, past=past,
                            **_attn_tiling(seq_l))
        yf, yb, _ = _ssm_call(z, bbt[l], cct[l], a8[l], h0, nb=nb_l, seq=seq_l,
                              nbg=min(nb_l, SSM_MAX_BATCH))
        xl = _merge_call(xl, mod_l, z, y_attn, yf, yb, lp, layer=l, seq=seq_l, tm=TOKEN_TILE)
        xl = _ffn_call(xl, mod_l, g2, wb['w_ffn_gate'], wb['w_ffn_up'], wb['w_ffn_down'],
                       fg, layer=l, seq=seq_l, tm=FFN_TOKEN_TILE, tf=FFN_CHUNK, final=final)

    y_prompt = xc.reshape(nb_c, seq_c, D_MODEL)
    y_sample = xl.reshape(nb_l, seq_l, D_MODEL)
    kv_shape = (nb_c, DEPTH, seq_c, N_KV_HEADS, HEAD_DIM)
    return (y_prompt, y_sample, new_k.reshape(kv_shape), new_v.reshape(kv_shape),
            jnp.stack(srs, axis=1), jnp.stack(sis, axis=1))
```

```python
import functools
import math

import jax
import jax.numpy as jnp
import numpy as np
from jax import lax
from jax.experimental import pallas as pl
from jax.experimental.pallas import tpu as pltpu

F32 = jnp.float32
BF16 = jnp.bfloat16

D_MODEL = 1024
DEPTH = 4
GRID_W = 64
HEAD_DIM = 64
N_HEADS = 16
N_KV_HEADS = 4
ROPE_THETA = 10000.0
CONV_W = 512
SSM_W = 512
SSM_GROUP_CH = 16
SSM_GROUPS = 32
SSM_STATE = 64
D_FF = 2816
RMS_EPS = 1e-6
IN_COLS = 6656

LANES = 128
SUBLANES = 8
VMEM_LIMIT = 56 * 1024 * 1024

Z_TILE = 512
Z_SRC = (3584, 4096, 4608, 5120, 5632, 6144, 1536, 2048, 2560, 0, 512, 1024, 3072)
Z_Q_TILES = (6, 7)
Z_KV_TILE = 8
HALF_STATE = SSM_GROUPS // 2 * SSM_STATE
SSM_ROWS = 512
VT_ROWS = HEAD_DIM + 16

TOKEN_TILE = 512
FFN_CHUNK = 6 * 2 * LANES
FFN_TOKEN_TILE = 1024
SSM_MAX_BATCH = 16


def _attn_tiling(seq):
    if seq <= 2 * LANES:
        return dict(tq=seq, tok=seq, kchunk=SUBLANES, wave_units=1, barrier=False)
    return dict(tq=8 * LANES, tok=HEAD_DIM, kchunk=4 * SUBLANES, wave_units=2, barrier=True)


def _cparams(sem):
    return pltpu.CompilerParams(dimension_semantics=sem, vmem_limit_bytes=VMEM_LIMIT)


def _mod_norm(x, g, shift, scale):
    ms = jnp.mean(x * x, axis=-1, keepdims=True)
    y = x * lax.rsqrt(ms + RMS_EPS) * g
    return y * (1.0 + scale) + shift


def _ada_kernel(c_ref, w_ref, b_ref, o_ref):
    s = jax.nn.silu(c_ref[...])
    o_ref[...] = jnp.dot(s.astype(BF16), w_ref[...].astype(BF16),
                         preferred_element_type=F32) + b_ref[...]


def _ada_call(cvec, w_ada, b_ada):
    tn = 1536
    return pl.pallas_call(
        _ada_kernel,
        grid=(DEPTH, 6 * D_MODEL // tn),
        in_specs=[pl.BlockSpec((8, D_MODEL), lambda l, j: (0, 0)),
                  pl.BlockSpec((None, D_MODEL, tn), lambda l, j: (l, 0, j)),
                  pl.BlockSpec((None, 1, tn), lambda l, j: (l, 0, j))],
        out_specs=pl.BlockSpec((None, 8, tn), lambda l, j: (l, 0, j)),
        out_shape=jax.ShapeDtypeStruct((DEPTH, 8, 6 * D_MODEL), F32),
        compiler_params=_cparams(("arbitrary", "arbitrary")),
        name="ada_mod",
    )(cvec, w_ada, b_ada.reshape(DEPTH, 1, 6 * D_MODEL))


def _disc_kernel(lre_ref, lim_ref, ldt_ref, bre_ref, bim_ref,
                 are_ref, aim_ref, bbre_ref, bbim_ref):
    lre = lre_ref[...]
    lim = lim_ref[...]
    dt = jnp.exp(ldt_ref[...])
    mag = jnp.exp(lre * dt)
    a_re = mag * jnp.cos(lim * dt)
    a_im = mag * jnp.sin(lim * dt)
    den = lre * lre + lim * lim
    n_re = a_re - 1.0
    coef_re = (n_re * lre + a_im * lim) / den
    coef_im = (a_im * lre - n_re * lim) / den
    are_ref[...] = a_re
    aim_ref[...] = a_im
    bre = bre_ref[...]
    bim = bim_ref[...]
    bbre_ref[...] = coef_re * bre - coef_im * bim
    bbim_ref[...] = coef_re * bim + coef_im * bre


def _disc_call(lam_re, lam_im, log_dt, b_re, b_im):
    n = DEPTH * 2 * SSM_GROUPS
    lre = lam_re.reshape(n, 1, SSM_STATE)
    lim = lam_im.reshape(n, 1, SSM_STATE)
    ldt = log_dt.reshape(n, 1, 1)
    bre = jnp.swapaxes(b_re.reshape(n, SSM_STATE, SSM_GROUP_CH), 1, 2)
    bim = jnp.swapaxes(b_im.reshape(n, SSM_STATE, SSM_GROUP_CH), 1, 2)
    small = jax.ShapeDtypeStruct((n, 1, SSM_STATE), F32)
    big = jax.ShapeDtypeStruct((n, SSM_GROUP_CH, SSM_STATE), F32)
    return pl.pallas_call(
        _disc_kernel, out_shape=(small, small, big, big), name="ssm_disc",
    )(lre, lim, ldt, bre, bim)


def _ssm_weights(a_re, a_im, bb_re, bb_im, c_re, c_im):
    def bb_tiles(bb):
        return bb.reshape(DEPTH, 2, 2, 4, 4, SSM_GROUP_CH, SSM_STATE)
    bb = jnp.stack([bb_tiles(bb_re), bb_tiles(bb_im)], axis=2)
    sel = np.zeros((4, 8, 4), np.float32)
    for m in range(4):
        for j in range(4):
            sel[m, 4 * (m % 2) + j, j] = 1.0
    bbt = jnp.einsum('ldrhmjcp,mkj->ldrmhkcjp', bb, jnp.asarray(sel))
    bbt = bbt.reshape(DEPTH, 2, 8, 2 * LANES, 2 * LANES).astype(BF16)
    def c_tiles(c):
        return c.reshape(DEPTH, 2, 2, 2, 8, SSM_GROUP_CH, SSM_STATE)
    cc = jnp.stack([c_tiles(c_re), -c_tiles(c_im)], axis=2)
    eye = jnp.eye(8, dtype=F32)
    cct = jnp.einsum('ldrhmkop,kj->ldmrkphjo', cc, eye)
    cct = cct.reshape(DEPTH, 2, 2, 2 * 8 * SSM_STATE, 2 * LANES).astype(BF16)
    def a_rows(a):
        a = a.reshape(DEPTH, 2, 2, HALF_STATE)
        return jnp.tile(a, (1, 1, 4, 1))
    a8 = jnp.stack([a_rows(a_re), a_rows(a_im)], axis=2)
    return bbt, cct, a8


def _inproj_kernel(*refs, rope, emit_kv):
    it = iter(refs)
    x_ref, mod_ref, g_ref, w_ref, qg_ref, kg_ref, bd_ref = (next(it) for _ in range(7))
    cos = sup = sdn = None
    if rope:
        cos, sup, sdn = next(it)[...], next(it)[...], next(it)[...]
    if emit_kv:
        next(it), next(it)
    z_ref = next(it)
    kout_ref, vout_ref = (next(it), next(it)) if emit_kv else (None, None)

    h = _mod_norm(x_ref[...], g_ref[...], mod_ref[:, 0:D_MODEL], mod_ref[:, D_MODEL:2 * D_MODEL])
    h = h.astype(BF16)
    bd = bd_ref[...]
    for c, src in enumerate(Z_SRC):
        zt = jnp.dot(h, w_ref[:, src:src + Z_TILE], preferred_element_type=F32)
        if c in Z_Q_TILES:
            qcols = slice((c - Z_Q_TILES[0]) * Z_TILE, (c - Z_Q_TILES[0] + 1) * Z_TILE)
            zt = zt * lax.rsqrt(_head_ssq(zt, bd) * (1.0 / HEAD_DIM) + RMS_EPS) * qg_ref[:, qcols]
            if rope:
                zt = _rope(zt, cos, sup, sdn)
            zt = zt * (HEAD_DIM ** -0.5 * math.log2(math.e))
        elif c == Z_KV_TILE:
            k = zt[:, :2 * LANES]
            v = zt[:, 2 * LANES:]
            kn = k * lax.rsqrt(_head_ssq(k, bd) * (1.0 / HEAD_DIM) + RMS_EPS) * kg_ref[...]
            if emit_kv:
                kout_ref[...] = kn.reshape(kout_ref.shape)
                vout_ref[...] = v.reshape(vout_ref.shape)
            if rope:
                kn = _rope(kn, cos, sup, sdn)
            zt = jnp.concatenate([kn, v], axis=1)
        z_ref[:, c * Z_TILE:(c + 1) * Z_TILE] = zt.astype(BF16)


def _inproj_call(x, mod, g1, w_in, qg, kg, bd, tabs, caches, *, layer, seq, tm):
    t = x.shape[0]
    per = max(seq // tm, 1)
    mod_map = (lambda i: (0, 0, 0)) if tm >= seq else (lambda i: (i // per, 0, 0))
    in_specs = [pl.BlockSpec((tm, D_MODEL), lambda i: (i, 0)),
                pl.BlockSpec((None, 1, 6 * D_MODEL), mod_map),
                pl.BlockSpec((1, D_MODEL), lambda i: (0, 0)),
                pl.BlockSpec((None, D_MODEL, IN_COLS), lambda i: (layer, 0, 0),
                             pipeline_mode=pl.Buffered(1)),
                pl.BlockSpec((1, D_MODEL), lambda i: (0, 0)),
                pl.BlockSpec((1, 2 * LANES), lambda i: (0, 0)),
                pl.BlockSpec((2 * LANES, 2 * LANES), lambda i: (0, 0))]
    args = [x, mod, g1, w_in, qg, kg, bd]
    if tabs is not None:
        assert tm <= seq
        in_specs += [pl.BlockSpec((tm, LANES), lambda i: (i % per, 0))] * 3
        args += list(tabs)
    out_specs = [pl.BlockSpec((tm, IN_COLS), lambda i: (i, 0))]
    out_shape = [jax.ShapeDtypeStruct((t, IN_COLS), BF16)]
    aliases = {}
    if caches is not None:
        assert tm % seq == 0
        aliases = {len(args): 1, len(args) + 1: 2}
        in_specs += [pl.BlockSpec(memory_space=pl.ANY)] * 2
        args += list(caches)
        out_specs += [pl.BlockSpec((tm // seq, None, seq, 2 * LANES), lambda i: (i, layer, 0, 0))] * 2
        out_shape += [jax.ShapeDtypeStruct(a.shape, a.dtype) for a in caches]
    return pl.pallas_call(
        functools.partial(_inproj_kernel, rope=tabs is not None, emit_kv=caches is not None),
        grid=(t // tm,),
        in_specs=in_specs, out_specs=out_specs, out_shape=out_shape, input_output_aliases=aliases,
        compiler_params=_cparams(("arbitrary",)),
        name="in_proj",
    )(*args)


def _head_ssq(x, bd):
    parts = []
    for t in range(x.shape[1] // (2 * LANES)):
        xs = x[:, 2 * LANES * t:2 * LANES * (t + 1)]
        parts.append(jnp.dot((xs * xs).astype(BF16), bd, preferred_element_type=F32))
    return parts[0] if len(parts) == 1 else jnp.concatenate(parts, axis=1)


def _rope(x, cos, sin_up, sin_dn):
    w = x.shape[1]
    rep = w // LANES
    cos = jnp.tile(cos, (1, rep))
    sin_up = jnp.tile(sin_up, (1, rep))
    sin_dn = jnp.tile(sin_dn, (1, rep))
    return x * cos + pltpu.roll(x, w - 16, 1) * sin_up + pltpu.roll(x, 16, 1) * sin_dn


def _dup_heads(x, lane_lo):
    xr = pltpu.roll(x, HEAD_DIM, 1)
    return jnp.where(lane_lo, x, xr), jnp.where(lane_lo, xr, x)


def _attn_kernel(*refs, seq, past, tok, kchunk, wave_units, barrier):
    it = iter(refs)
    q_ref, kv_ref = next(it), next(it)
    kc_ref = vc_ref = None
    if past:
        kc_ref, vc_ref = next(it), next(it)
    o_ref = next(it)
    k2_scr, vt_scr, qst_scr, s_scr, m_scr, p_scr, ot_scr, o_scr = (next(it) for _ in range(8))

    qi = pl.program_id(1)

    @pl.when(qi == 0)
    def _prep():
        kv = kv_ref[...].astype(F32)
        kn = kv[:, :2 * LANES]
        v = kv[:, 2 * LANES:]
        lane_lo = lax.broadcasted_iota(jnp.int32, (seq, LANES), 1) < HEAD_DIM
        for t in range(2):
            ka, kb = _dup_heads(kn[:, LANES * t:LANES * (t + 1)], lane_lo)
            k2_scr[2 * t, 0:seq, :] = ka.astype(BF16)
            k2_scr[2 * t + 1, 0:seq, :] = kb.astype(BF16)
        vt = v.T.astype(BF16)
        for kvh in range(N_KV_HEADS):
            vt_scr[kvh * VT_ROWS:kvh * VT_ROWS + HEAD_DIM, 0:seq] = vt[kvh * HEAD_DIM:(kvh + 1) * HEAD_DIM]
            vt_scr[kvh * VT_ROWS + HEAD_DIM:(kvh + 1) * VT_ROWS, :] = jnp.ones(
                (VT_ROWS - HEAD_DIM, vt_scr.shape[1]), BF16)
        if past:
            kc = kc_ref[...]
            lane_lo_p = lax.broadcasted_iota(jnp.int32, (past, LANES), 1) < HEAD_DIM
            for t in range(2):
                ka, kb = _dup_heads(kc[:, LANES * t:LANES * (t + 1)], lane_lo_p)
                k2_scr[2 * t, seq:seq + past, :] = ka.astype(BF16)
                k2_scr[2 * t + 1, seq:seq + past, :] = kb.astype(BF16)
            vct = vc_ref[...].T.astype(BF16)
            for kvh in range(N_KV_HEADS):
                vt_scr[kvh * VT_ROWS:kvh * VT_ROWS + HEAD_DIM, seq:seq + past] = (
                    vct[kvh * HEAD_DIM:(kvh + 1) * HEAD_DIM])

    qn = q_ref[...].astype(F32)
    tq = qn.shape[0]

    n_tb = tq // tok
    n_units = N_KV_HEADS * n_tb
    keys = k2_scr.shape[1]
    n_chunks = keys // kchunk
    lane_lo = lax.broadcasted_iota(jnp.int32, (tok, LANES), 1) < HEAD_DIM
    for kvh in range(N_KV_HEADS):
        for tb in range(n_tb):
            stack = []
            for t in (2 * kvh, 2 * kvh + 1):
                qt = qn[tb * tok:(tb + 1) * tok, LANES * t:LANES * (t + 1)]
                stack += [jnp.where(lane_lo, qt, 0.0), jnp.where(lane_lo, 0.0, qt)]
            qst_scr[kvh * n_tb + tb] = jnp.concatenate(stack, axis=0).astype(BF16)

    def kv_of(u):
        return u // n_tb if isinstance(u, int) else lax.shift_right_logical(u, n_tb.bit_length() - 1)

    def scores(u, slot):
        s = lax.dot_general(k2_scr[kv_of(u)], qst_scr[u], (((1,), (1,)), ((), ())),
                            preferred_element_type=F32)
        s_scr[slot] = s
        m_run = s[0:kchunk]
        for c in range(1, n_chunks):
            m_run = jnp.maximum(m_run, s[c * kchunk:(c + 1) * kchunk])
        m_scr[slot] = jnp.broadcast_to(jnp.max(m_run, axis=0, keepdims=True), m_scr.shape[1:])

    def softmax(u, slot):
        del u
        m = jnp.broadcast_to(m_scr[slot, 0:1, :], (kchunk, 4 * tok))
        for c in range(n_chunks):
            p = jnp.exp2(s_scr[slot, c * kchunk:(c + 1) * kchunk, :] - m)
            p_scr[slot, c * kchunk:(c + 1) * kchunk, :] = p.astype(BF16)

    def values(u, slot):
        r0 = kv_of(u) * VT_ROWS
        v_rows = (slice(r0, r0 + VT_ROWS) if isinstance(r0, int)
                  else pl.ds(pl.multiple_of(r0, VT_ROWS), VT_ROWS))
        ot_scr[slot] = jnp.dot(vt_scr[v_rows, :], p_scr[slot], preferred_element_type=F32)

    def finish(u, slot):
        ot = ot_scr[slot]
        ot = ot[0:HEAD_DIM] * (1.0 / ot[HEAD_DIM:HEAD_DIM + 1])
        for pair in range(2):
            if tok % LANES == 0:
                w = jnp.concatenate([ot[:, 2 * pair * tok:(2 * pair + 1) * tok],
                                     ot[:, (2 * pair + 1) * tok:(2 * pair + 2) * tok]], axis=0)
            else:
                both = ot[:, LANES * pair:LANES * (pair + 1)]
                w = jnp.concatenate([both, pltpu.roll(both, tok, 1)], axis=0)
            o_scr[u, pair] = w.T[0:tok].astype(BF16)

    n_waves = n_units // wave_units
    assert n_tb & (n_tb - 1) == 0 and n_units % wave_units == 0
    assert n_waves >= 4

    def wave(g, slot_set, stage):
        for w in range(wave_units):
            stage(g * wave_units + w, slot_set * wave_units + w)

    def step_barrier():
        if barrier:
            pl.delay(1)

    wave(0, 0, scores)
    step_barrier()
    wave(0, 0, softmax)
    wave(1, 1, scores)
    for w in range(wave_units):
        ot_scr[2 * wave_units + w] = jnp.ones(ot_scr.shape[1:], F32)

    def step(c, cur):
        step_barrier()
        wave(jnp.minimum(c + 1, n_waves - 1), (cur + 1) % 3, scores)
        wave(c, cur, softmax)
        wave(c - 1, (cur + 2) % 3, values)
        wave(jnp.maximum(c - 2, 0), (cur + 1) % 3, finish)

    def pipelined(i, carry):
        for k in range(3):
            step(3 * i + 1 + k, (1 + k) % 3)
        return carry

    rolled = (n_waves - 1) // 3
    lax.fori_loop(0, rolled, pipelined, 0)
    for c in range(3 * rolled + 1, n_waves):
        step(c, c % 3)
    step_barrier()
    wave(n_waves - 1, (n_waves - 1) % 3, values)
    wave(n_waves - 2, (n_waves - 2) % 3, finish)
    step_barrier()
    wave(n_waves - 1, (n_waves - 1) % 3, finish)

    for kvh in range(N_KV_HEADS):
        for tb in range(n_tb):
            u = kvh * n_tb + tb
            for pair, t in enumerate((2 * kvh, 2 * kvh + 1)):
                o_ref[tb * tok:(tb + 1) * tok, LANES * t:LANES * (t + 1)] = o_scr[u, pair]


def _attn_call(z, kc, vc, *, layer, nb, seq, tq, past, tok, kchunk, wave_units, barrier):
    nq = seq // tq
    keys = seq + past
    n_units = N_KV_HEADS * (tq // tok)
    q_col = 3072 // D_MODEL
    kv_col = 4096 // 512
    in_specs = [pl.BlockSpec((tq, D_MODEL), lambda b, i: (b * nq + i, q_col)),
                pl.BlockSpec((seq, 512), lambda b, i: (b, kv_col),
                             pipeline_mode=pl.Buffered(1 if nq > 1 else 2))]
    args = [z, z]
    if past:
        in_specs += [pl.BlockSpec((None, None, past, 2 * LANES), lambda b, i: (b, layer, 0, 0))] * 2
        args += [kc, vc]
    return pl.pallas_call(
        functools.partial(_attn_kernel, seq=seq, past=past,
                          tok=tok, kchunk=kchunk, wave_units=wave_units, barrier=barrier),
        grid=(nb, nq),
        in_specs=in_specs,
        out_specs=pl.BlockSpec((tq, D_MODEL), lambda b, i: (b * nq + i, 0)),
        out_shape=jax.ShapeDtypeStruct((nb * seq, D_MODEL), BF16),
        scratch_shapes=[pltpu.VMEM((N_KV_HEADS, keys, LANES), BF16),
                        pltpu.VMEM((N_KV_HEADS * VT_ROWS, keys), BF16),
                        pltpu.VMEM((n_units, 4 * tok, LANES), BF16),
                        pltpu.VMEM((3 * wave_units, keys, 4 * tok), F32),
                        pltpu.VMEM((3 * wave_units, SUBLANES, 4 * tok), F32),
                        pltpu.VMEM((3 * wave_units, keys, 4 * tok), BF16),
                        pltpu.VMEM((3 * wave_units, VT_ROWS, 4 * tok), F32),
                        pltpu.VMEM((n_units, 2, tok, LANES), BF16)],
        compiler_params=_cparams(("arbitrary", "arbitrary")),
        name="attention",
    )(*args)


def _rope_tables(length):
    f = HEAD_DIM // 4
    inv = ROPE_THETA ** (-np.arange(f, dtype=np.float32) / f)
    pos = np.arange(length)
    row = (pos // GRID_W).astype(np.float32)
    col = (pos % GRID_W).astype(np.float32)
    ang_r = row[:, None] * inv[None, :]
    ang_c = col[:, None] * inv[None, :]
    z = np.zeros_like(ang_r)
    cos64 = np.concatenate([np.cos(ang_r), np.cos(ang_r), np.cos(ang_c), np.cos(ang_c)], axis=1)
    up64 = np.concatenate([-np.sin(ang_r), z, -np.sin(ang_c), z], axis=1)
    dn64 = np.concatenate([z, np.sin(ang_r), z, np.sin(ang_c)], axis=1)
    tile2 = lambda a: jnp.asarray(np.tile(a, (1, 2)), dtype=F32)
    return tile2(cos64), tile2(up64), tile2(dn64)


def _ssm_kernel(uf_ref, ub_ref, pin_ref, pout_ref, bbt_ref, cct_ref, a_ref, h0_ref,
                yf_ref, yb_ref, ht_ref, x_scr, hc_scr, *, rows_per_step, steps):
    j = pl.program_id(1)
    r = rows_per_step
    rows = r * steps
    nbg = r // 2
    tok = nbg * steps

    @pl.when(j == 0)
    def _():
        hc_scr[...] = h0_ref[...]

    first_half = (lax.broadcasted_iota(jnp.int32, (rows, LANES), 0) & 1) == 0
    width = SUBLANES * HALF_STATE // r
    u_refs = (uf_ref, ub_ref)
    y_refs = (yf_ref, yb_ref)

    def project_in(d):
        u_bt = u_refs[d][...].reshape(tok, SSM_W)
        u = jnp.dot(pin_ref[...], u_bt, preferred_element_type=F32)
        kcat = []
        for q in range(2):
            u0 = u[:, LANES * q:LANES * (q + 1)]
            u1 = u[:, 2 * LANES + LANES * q:2 * LANES + LANES * (q + 1)]
            kcat.append(jnp.concatenate([jnp.where(first_half, u0, 0.0),
                                         jnp.where(first_half, 0.0, u1)], axis=1).astype(BF16))
        for n in range(8):
            x_scr[d, :, 2 * LANES * n:2 * LANES * (n + 1)] = jnp.dot(
                kcat[(n % 4) // 2], bbt_ref[d, n], preferred_element_type=F32)

    def scan(d):
        for p in range(HALF_STATE // width):
            re_cols = slice(p * width, (p + 1) * width)
            im_cols = slice(HALF_STATE + p * width, HALF_STATE + (p + 1) * width)
            reps = r // SUBLANES
            a_re = jnp.tile(a_ref[d, 0, :, re_cols], (reps, 1))
            a_im = jnp.tile(a_ref[d, 1, :, re_cols], (reps, 1))
            h_re = hc_scr[d, :, re_cols]
            h_im = hc_scr[d, :, im_cols]
            for kk in range(steps):
                k = kk if d == 0 else steps - 1 - kk
                srows = slice(k * r, (k + 1) * r)
                n_re = a_re * h_re - a_im * h_im + x_scr[d, srows, re_cols]
                n_im = a_re * h_im + a_im * h_re + x_scr[d, srows, im_cols]
                x_scr[d, srows, re_cols] = n_re
                x_scr[d, srows, im_cols] = n_im
                h_re, h_im = n_re, n_im
            hc_scr[d, :, re_cols] = h_re
            hc_scr[d, :, im_cols] = h_im

    def project_out(d):
        yy = []
        for m in range(2):
            hk = jnp.concatenate(
                [x_scr[d, :, 512 * m:512 * (m + 1)],
                 x_scr[d, :, HALF_STATE + 512 * m:HALF_STATE + 512 * (m + 1)]], axis=1).astype(BF16)
            yy.append(jnp.dot(hk, cct_ref[d, m], preferred_element_type=F32))
        for hh in range(2):
            ysel = jnp.concatenate([yy[0][:, LANES * hh:LANES * (hh + 1)],
                                    yy[1][:, LANES * hh:LANES * (hh + 1)]], axis=1).astype(BF16)
            y_bt = jnp.dot(pout_ref[hh], ysel, preferred_element_type=F32)
            y_refs[d][:, :, 2 * LANES * hh:2 * LANES * (hh + 1)] = (
                y_bt.reshape(nbg, steps, 2 * LANES).astype(BF16))

    for d in range(2):
        project_in(d)
        scan(d)
        project_out(d)

    @pl.when(j == pl.num_programs(1) - 1)
    def _():
        ht_ref[...] = hc_scr[...]


def _ssm_perms(nbg, steps):
    tok = nbg * steps
    pin = np.zeros((2 * tok, tok), np.float32)
    for t in range(steps):
        for b in range(nbg):
            for h in range(2):
                pin[(t * nbg + b) * 2 + h, b * steps + t] = 1.0
    pout = np.zeros((2, tok, 2 * tok), np.float32)
    for h in range(2):
        pout[h] = (pin * (np.arange(2 * tok)[:, None] % 2 == h)).T
    return jnp.asarray(pin, dtype=BF16), jnp.asarray(pout, dtype=BF16)


def _ssm_call(z, bbt, cct, a8, h0, *, nb, seq, nbg):
    r = 2 * nbg
    steps = SSM_ROWS // r
    n = seq // steps
    ngrp = nb // nbg
    z3 = z.reshape(nb, seq, IN_COLS)
    pin, pout = _ssm_perms(nbg, steps)
    ys = jax.ShapeDtypeStruct((nb, seq, SSM_W), BF16)
    ublk = (nbg, steps, SSM_W)
    ucol = 6144 // SSM_W
    hblk = (2, r, 2 * HALF_STATE)
    c4 = lambda g, j: (0, 0, 0, 0)
    return pl.pallas_call(
        functools.partial(_ssm_kernel, rows_per_step=r, steps=steps),
        grid=(ngrp, n),
        in_specs=[pl.BlockSpec(ublk, lambda g, j: (g, j, ucol)),
                  pl.BlockSpec(ublk, lambda g, j: (g, n - 1 - j, ucol)),
                  pl.BlockSpec(pin.shape, lambda g, j: (0, 0)),
                  pl.BlockSpec(pout.shape, lambda g, j: (0, 0, 0)),
                  pl.BlockSpec(bbt.shape, c4),
                  pl.BlockSpec(cct.shape, c4),
                  pl.BlockSpec(a8.shape, c4),
                  pl.BlockSpec(hblk, lambda g, j: (0, g, 0))],
        out_specs=[pl.BlockSpec(ublk, lambda g, j: (g, j, 0)),
                   pl.BlockSpec(ublk, lambda g, j: (g, n - 1 - j, 0)),
                   pl.BlockSpec(hblk, lambda g, j: (0, g, 0))],
        out_shape=[ys, ys, jax.ShapeDtypeStruct(h0.shape, F32)],
        scratch_shapes=[pltpu.VMEM((2, SSM_ROWS, 2 * HALF_STATE), F32),
                        pltpu.VMEM(hblk, F32)],
        compiler_params=_cparams(("arbitrary", "arbitrary")),
        name="ssm_scan",
    )(z3, z3, pin, pout, bbt, cct, a8, h0)


def _merge_kernel(x_ref, mod_ref, zg_ref, zc_ref, zu_ref, cprev_ref, cnext_ref,
                  ya_ref, yf_ref, yb_ref, cw_ref, cb_ref, sd_ref, wglu_ref, bglu_ref,
                  wpc_ref, wpa_ref, wps_ref, wo_ref, o_ref, *, seq):
    i = pl.program_id(0)
    tm = x_ref.shape[0]
    zc = zc_ref[...].astype(F32)
    cb = zc[:, 0:CONV_W]
    prod = zc[:, CONV_W:2 * CONV_W] * zc[:, 2 * CONV_W:3 * CONV_W]
    cp = cprev_ref[...].astype(F32)
    cn = cnext_ref[...].astype(F32)
    halo_prev = cp[7:8, CONV_W:2 * CONV_W] * cp[7:8, 2 * CONV_W:3 * CONV_W]
    halo_next = cn[0:1, CONV_W:2 * CONV_W] * cn[0:1, 2 * CONV_W:3 * CONV_W]
    row = lax.broadcasted_iota(jnp.int32, (tm, CONV_W), 0)
    tpos = (i * tm + row) % seq
    prev = jnp.where(row == 0, halo_prev, pltpu.roll(prod, 1, 0))
    prev = jnp.where(tpos == 0, 0.0, prev)
    nxt = jnp.where(row == tm - 1, halo_next, pltpu.roll(prod, tm - 1, 0))
    nxt = jnp.where(tpos == seq - 1, 0.0, nxt)
    cw = cw_ref[...]
    y_conv = cb * (cw[0:1] * prev + cw[1:2] * prod + cw[2:3] * nxt + cb_ref[...])

    u = zu_ref[...].astype(F32)
    y = jax.nn.gelu(yf_ref[...].astype(F32) + yb_ref[...].astype(F32) + sd_ref[...] * u)
    y_ssm = y * jax.nn.sigmoid(
        jnp.dot(y.astype(BF16), wglu_ref[...], preferred_element_type=F32) + bglu_ref[...])

    zg = zg_ref[...].astype(F32)
    merged = (jax.nn.sigmoid(zg[:, 0:D_MODEL])
              * jnp.dot(y_conv.astype(BF16), wpc_ref[...], preferred_element_type=F32)
              + jax.nn.sigmoid(zg[:, D_MODEL:2 * D_MODEL])
              * jnp.dot(ya_ref[...], wpa_ref[...], preferred_element_type=F32)
              + jax.nn.sigmoid(zg[:, 2 * D_MODEL:3 * D_MODEL])
              * jnp.dot(y_ssm.astype(BF16), wps_ref[...], preferred_element_type=F32))
    gate1 = mod_ref[:, 2 * D_MODEL:3 * D_MODEL]
    o_ref[...] = x_ref[...] + gate1 * jnp.dot(merged.astype(BF16), wo_ref[...],
                                              preferred_element_type=F32)


def _merge_call(x, mod, z, y_attn, yf, yb, lp, *, layer, seq, tm):
    t = x.shape[0]
    n_m = t // tm
    per = max(seq // tm, 1)
    hb = tm // SUBLANES
    n_hb = t // SUBLANES
    mod_map = (lambda i: (0, 0, 0)) if tm >= seq else (lambda i: (i // per, 0, 0))
    y_spec = pl.BlockSpec((tm, SSM_W), lambda i: (i, 0))
    yf2 = yf.reshape(t, SSM_W)
    yb2 = yb.reshape(t, SSM_W)
    const2 = lambda i: (0, 0)
    lslab = lambda i: (layer, 0, 0)
    resident = dict(pipeline_mode=pl.Buffered(1))
    return pl.pallas_call(
        functools.partial(_merge_kernel, seq=seq),
        grid=(n_m,),
        in_specs=[pl.BlockSpec((tm, D_MODEL), lambda i: (i, 0)),
                  pl.BlockSpec((None, 1, 6 * D_MODEL), mod_map),
                  pl.BlockSpec((tm, 3 * D_MODEL), lambda i: (i, 0)),
                  pl.BlockSpec((tm, 3 * CONV_W), lambda i: (i, 3)),
                  pl.BlockSpec((tm, SSM_W), lambda i: (i, 6144 // SSM_W)),
                  pl.BlockSpec((SUBLANES, 3 * CONV_W), lambda i: (jnp.maximum(i * hb - 1, 0), 3)),
                  pl.BlockSpec((SUBLANES, 3 * CONV_W),
                               lambda i: (jnp.minimum((i + 1) * hb, n_hb - 1), 3)),
                  pl.BlockSpec((tm, D_MODEL), lambda i: (i, 0)),
                  y_spec, y_spec,
                  pl.BlockSpec((3, CONV_W), const2),
                  pl.BlockSpec((1, CONV_W), const2),
                  pl.BlockSpec((1, SSM_W), const2),
                  pl.BlockSpec((None, SSM_W, SSM_W), lslab, **resident),
                  pl.BlockSpec((1, SSM_W), const2),
                  pl.BlockSpec((None, CONV_W, D_MODEL), lslab, **resident),
                  pl.BlockSpec((None, D_MODEL, D_MODEL), lslab, **resident),
                  pl.BlockSpec((None, SSM_W, D_MODEL), lslab, **resident),
                  pl.BlockSpec((None, D_MODEL, D_MODEL), lslab, **resident)],
        out_specs=pl.BlockSpec((tm, D_MODEL), lambda i: (i, 0)),
        out_shape=jax.ShapeDtypeStruct((t, D_MODEL), F32),
        compiler_params=_cparams(("arbitrary",)),
        name="merge",
    )(x, mod, z, z, z, z, z, y_attn, yf2, yb2,
      lp['conv_w'], lp['conv_b'], lp['ssm_d'], lp['w_glu'], lp['b_glu'],
      lp['w_proj_conv'], lp['w_proj_attn'], lp['w_proj_ssm'], lp['w_out'])


def _ffn_kernel(x_ref, mod_ref, g_ref, wg_ref, wu_ref, wd_ref, fg_ref, o_ref, *, final, tf):
    x = x_ref[...]
    h = _mod_norm(x, g_ref[...], mod_ref[:, 3 * D_MODEL:4 * D_MODEL],
                  mod_ref[:, 4 * D_MODEL:5 * D_MODEL]).astype(BF16)
    acc = None
    bounds = list(range(0, D_FF, tf)) + [D_FF]
    for lo, hi in zip(bounds[:-1], bounds[1:]):
        cols = slice(lo, hi)
        g = jnp.dot(h, wg_ref[:, cols], preferred_element_type=F32)
        u = jnp.dot(h, wu_ref[:, cols], preferred_element_type=F32)
        part = jnp.dot((jax.nn.silu(g) * u).astype(BF16), wd_ref[cols, :], preferred_element_type=F32)
        acc = part if acc is None else acc + part
    xn = x + mod_ref[:, 5 * D_MODEL:6 * D_MODEL] * acc
    if final:
        ms = jnp.mean(xn * xn, axis=-1, keepdims=True)
        xn = xn * lax.rsqrt(ms + RMS_EPS) * fg_ref[...]
    o_ref[...] = xn


def _ffn_call(x, mod, g2, wg, wu, wd, fg, *, layer, seq, tm, tf, final):
    t = x.shape[0]
    per = max(seq // tm, 1)
    mod_map = (lambda i: (0, 0, 0)) if tm >= seq else (lambda i: (i // per, 0, 0))
    lslab = lambda i: (layer, 0, 0)
    resident = dict(pipeline_mode=pl.Buffered(1))
    return pl.pallas_call(
        functools.partial(_ffn_kernel, final=final, tf=tf),
        grid=(t // tm,),
        in_specs=[pl.BlockSpec((tm, D_MODEL), lambda i: (i, 0)),
                  pl.BlockSpec((None, 1, 6 * D_MODEL), mod_map),
                  pl.BlockSpec((1, D_MODEL), lambda i: (0, 0)),
                  pl.BlockSpec((None, D_MODEL, D_FF), lslab, **resident),
                  pl.BlockSpec((None, D_MODEL, D_FF), lslab, **resident),
                  pl.BlockSpec((None, D_FF, D_MODEL), lslab, **resident),
                  pl.BlockSpec((1, D_MODEL), lambda i: (0, 0))],
        out_specs=pl.BlockSpec((tm, D_MODEL), lambda i: (i, 0)),
        out_shape=jax.ShapeDtypeStruct((t, D_MODEL), F32),
        compiler_params=_cparams(("arbitrary",)),
        name="ffn",
    )(x, mod, g2, wg, wu, wd, fg)


def _state_rows(s):
    b = s.shape[0]
    return jnp.transpose(s, (1, 0, 2, 3)).reshape(2, 2 * b, HALF_STATE)


def _state_unrows(s, b):
    return jnp.transpose(s.reshape(2, b, SSM_GROUPS, SSM_STATE), (1, 0, 2, 3))


def kernel(x_prompt, x_sample, c, cache_k, cache_v, state_ssm_re, state_ssm_im, c_ctx, w_ada, b_ada, norm1_g, w_in, conv_w, conv_b, q_norm_g, k_norm_g, ssm_lambda_re, ssm_lambda_im, ssm_b_re, ssm_b_im, ssm_c_re, ssm_c_im, ssm_log_dt, ssm_d, w_glu, b_glu, w_proj_conv, w_proj_attn, w_proj_ssm, w_out, norm2_g, w_ffn_gate, w_ffn_up, w_ffn_down, final_norm_g):
    nb_c, seq_c, _ = x_prompt.shape
    nb_l, seq_l, _ = x_sample.shape
    past = cache_k.shape[2]

    cvec = jnp.zeros((8, D_MODEL), F32).at[0].set(c_ctx).at[1:1 + nb_l].set(c)
    mod_all = _ada_call(cvec, w_ada, b_ada)

    a_re, a_im, bb_re, bb_im = _disc_call(ssm_lambda_re, ssm_lambda_im, ssm_log_dt, ssm_b_re, ssm_b_im)
    bbt, cct, a8 = _ssm_weights(a_re, a_im, bb_re, bb_im, ssm_c_re, ssm_c_im)

    w_in_p = w_in.astype(BF16)
    wb = {k: v.astype(BF16) for k, v in dict(
        w_glu=w_glu, w_proj_conv=w_proj_conv, w_proj_attn=w_proj_attn, w_proj_ssm=w_proj_ssm,
        w_out=w_out, w_ffn_gate=w_ffn_gate, w_ffn_up=w_ffn_up, w_ffn_down=w_ffn_down).items()}

    bd = jnp.asarray(np.kron(np.eye(2 * LANES // HEAD_DIM, dtype=np.float32),
                             np.ones((HEAD_DIM, HEAD_DIM), np.float32)), dtype=BF16)
    tabs = _rope_tables(seq_l)
    fg = final_norm_g.reshape(1, D_MODEL)

    xc = x_prompt.reshape(nb_c * seq_c, D_MODEL)
    xl = x_sample.reshape(nb_l * seq_l, D_MODEL)
    h0_zero = jnp.zeros((2, 2 * nb_c, 2 * HALF_STATE), F32)
    ck = cache_k.reshape(nb_l, DEPTH, past, N_KV_HEADS * HEAD_DIM)
    cv = cache_v.reshape(nb_l, DEPTH, past, N_KV_HEADS * HEAD_DIM)
    srs, sis = [], []
    new_k = jnp.zeros((nb_c, DEPTH, seq_c, N_KV_HEADS * HEAD_DIM), F32)
    new_v = jnp.zeros((nb_c, DEPTH, seq_c, N_KV_HEADS * HEAD_DIM), F32)

    for l in range(DEPTH):
        lp = dict(conv_w=conv_w[l], conv_b=conv_b[l].reshape(1, CONV_W),
                  ssm_d=ssm_d[l].reshape(1, SSM_W), b_glu=b_glu[l].reshape(1, SSM_W),
                  w_glu=wb['w_glu'], w_proj_conv=wb['w_proj_conv'],
                  w_proj_attn=wb['w_proj_attn'], w_proj_ssm=wb['w_proj_ssm'],
                  w_out=wb['w_out'])
        g1 = norm1_g[l].reshape(1, D_MODEL)
        g2 = norm2_g[l].reshape(1, D_MODEL)
        qg = jnp.tile(q_norm_g[l], N_HEADS).reshape(1, D_MODEL)
        kg = jnp.tile(k_norm_g[l], N_KV_HEADS).reshape(1, 2 * LANES)
        mod_c = mod_all[l, 0:1].reshape(1, 1, 6 * D_MODEL)
        mod_l = mod_all[l, 1:1 + nb_l].reshape(nb_l, 1, 6 * D_MODEL)
        final = l == DEPTH - 1

        z, new_k, new_v = _inproj_call(xc, mod_c, g1, w_in_p, qg, kg, bd, None, (new_k, new_v),
                                       layer=l, seq=seq_c, tm=TOKEN_TILE)
        y_attn = _attn_call(z, None, None, layer=l, nb=nb_c, seq=seq_c, past=0,
                            **_attn_tiling(seq_c))
        yf, yb, ht = _ssm_call(z, bbt[l], cct[l], a8[l], h0_zero, nb=nb_c, seq=seq_c,
                               nbg=min(nb_c, SSM_MAX_BATCH))
        xc = _merge_call(xc, mod_c, z, y_attn, yf, yb, lp, layer=l, seq=seq_c, tm=TOKEN_TILE)
        xc = _ffn_call(xc, mod_c, g2, wb['w_ffn_gate'], wb['w_ffn_up'], wb['w_ffn_down'],
                       fg, layer=l, seq=seq_c, tm=FFN_TOKEN_TILE, tf=FFN_CHUNK, final=final)
        srs.append(_state_unrows(ht[:, :, :HALF_STATE], nb_c))
        sis.append(_state_unrows(ht[:, :, HALF_STATE:], nb_c))

        h0 = jnp.concatenate([_state_rows(state_ssm_re[:, l]), _state_rows(state_ssm_im[:, l])], axis=2)
        (z,) = _inproj_call(xl, mod_l, g1, w_in_p, qg, kg, bd, tabs, None,
                            layer=l, seq=seq_l, tm=TOKEN_TILE)
        y_attn = _attn_call(z, ck, cv, layer=l, nb=nb_l, seq=seq_l, past=past,
                            **_attn_tiling(seq_l))
        yf, yb, _ = _ssm_call(z, bbt[l], cct[l], a8[l], h0, nb=nb_l, seq=seq_l,
                              nbg=min(nb_l, SSM_MAX_BATCH))
        xl = _merge_call(xl, mod_l, z, y_attn, yf, yb, lp, layer=l, seq=seq_l, tm=TOKEN_TILE)
        xl = _ffn_call(xl, mod_l, g2, wb['w_ffn_gate'], wb['w_ffn_up'], wb['w_ffn_down'],
                       fg, layer=l, seq=seq_l, tm=FFN_TOKEN_TILE, tf=FFN_CHUNK, final=final)

    y_prompt = xc.reshape(nb_c, seq_c, D_MODEL)
    y_sample = xl.reshape(nb_l, seq_l, D_MODEL)
    kv_shape = (nb_c, DEPTH, seq_c, N_KV_HEADS, HEAD_DIM)
    return (y_prompt, y_sample, new_k.reshape(kv_shape), new_v.reshape(kv_shape),
            jnp.stack(srs, axis=1), jnp.stack(sis, axis=1))
```
